```python
import jax, jax.numpy as jnp
from jax import lax
import numpy as np

D_MODEL = 2048
BATCH = 1
SEQ = 16384
DEPTH = 2

GRID_W = 64
CTX_LEN = 256
N_HEADS = 16
N_KV_HEADS = 4
HEAD_DIM = 128
GQA_GROUP = N_HEADS // N_KV_HEADS
Q_BLOCK = 128
ROPE_THETA = 10000.0
ROPE_FREQS = HEAD_DIM // 4
ATTN_SCALE = HEAD_DIM ** -0.5
FOURIER_DIM = D_MODEL // 2
N_FOURIER_GROUPS = 4
FOURIER_GROUP = FOURIER_DIM // N_FOURIER_GROUPS
D_FF = 5632
N_EXPERTS = 8
TOP_K = 2
D_FF_EXPERT = D_FF // TOP_K
N_DENSE = (DEPTH + 1) // 2
N_MOE = DEPTH // 2
EPS = 1e-6
Q_W = N_HEADS * HEAD_DIM
KV_W = N_KV_HEADS * HEAD_DIM
IN_COLS = Q_W + 2 * KV_W + FOURIER_DIM + 2 * D_MODEL
SPLITS = (Q_W, Q_W + KV_W, Q_W + 2 * KV_W, Q_W + 2 * KV_W + FOURIER_DIM)

kernel_name = "hybrid_gqa_fnet_moe_diffusion_block"


def rms_norm(t, g):
    tf = t.astype(jnp.float32)
    y = tf * lax.rsqrt(jnp.mean(tf * tf, axis=-1, keepdims=True) + EPS)
    return (y * g.astype(jnp.float32)).astype(t.dtype)


def modulate(h, shift, scale):
    return h * (1.0 + scale) + shift


def ada_modulation(cond, w, b):
    m = jax.nn.silu(cond) @ w + b
    return jnp.split(m[:, None, :], 6, axis=-1)


def axial_rope_tables(rows):
    row = jnp.repeat(jnp.arange(rows, dtype=jnp.float32), GRID_W)
    col = jnp.tile(jnp.arange(GRID_W, dtype=jnp.float32), rows)
    inv = ROPE_THETA ** (-jnp.arange(ROPE_FREQS, dtype=jnp.float32) / ROPE_FREQS)
    ar = row[:, None] * inv
    ac = col[:, None] * inv
    ang = jnp.concatenate([ar, ar, ac, ac], axis=-1)
    return jnp.cos(ang), jnp.sin(ang)


def apply_rope(t, cos, sin):
    tf = t.astype(jnp.float32)
    seg = tf.reshape(*t.shape[:-1], 2, 2, ROPE_FREQS)
    rot = jnp.stack([-seg[..., 1, :], seg[..., 0, :]], axis=-2).reshape(t.shape)
    return (tf * cos[:, None, :] + rot * sin[:, None, :]).astype(t.dtype)


def _heads(t, n):
    return t.reshape(*t.shape[:-1], n, HEAD_DIM)


def project(h, w_in, b_gate, q_g, k_g):
    p = h @ w_in
    q, k, v, f, g = jnp.split(p, SPLITS, axis=-1)
    q = rms_norm(_heads(q, N_HEADS), q_g)
    k = rms_norm(_heads(k, N_KV_HEADS), k_g)
    v = _heads(v, N_KV_HEADS)
    return q, k, v, f, g + b_gate


def gqa(q, k, v):
    B, Q = q.shape[:2]
    qg = q.reshape(B, Q, N_KV_HEADS, GQA_GROUP, HEAD_DIM)
    s = jnp.einsum('bqhgd,bkhd->bhgqk', qg, k, preferred_element_type=jnp.float32) * ATTN_SCALE
    p = jax.nn.softmax(s, axis=-1).astype(v.dtype)
    o = jnp.einsum('bhgqk,bkhd->bqhgd', p, v)
    return o.reshape(B, Q, N_HEADS * HEAD_DIM)


def latent_attention(q, k_all, v_all):
    B, S = q.shape[:2]
    nb = S // Q_BLOCK
    qb = jnp.moveaxis(q.reshape(B, nb, Q_BLOCK, N_HEADS, HEAD_DIM), 1, 0)
    o = lax.map(lambda qi: gqa(qi, k_all, v_all), qb)
    return jnp.moveaxis(o, 0, 1).reshape(B, S, N_HEADS * HEAD_DIM)


def fourier_mix(u):
    B, N = u.shape[:2]
    ug = u.astype(jnp.float32).reshape(B, N, N_FOURIER_GROUPS, FOURIER_GROUP)
    f = jnp.fft.fftn(ug, axes=(1, 3), norm='ortho').real
    return f.reshape(B, N, FOURIER_DIM).astype(u.dtype)


def merge(attn, four, gate_logits, w_attn_proj, w_four_proj, w_out):
    gates = jax.nn.sigmoid(gate_logits.astype(jnp.float32)).astype(attn.dtype)
    g_attn, g_four = jnp.split(gates, 2, axis=-1)
    y = g_attn * (attn @ w_attn_proj) + g_four * (four @ w_four_proj)
    return y @ w_out


def swiglu(h, wg, wu, wd):
    return (jax.nn.silu(h @ wg) * (h @ wu)) @ wd


def moe_swiglu(h, router_w, wg, wu, wd):
    logits = jnp.einsum('bsd,de->bse', h, router_w, preferred_element_type=jnp.float32)
    top_v, top_i = lax.top_k(logits, TOP_K)
    w = jax.nn.softmax(top_v, axis=-1)
    combine = jnp.sum(jax.nn.one_hot(top_i, N_EXPERTS, dtype=jnp.float32) * w[..., None], axis=-2)
    out = jnp.zeros_like(h)
    for e in range(N_EXPERTS):
        out = out + combine[..., e:e + 1].astype(h.dtype) * swiglu(h, wg[e], wu[e], wd[e])
    return out


def channel_mixer(l, h, ffn_w_gate, ffn_w_up, ffn_w_down, router_w, moe_w_gate, moe_w_up, moe_w_down):
    i = l // 2
    if l % 2 == 0:
        return swiglu(h, ffn_w_gate[i], ffn_w_up[i], ffn_w_down[i])
    return moe_swiglu(h, router_w[i], moe_w_gate[i], moe_w_up[i], moe_w_down[i])


def setup_inputs(seed: int = 0) -> dict:
    key = jax.random.key(seed)
    ks = jax.random.split(key, 24)
    f32 = jnp.float32

    def nrm(k, shape, fan_in):
        return jax.random.normal(k, shape, f32) * (fan_in ** -0.5)

    def gain(k, shape):
        return 1.0 + 0.02 * jax.random.normal(k, shape, f32)

    def small(k, shape):
        return 0.02 * jax.random.normal(k, shape, f32)

    D = D_MODEL
    return {
        'x': jax.random.normal(ks[0], (BATCH, SEQ, D), f32),
        'c': jax.random.normal(ks[1], (BATCH, D), f32),
        'ctx': jax.random.normal(ks[2], (BATCH, CTX_LEN, D), f32),
        'c_ctx': jax.random.normal(ks[3], (D,), f32),
        'ada_w': nrm(ks[4], (DEPTH, D, 6 * D), D),
        'ada_b': small(ks[5], (DEPTH, 6 * D)),
        'norm_attn_g': gain(ks[6], (DEPTH, D)),
        'norm_ffn_g': gain(ks[7], (DEPTH, D)),
        'w_in': nrm(ks[8], (DEPTH, D, IN_COLS), D),
        'b_gate': small(ks[9], (DEPTH, 2 * D)),
        'q_norm_g': gain(ks[10], (DEPTH, HEAD_DIM)),
        'k_norm_g': gain(ks[11], (DEPTH, HEAD_DIM)),
        'w_attn_proj': nrm(ks[12], (DEPTH, Q_W, D), Q_W),
        'w_four_proj': nrm(ks[13], (DEPTH, FOURIER_DIM, D), FOURIER_DIM),
        'w_out': nrm(ks[14], (DEPTH, D, D), D),
        'ffn_w_gate': nrm(ks[15], (N_DENSE, D, D_FF), D),
        'ffn_w_up': nrm(ks[16], (N_DENSE, D, D_FF), D),
        'ffn_w_down': nrm(ks[17], (N_DENSE, D_FF, D), D_FF),
        'router_w': nrm(ks[18], (N_MOE, D, N_EXPERTS), D),
        'moe_w_gate': nrm(ks[19], (N_MOE, N_EXPERTS, D, D_FF_EXPERT), D),
        'moe_w_up': nrm(ks[20], (N_MOE, N_EXPERTS, D, D_FF_EXPERT), D),
        'moe_w_down': nrm(ks[21], (N_MOE, N_EXPERTS, D_FF_EXPERT, D), D_FF_EXPERT),
        'final_norm_g': gain(ks[22], (D,)),
    }


def reference(x, c, ctx, c_ctx, ada_w, ada_b, norm_attn_g, norm_ffn_g, w_in, b_gate, q_norm_g, k_norm_g,
              w_attn_proj, w_four_proj, w_out, ffn_w_gate, ffn_w_up, ffn_w_down, router_w, moe_w_gate,
              moe_w_up, moe_w_down, final_norm_g):
    S = x.shape[1]
    ROWS = S // GRID_W
    cos, sin = axial_rope_tables(ROWS)
    xc = ctx
    for l in range(DEPTH):
        last = l == DEPTH - 1
        sh1, sc1, ga1, sh2, sc2, ga2 = ada_modulation(c, ada_w[l], ada_b[l])
        mod_c = ada_modulation(c_ctx[None, :], ada_w[l], ada_b[l])

        hc = modulate(rms_norm(xc, norm_attn_g[l]), mod_c[0], mod_c[1])
        if last:
            kc, vc = jnp.split(hc @ w_in[l][:, Q_W:Q_W + 2 * KV_W], 2, axis=-1)
            kc = rms_norm(_heads(kc, N_KV_HEADS), k_norm_g[l])
            vc = _heads(vc, N_KV_HEADS)
        else:
            qc, kc, vc, fc, gc = project(hc, w_in[l], b_gate[l], q_norm_g[l], k_norm_g[l])
            yc = merge(gqa(qc, kc, vc), fourier_mix(fc), gc, w_attn_proj[l], w_four_proj[l], w_out[l])
            xc_next = xc + mod_c[2] * yc
            hc2 = modulate(rms_norm(xc_next, norm_ffn_g[l]), mod_c[3], mod_c[4])
            xc_next = xc_next + mod_c[5] * channel_mixer(l, hc2, ffn_w_gate, ffn_w_up, ffn_w_down,
                                                         router_w, moe_w_gate, moe_w_up, moe_w_down)

        h = modulate(rms_norm(x, norm_attn_g[l]), sh1, sc1)
        q, k, v, f, g = project(h, w_in[l], b_gate[l], q_norm_g[l], k_norm_g[l])
        q = apply_rope(q, cos, sin)
        k = apply_rope(k, cos, sin)
        k_all = jnp.concatenate([kc, k], axis=1)
        v_all = jnp.concatenate([vc, v], axis=1)
        y = merge(latent_attention(q, k_all, v_all), fourier_mix(f), g, w_attn_proj[l], w_four_proj[l], w_out[l])
        x = x + ga1 * y
        h2 = modulate(rms_norm(x, norm_ffn_g[l]), sh2, sc2)
        x = x + ga2 * channel_mixer(l, h2, ffn_w_gate, ffn_w_up, ffn_w_down,
                                    router_w, moe_w_gate, moe_w_up, moe_w_down)
        if not last:
            xc = xc_next
    return rms_norm(x, final_norm_g)
```

```python
import functools
import math

import numpy as np
import jax
import jax.numpy as jnp
from jax import lax
from jax.experimental import pallas as pl
from jax.experimental.pallas import tpu as pltpu

F32 = jnp.float32
BF16 = jnp.bfloat16

HEAD_DIM = 128
GRID_W = 64
ROPE_THETA = 10000.0
ROPE_FREQS = HEAD_DIM // 4
ATTN_SCALE = HEAD_DIM ** -0.5
N_FOURIER_GROUPS = 4
TOP_K = 2
EPS = 1e-6

LANES = 128
SUBLANES = 8
VMEM_LIMIT_BYTES = 58 * 1024 * 1024
NEG_INF = -1e30


def _cparams(*sem):
    return pltpu.CompilerParams(dimension_semantics=sem, vmem_limit_bytes=VMEM_LIMIT_BYTES)


def _norm_mod(x, g, shift, scale):
    ms = jnp.mean(x * x, axis=-1, keepdims=True)
    return (x * lax.rsqrt(ms + EPS) * g) * (1.0 + scale) + shift


def _silu(x):
    return x * jax.nn.sigmoid(x)


def _ada_kernel(cond_ref, w_ref, b_ref, o_ref):
    s = _silu(cond_ref[...]).astype(BF16)
    w = w_ref[0].astype(BF16)
    o_ref[0] = jnp.dot(s, w, preferred_element_type=F32) + b_ref[0]


def ada_modulation(cond, ada_w, ada_b, tn=1024):
    L, D, N = ada_w.shape
    return pl.pallas_call(
        _ada_kernel,
        grid=(L, N // tn),
        in_specs=[
            pl.BlockSpec((SUBLANES, D), lambda l, j: (0, 0)),
            pl.BlockSpec((1, D, tn), lambda l, j: (l, 0, j)),
            pl.BlockSpec((1, 1, tn), lambda l, j: (l, 0, j)),
        ],
        out_specs=pl.BlockSpec((1, SUBLANES, tn), lambda l, j: (l, 0, j)),
        out_shape=jax.ShapeDtypeStruct((L, SUBLANES, N), F32),
        compiler_params=_cparams("parallel", "parallel"),
        name="ada_modulation",
    )(cond, ada_w, ada_b.reshape(L, 1, N))


def _inproj_kernel(x_ref, vec_ref, w_ref, bg_ref, qkg_ref, cos_ref, sa_ref, sb_ref,
                   qkv_ref, f_ref, gates_ref, hs_ref, acc_ref, *, n_q_tiles, kv_w, use_rope):
    j = pl.program_id(1)

    @pl.when(j == 0)
    def _():
        hs_ref[...] = _norm_mod(x_ref[...], vec_ref[0:1], vec_ref[1:2], vec_ref[2:3]).astype(BF16)

    acc_ref[...] = jnp.dot(hs_ref[...], w_ref[...], preferred_element_type=F32)
    tn = acc_ref.shape[1]

    def head_norm(c0, g, scale):
        t = acc_ref[:, c0:c0 + HEAD_DIM]
        ms = jnp.mean(t * t, axis=-1, keepdims=True)
        y = t * lax.rsqrt(ms + EPS) * g
        if use_rope:
            y = (y * cos_ref[...] + pltpu.roll(y, HEAD_DIM - ROPE_FREQS, 1) * sa_ref[...]
                 + pltpu.roll(y, ROPE_FREQS, 1) * sb_ref[...])
        if scale != 1.0:
            y = y * scale
        qkv_ref[:, c0:c0 + HEAD_DIM] = y.astype(BF16)

    @pl.when(j < n_q_tiles)
    def _():
        for h in range(tn // HEAD_DIM):
            head_norm(h * HEAD_DIM, qkg_ref[0:1], ATTN_SCALE)

    @pl.when(j == n_q_tiles)
    def _():
        for h in range(kv_w // HEAD_DIM):
            head_norm(h * HEAD_DIM, qkg_ref[1:2], 1.0)
        qkv_ref[:, kv_w:] = acc_ref[:, kv_w:].astype(BF16)

    @pl.when(j == n_q_tiles + 1)
    def _():
        f_ref[...] = acc_ref[...].astype(BF16)

    @pl.when(j > n_q_tiles + 1)
    def _():
        gates_ref[...] = jax.nn.sigmoid(acc_ref[...] + bg_ref[...]).astype(BF16)


def in_projection(x, vec, w_in, b_gate, qkg, rope, *, q_w, kv_w, four_w, tm):
    R, D = x.shape
    tn = 2 * kv_w
    assert q_w % tn == 0 and four_w == tn and (2 * D) % tn == 0 and R % tm == 0
    n_q = q_w // tn
    n_g = (2 * D) // tn
    nj = n_q + 2 + n_g
    use_rope = rope is not None
    if rope is None:
        rope = (jnp.zeros((SUBLANES, HEAD_DIM), F32),) * 3
        rope_spec = pl.BlockSpec((SUBLANES, HEAD_DIM), lambda i, j: (0, 0))
    else:
        rope_spec = pl.BlockSpec((tm, HEAD_DIM), lambda i, j: (i, 0))
    kern = functools.partial(_inproj_kernel, n_q_tiles=n_q, kv_w=kv_w, use_rope=use_rope)
    return pl.pallas_call(
        kern,
        grid=(R // tm, nj),
        in_specs=[
            pl.BlockSpec((tm, D), lambda i, j: (i, 0)),
            pl.BlockSpec((SUBLANES, D), lambda i, j: (0, 0)),
            pl.BlockSpec((D, tn), lambda i, j: (0, j)),
            pl.BlockSpec((1, tn), lambda i, j: (0, jnp.clip(j - (n_q + 2), 0, n_g - 1))),
            pl.BlockSpec((SUBLANES, HEAD_DIM), lambda i, j: (0, 0)),
            rope_spec, rope_spec, rope_spec,
        ],
        out_specs=[
            pl.BlockSpec((tm, tn), lambda i, j: (i, jnp.minimum(j, n_q))),
            pl.BlockSpec((tm, tn), lambda i, j: (i, 0)),
            pl.BlockSpec((tm, tn), lambda i, j: (i, jnp.clip(j - (n_q + 2), 0, n_g - 1))),
        ],
        out_shape=[
            jax.ShapeDtypeStruct((R, q_w + 2 * kv_w), BF16),
            jax.ShapeDtypeStruct((R, four_w), BF16),
            jax.ShapeDtypeStruct((R, 2 * D), BF16),
        ],
        scratch_shapes=[pltpu.VMEM((tm, D), BF16), pltpu.VMEM((tm, tn), F32)],
        compiler_params=_cparams("parallel", "arbitrary"),
        name="in_projection",
    )(x, vec, w_in, b_gate, qkg, *rope)


def _attn_kernel(*refs, tq, group, nk, has_extra):
    if has_extra:
        q_ref, k_ref, v_ref, kx_ref, vx_ref, o_ref, qs_ref, m_ref, l_ref, acc_ref = refs
    else:
        q_ref, k_ref, v_ref, o_ref, qs_ref, m_ref, l_ref, acc_ref = refs
    kb = pl.program_id(2)

    @pl.when(kb == 0)
    def _():
        for g in range(group):
            qs_ref[g * tq:(g + 1) * tq, :] = q_ref[:, g * HEAD_DIM:(g + 1) * HEAD_DIM]
        m_ref[...] = jnp.full(m_ref.shape, NEG_INF, F32)
        l_ref[...] = jnp.zeros(l_ref.shape, F32)
        acc_ref[...] = jnp.zeros(acc_ref.shape, F32)

    def update(k, v):
        s = lax.dot_general(qs_ref[...], k, (((1,), (1,)), ((), ())), preferred_element_type=F32)
        m_prev = m_ref[...]
        m_new = jnp.maximum(m_prev, jnp.max(s, axis=-1, keepdims=True))
        alpha = jnp.exp(m_prev - m_new)
        p = jnp.exp(s - m_new)
        l_ref[...] = alpha * l_ref[...] + jnp.sum(p, axis=-1, keepdims=True)
        acc_ref[...] = alpha * acc_ref[...] + jnp.dot(p.astype(BF16), v, preferred_element_type=F32)
        m_ref[...] = m_new

    update(k_ref[...], v_ref[...])

    @pl.when(kb == nk - 1)
    def _():
        if has_extra:
            update(kx_ref[...], vx_ref[...])
        o = acc_ref[...] / l_ref[...]
        for g in range(group):
            o_ref[:, g * HEAD_DIM:(g + 1) * HEAD_DIM] = o[g * tq:(g + 1) * tq].astype(BF16)


def attention(qkv, qkv_extra, *, n_heads, n_kv, tq, tk):
    R = qkv.shape[0]
    group = n_heads // n_kv
    assert R % tq == 0 and R % tk == 0
    nk = R // tk
    has_extra = qkv_extra is not None
    in_specs = [
        pl.BlockSpec((tq, group * HEAD_DIM), lambda h, i, kb: (i, h)),
        pl.BlockSpec((tk, HEAD_DIM), lambda h, i, kb: (kb, n_heads + h)),
        pl.BlockSpec((tk, HEAD_DIM), lambda h, i, kb: (kb, n_heads + n_kv + h)),
    ]
    args = [qkv, qkv, qkv]
    if has_extra:
        rx = qkv_extra.shape[0]
        in_specs += [
            pl.BlockSpec((rx, HEAD_DIM), lambda h, i, kb: (0, n_heads + h)),
            pl.BlockSpec((rx, HEAD_DIM), lambda h, i, kb: (0, n_heads + n_kv + h)),
        ]
        args += [qkv_extra, qkv_extra]
    kern = functools.partial(_attn_kernel, tq=tq, group=group, nk=nk, has_extra=has_extra)
    return pl.pallas_call(
        kern,
        grid=(n_kv, R // tq, nk),
        in_specs=in_specs,
        out_specs=pl.BlockSpec((tq, group * HEAD_DIM), lambda h, i, kb: (i, h)),
        out_shape=jax.ShapeDtypeStruct((R, n_heads * HEAD_DIM), BF16),
        scratch_shapes=[
            pltpu.VMEM((group * tq, HEAD_DIM), BF16),
            pltpu.VMEM((group * tq, 1), F32),
            pltpu.VMEM((group * tq, 1), F32),
            pltpu.VMEM((group * tq, HEAD_DIM), F32),
        ],
        compiler_params=_cparams("parallel", "parallel", "arbitrary"),
        name="attention",
    )(*args)


def _dft_mats(n, scale):
    k = np.arange(n)
    ang = 2.0 * np.pi * ((k[:, None] * k[None, :]) % n) / n
    return (jnp.asarray(np.cos(ang) * scale, BF16), jnp.asarray(np.sin(ang) * scale, BF16))


def _four_small_kernel(u_ref, cn_ref, sn_ref, cc_ref, sc_ref, o_ref, *, gc):
    u = u_ref[...]
    for g in range(u.shape[1] // gc):
        ug = u[:, g * gc:(g + 1) * gc]
        a = jnp.dot(ug, cc_ref[...], preferred_element_type=F32).astype(BF16)
        b = jnp.dot(ug, sc_ref[...], preferred_element_type=F32).astype(BF16)
        o = (jnp.dot(cn_ref[...], a, preferred_element_type=F32)
             - jnp.dot(sn_ref[...], b, preferred_element_type=F32))
        o_ref[:, g * gc:(g + 1) * gc] = o.astype(BF16)


def fourier_mix_small(u):
    n, c = u.shape
    gc = c // N_FOURIER_GROUPS
    cn, sn = _dft_mats(n, n ** -0.5)
    cc, sc = _dft_mats(gc, gc ** -0.5)
    full = lambda a: pl.BlockSpec(a.shape, lambda: (0,) * a.ndim)
    return pl.pallas_call(
        functools.partial(_four_small_kernel, gc=gc),
        in_specs=[full(u), full(cn), full(sn), full(cc), full(sc)],
        out_specs=pl.BlockSpec((n, c), lambda: (0, 0)),
        out_shape=jax.ShapeDtypeStruct((n, c), BF16),
        compiler_params=pltpu.CompilerParams(vmem_limit_bytes=VMEM_LIMIT_BYTES),
        name="fourier_small",
    )(u, cn, sn, cc, sc)


def _four_stage1_kernel(u_ref, ca_ref, sa_ref, cw_ref, sw_ref, yr_ref, yi_ref, *, c, tb):
    u = u_ref[...]
    yr = jnp.dot(ca_ref[...], u, preferred_element_type=F32)
    yi = -jnp.dot(sa_ref[...], u, preferred_element_type=F32)
    for b in range(tb):
        cw = cw_ref[:, b * LANES:(b + 1) * LANES]
        sw = sw_ref[:, b * LANES:(b + 1) * LANES]
        for l in range(c // LANES):
            sl = slice(b * c + l * LANES, b * c + (l + 1) * LANES)
            r, i = yr[:, sl], yi[:, sl]
            yr_ref[:, sl] = (r * cw + i * sw).astype(BF16)
            yi_ref[:, sl] = (i * cw - r * sw).astype(BF16)


def _four_stage2_kernel(yr_ref, yi_ref, cb_ref, sb_ref, cc_ref, sc_ref, o_ref, *, nb, c, gc, tk1):
    for t in range(tk1):
        yr = yr_ref[t * nb:(t + 1) * nb, :]
        yi = yi_ref[t * nb:(t + 1) * nb, :]
        pr = (jnp.dot(cb_ref[...], yr, preferred_element_type=F32)
              + jnp.dot(sb_ref[...], yi, preferred_element_type=F32)).astype(BF16)
        pi = (jnp.dot(cb_ref[...], yi, preferred_element_type=F32)
              - jnp.dot(sb_ref[...], yr, preferred_element_type=F32)).astype(BF16)
        for g in range(c // gc):
            sl = slice(g * gc, (g + 1) * gc)
            o = (jnp.dot(pr[:, sl], cc_ref[...], preferred_element_type=F32)
                 + jnp.dot(pi[:, sl], sc_ref[...], preferred_element_type=F32))
            o_ref[:, t * c + g * gc:t * c + (g + 1) * gc] = o.astype(BF16)


def fourier_mix_large(u, *, na=128, tb=4, tk1=4):
    n, c = u.shape
    nb = n // na
    assert na * nb == n and nb % tb == 0 and na % tk1 == 0 and nb % SUBLANES == 0
    gc = c // N_FOURIER_GROUPS
    ca, sa = _dft_mats(na, na ** -0.5)
    cb, sb = _dft_mats(nb, nb ** -0.5)
    cc, sc = _dft_mats(gc, gc ** -0.5)
    k1 = lax.broadcasted_iota(jnp.int32, (na, nb * LANES), 0)
    b = lax.broadcasted_iota(jnp.int32, (na, nb * LANES), 1) // LANES
    ang = ((k1 * b) % n).astype(F32) * (2.0 * math.pi / n)
    cw, sw = jnp.cos(ang), jnp.sin(ang)
    const = lambda a: pl.BlockSpec(a.shape, lambda i: (0,) * a.ndim)
    yr, yi = pl.pallas_call(
        functools.partial(_four_stage1_kernel, c=c, tb=tb),
        grid=(nb // tb,),
        in_specs=[
            pl.BlockSpec((na, tb * c), lambda i: (0, i)),
            const(ca), const(sa),
            pl.BlockSpec((na, tb * LANES), lambda i: (0, i)),
            pl.BlockSpec((na, tb * LANES), lambda i: (0, i)),
        ],
        out_specs=[pl.BlockSpec((na, tb * c), lambda i: (0, i))] * 2,
        out_shape=[jax.ShapeDtypeStruct((na, nb * c), BF16)] * 2,
        compiler_params=_cparams("parallel"),
        name="fourier_stage1",
    )(u.reshape(na, nb * c), ca, sa, cw, sw)
    out = pl.pallas_call(
        functools.partial(_four_stage2_kernel, nb=nb, c=c, gc=gc, tk1=tk1),
        grid=(na // tk1,),
        in_specs=[
            pl.BlockSpec((tk1 * nb, c), lambda i: (i, 0)),
            pl.BlockSpec((tk1 * nb, c), lambda i: (i, 0)),
            const(cb), const(sb), const(cc), const(sc),
        ],
        out_specs=pl.BlockSpec((nb, tk1 * c), lambda i: (0, i)),
        out_shape=jax.ShapeDtypeStruct((nb, na * c), BF16),
        compiler_params=_cparams("parallel"),
        name="fourier_stage2",
    )(yr.reshape(na * nb, c), yi.reshape(na * nb, c), cb, sb, cc, sc)
    return out.reshape(n, c)


def fourier_mix(u):
    n = u.shape[0]
    if n % (128 * SUBLANES) == 0 and n >= 128 * 16:
        return fourier_mix_large(u)
    return fourier_mix_small(u)


def _merge_kernel(x_ref, attn_ref, four_ref, gates_ref, vec_ref, wa_ref, wf_ref, wo_ref, o_ref):
    d = x_ref.shape[1]
    a = jnp.dot(attn_ref[...], wa_ref[...], preferred_element_type=F32)
    f = jnp.dot(four_ref[...], wf_ref[...], preferred_element_type=F32)
    y = gates_ref[:, :d].astype(F32) * a + gates_ref[:, d:].astype(F32) * f
    y = jnp.dot(y.astype(BF16), wo_ref[...], preferred_element_type=F32)
    o_ref[...] = x_ref[...] + vec_ref[0:1] * y


def merge(x, attn, four, gates, vec, wa, wf, wo, *, tm):
    R, D = x.shape
    row = lambda w: pl.BlockSpec((tm, w), lambda i: (i, 0))
    const = lambda a: pl.BlockSpec(a.shape, lambda i: (0, 0), pipeline_mode=pl.Buffered(1))
    return pl.pallas_call(
        _merge_kernel,
        grid=(R // tm,),
        in_specs=[row(D), row(attn.shape[1]), row(four.shape[1]), row(gates.shape[1]),
                  pl.BlockSpec((SUBLANES, D), lambda i: (0, 0)), const(wa), const(wf), const(wo)],
        out_specs=row(D),
        out_shape=jax.ShapeDtypeStruct((R, D), F32),
        compiler_params=_cparams("parallel"),
        name="merge",
    )(x, attn, four, gates, vec, wa, wf, wo)


def _ffn_kernel(x_ref, vec_ref, wg_ref, wu_ref, wd_ref, o_ref, hs_ref, acc_ref):
    j = pl.program_id(1)

    @pl.when(j == 0)
    def _():
        hs_ref[...] = _norm_mod(x_ref[...], vec_ref[0:1], vec_ref[1:2], vec_ref[2:3]).astype(BF16)
        acc_ref[...] = jnp.zeros(acc_ref.shape, F32)

    h = hs_ref[...]
    g = jnp.dot(h, wg_ref[...], preferred_element_type=F32)
    u = jnp.dot(h, wu_ref[...], preferred_element_type=F32)
    acc_ref[...] += jnp.dot((_silu(g) * u).astype(BF16), wd_ref[...], preferred_element_type=F32)

    @pl.when(j == pl.num_programs(1) - 1)
    def _():
        o_ref[...] = x_ref[...] + vec_ref[3:4] * acc_ref[...]


def dense_ffn(x, vec, wg, wu, wd, *, tm, tf):
    R, D = x.shape
    F = wg.shape[1]
    assert R % tm == 0 and F % tf == 0
    return pl.pallas_call(
        _ffn_kernel,
        grid=(R // tm, F // tf),
        in_specs=[
            pl.BlockSpec((tm, D), lambda i, j: (i, 0)),
            pl.BlockSpec((SUBLANES, D), lambda i, j: (0, 0)),
            pl.BlockSpec((D, tf), lambda i, j: (0, j)),
            pl.BlockSpec((D, tf), lambda i, j: (0, j)),
            pl.BlockSpec((tf, D), lambda i, j: (j, 0)),
        ],
        out_specs=pl.BlockSpec((tm, D), lambda i, j: (i, 0)),
        out_shape=jax.ShapeDtypeStruct((R, D), F32),
        scratch_shapes=[pltpu.VMEM((tm, D), BF16), pltpu.VMEM((tm, D), F32)],
        compiler_params=_cparams("parallel", "arbitrary"),
        name="dense_ffn",
    )(x, vec, wg, wu, wd)


def _router_kernel(x_ref, vec_ref, rw_ref, h_ref, meta_ref, cnt_ref, run_ref, *, n_exp):
    i = pl.program_id(0)

    @pl.when(i == 0)
    def _():
        run_ref[...] = jnp.zeros(run_ref.shape, F32)

    h = _norm_mod(x_ref[...], vec_ref[0:1], vec_ref[1:2], vec_ref[2:3])
    h_ref[...] = h
    tm = h.shape[0]
    logits = [jnp.sum(h * rw_ref[e:e + 1], axis=-1, keepdims=True) for e in range(n_exp)]
    m1 = logits[0]
    for e in range(1, n_exp):
        m1 = jnp.maximum(m1, logits[e])
    i1 = jnp.full((tm, 1), n_exp, jnp.int32)
    for e in reversed(range(n_exp)):
        i1 = jnp.where(logits[e] == m1, e, i1)
    m2 = jnp.full((tm, 1), NEG_INF, F32)
    for e in range(n_exp):
        m2 = jnp.maximum(m2, jnp.where(i1 == e, NEG_INF, logits[e]))
    i2 = jnp.full((tm, 1), n_exp, jnp.int32)
    for e in reversed(range(n_exp)):
        i2 = jnp.where((logits[e] == m2) & (i1 != e), e, i2)
    e2 = jnp.exp(m2 - m1)
    w1 = 1.0 / (1.0 + e2)
    w2 = e2 / (1.0 + e2)

    lane = lax.broadcasted_iota(jnp.int32, (tm, LANES), 1)
    hit1 = lane == i1
    hit2 = lane == i2
    onehot = (hit1 | hit2).astype(F32)
    r = lax.broadcasted_iota(jnp.int32, (tm, tm), 0)
    cidx = lax.broadcasted_iota(jnp.int32, (tm, tm), 1)
    ltri = (cidx < r).astype(BF16)
    before = jnp.dot(ltri, onehot.astype(BF16), preferred_element_type=F32) + run_ref[0:1]
    rank1 = jnp.sum(jnp.where(hit1, before, 0.0), axis=-1, keepdims=True)
    rank2 = jnp.sum(jnp.where(hit2, before, 0.0), axis=-1, keepdims=True)
    run_ref[0:1] = run_ref[0:1] + jnp.sum(onehot, axis=0, keepdims=True)

    meta = jnp.where(lane == 0, i1.astype(F32), 0.0)
    meta = jnp.where(lane == 1, i2.astype(F32), meta)
    meta = jnp.where(lane == 2, w1, meta)
    meta = jnp.where(lane == 3, w2, meta)
    meta = jnp.where(lane == 4, rank1, meta)
    meta = jnp.where(lane == 5, rank2, meta)
    meta_ref[...] = meta
    cnt_ref[...] = run_ref[...]


def moe_router(x, vec, router_w, *, tm):
    R, D = x.shape
    n_exp = router_w.shape[1]
    rw = jnp.zeros((SUBLANES * ((n_exp + SUBLANES - 1) // SUBLANES), D), F32).at[:n_exp].set(router_w.T)
    return pl.pallas_call(
        functools.partial(_router_kernel, n_exp=n_exp),
        grid=(R // tm,),
        in_specs=[
            pl.BlockSpec((tm, D), lambda i: (i, 0)),
            pl.BlockSpec((SUBLANES, D), lambda i: (0, 0)),
            pl.BlockSpec(rw.shape, lambda i: (0, 0)),
        ],
        out_specs=[
            pl.BlockSpec((tm, D), lambda i: (i, 0)),
            pl.BlockSpec((tm, LANES), lambda i: (i, 0)),
            pl.BlockSpec((SUBLANES, LANES), lambda i: (0, 0)),
        ],
        out_shape=[
            jax.ShapeDtypeStruct((R, D), F32),
            jax.ShapeDtypeStruct((R, LANES), F32),
            jax.ShapeDtypeStruct((SUBLANES, LANES), F32),
        ],
        scratch_shapes=[pltpu.VMEM((SUBLANES, LANES), F32)],
        compiler_params=_cparams("arbitrary"),
        name="moe_router",
    )(x, vec, rw)


def _row_copy(src_ref, s, dst_ref, d, sem):
    return pltpu.make_async_copy(src_ref.at[pl.ds(s, 1)], dst_ref.at[pl.ds(d, 1)], sem)


def _dispatch_kernel(dest_ref, h_ref, xs_in_ref, xs_ref, sem):
    del xs_in_ref
    tm = h_ref.shape[0]
    base = pl.program_id(0) * tm

    def start(t, c):
        for s in range(TOP_K):
            _row_copy(h_ref, t, xs_ref, dest_ref[TOP_K * (base + t) + s], sem).start()
        return c

    def wait(t, c):
        for s in range(TOP_K):
            _row_copy(h_ref, t, xs_ref, dest_ref[TOP_K * (base + t) + s], sem).wait()
        return c

    lax.fori_loop(0, tm, start, 0)
    lax.fori_loop(0, tm, wait, 0)


def moe_dispatch(h, dest, n_rows, *, tm):
    R, D = h.shape
    return pl.pallas_call(
        _dispatch_kernel,
        grid_spec=pltpu.PrefetchScalarGridSpec(
            num_scalar_prefetch=1,
            grid=(R // tm,),
            in_specs=[pl.BlockSpec((tm, D), lambda i, dest: (i, 0)),
                      pl.BlockSpec(memory_space=pl.ANY)],
            out_specs=pl.BlockSpec(memory_space=pl.ANY),
            scratch_shapes=[pltpu.SemaphoreType.DMA(())],
        ),
        out_shape=jax.ShapeDtypeStruct((n_rows, D), F32),
        input_output_aliases={2: 0},
        compiler_params=_cparams("arbitrary"),
        name="moe_dispatch",
    )(dest, h, jnp.zeros((n_rows, D), F32))


def _gffn_kernel(te_ref, nv_ref, xs_ref, wg_ref, wu_ref, wd_ref, ys_ref, hb_ref, acc_ref):
    t = pl.program_id(0)
    j = pl.program_id(1)
    valid = t < nv_ref[0]

    @pl.when(valid & (j == 0))
    def _():
        hb_ref[...] = xs_ref[...].astype(BF16)
        acc_ref[...] = jnp.zeros(acc_ref.shape, F32)

    @pl.when(valid)
    def _():
        h = hb_ref[...]
        g = jnp.dot(h, wg_ref[0], preferred_element_type=F32)
        u = jnp.dot(h, wu_ref[0], preferred_element_type=F32)
        acc_ref[...] += jnp.dot((_silu(g) * u).astype(BF16), wd_ref[0], preferred_element_type=F32)

    last = j == pl.num_programs(1) - 1

    @pl.when(valid & last)
    def _():
        ys_ref[...] = acc_ref[...]

    @pl.when(jnp.logical_not(valid) & last)
    def _():
        ys_ref[...] = jnp.zeros(ys_ref.shape, F32)


def moe_grouped_ffn(xs, tile_expert, n_valid, wg, wu, wd, *, tm, tf):
    n_rows, D = xs.shape
    F = wg.shape[2]
    nf = F // tf
    assert n_rows % tm == 0 and F % tf == 0

    def wcol(t, j, te, nv):
        return (te[t], 0, jnp.where(t < nv[0], j, nf - 1))

    def wrow(t, j, te, nv):
        return (te[t], jnp.where(t < nv[0], j, nf - 1), 0)

    return pl.pallas_call(
        _gffn_kernel,
        grid_spec=pltpu.PrefetchScalarGridSpec(
            num_scalar_prefetch=2,
            grid=(n_rows // tm, nf),
            in_specs=[
                pl.BlockSpec((tm, D), lambda t, j, te, nv: (t, 0)),
                pl.BlockSpec((1, D, tf), wcol),
                pl.BlockSpec((1, D, tf), wcol),
                pl.BlockSpec((1, tf, D), wrow),
            ],
            out_specs=pl.BlockSpec((tm, D), lambda t, j, te, nv: (t, 0)),
            scratch_shapes=[pltpu.VMEM((tm, D), BF16), pltpu.VMEM((tm, D), F32)],
        ),
        out_shape=jax.ShapeDtypeStruct((n_rows, D), F32),
        compiler_params=_cparams("parallel", "arbitrary"),
        name="moe_grouped_ffn",
    )(tile_expert, n_valid, xs, wg, wu, wd)


def _combine_kernel(dest_ref, x_ref, meta_ref, vec_ref, ys_ref, o_ref, b0_ref, b1_ref, sem, *, final_norm):
    tm = x_ref.shape[0]
    base = pl.program_id(0) * tm
    bufs = (b0_ref, b1_ref)

    def start(t, c):
        for s in range(TOP_K):
            _row_copy(ys_ref, dest_ref[TOP_K * (base + t) + s], bufs[s], t, sem).start()
        return c

    def wait(t, c):
        for s in range(TOP_K):
            _row_copy(ys_ref, dest_ref[TOP_K * (base + t) + s], bufs[s], t, sem).wait()
        return c

    lax.fori_loop(0, tm, start, 0)
    lax.fori_loop(0, tm, wait, 0)
    y = meta_ref[:, 2:3] * b0_ref[...] + meta_ref[:, 3:4] * b1_ref[...]
    out = x_ref[...] + vec_ref[3:4] * y
    if final_norm:
        ms = jnp.mean(out * out, axis=-1, keepdims=True)
        out = out * lax.rsqrt(ms + EPS) * vec_ref[4:5]
    o_ref[...] = out


def moe_combine(x, meta, vec, ys, dest, *, tm, final_norm):
    R, D = x.shape
    return pl.pallas_call(
        functools.partial(_combine_kernel, final_norm=final_norm),
        grid_spec=pltpu.PrefetchScalarGridSpec(
            num_scalar_prefetch=1,
            grid=(R // tm,),
            in_specs=[
                pl.BlockSpec((tm, D), lambda i, dest: (i, 0)),
                pl.BlockSpec((tm, LANES), lambda i, dest: (i, 0)),
                pl.BlockSpec((SUBLANES, D), lambda i, dest: (0, 0)),
                pl.BlockSpec(memory_space=pl.ANY),
            ],
            out_specs=pl.BlockSpec((tm, D), lambda i, dest: (i, 0)),
            scratch_shapes=[pltpu.VMEM((tm, D), F32), pltpu.VMEM((tm, D), F32),
                            pltpu.SemaphoreType.DMA(())],
        ),
        out_shape=jax.ShapeDtypeStruct((R, D), F32),
        compiler_params=_cparams("arbitrary"),
        name="moe_combine",
    )(dest, x, meta, vec, ys)


def moe_layer(x, vec, router_w, wg, wu, wd, *, final_norm, tm_route, tm_rows, tm_group, tf):
    R, D = x.shape
    n_exp = router_w.shape[1]
    h, meta, counts = moe_router(x, vec, router_w, tm=tm_route)
    counts = counts[0, :n_exp].astype(jnp.int32)
    padded = ((counts + tm_group - 1) // tm_group) * tm_group
    ends = jnp.cumsum(padded)
    offsets = ends - padded
    eid = meta[:, 0:TOP_K].astype(jnp.int32)
    rank = meta[:, 4:4 + TOP_K].astype(jnp.int32)
    dest = (offsets[eid] + rank).reshape(-1)
    n_tiles = (TOP_K * R) // tm_group + n_exp
    tile_start = jnp.arange(n_tiles, dtype=jnp.int32) * tm_group
    n_valid = ends[-1] // tm_group
    tile_expert = jnp.sum(tile_start[:, None] >= ends[None, :], axis=1).astype(jnp.int32)
    last_expert = jnp.take(tile_expert, jnp.maximum(n_valid - 1, 0))
    tile_expert = jnp.where(jnp.arange(n_tiles) < n_valid, tile_expert, last_expert)
    xs = moe_dispatch(h, dest, n_tiles * tm_group, tm=tm_rows)
    ys = moe_grouped_ffn(xs, tile_expert, n_valid.reshape(1).astype(jnp.int32), wg, wu, wd,
                         tm=tm_group, tf=tf)
    return moe_combine(x, meta, vec, ys, dest, tm=tm_rows, final_norm=final_norm)


def _final_norm_kernel(x_ref, g_ref, o_ref):
    x = x_ref[...]
    ms = jnp.mean(x * x, axis=-1, keepdims=True)
    o_ref[...] = x * lax.rsqrt(ms + EPS) * g_ref[0:1]


def final_norm(x, vec, *, tm):
    R, D = x.shape
    return pl.pallas_call(
        _final_norm_kernel,
        grid=(R // tm,),
        in_specs=[pl.BlockSpec((tm, D), lambda i: (i, 0)), pl.BlockSpec((SUBLANES, D), lambda i: (0, 0))],
        out_specs=pl.BlockSpec((tm, D), lambda i: (i, 0)),
        out_shape=jax.ShapeDtypeStruct((R, D), F32),
        compiler_params=_cparams("parallel"),
        name="final_norm",
    )(x, vec)


def _rope_tables(n_tokens):
    rows = n_tokens // GRID_W
    row = jnp.repeat(jnp.arange(rows, dtype=F32), GRID_W)
    col = jnp.tile(jnp.arange(GRID_W, dtype=F32), rows)
    inv = ROPE_THETA ** (-jnp.arange(ROPE_FREQS, dtype=F32) / ROPE_FREQS)
    ar = row[:, None] * inv
    ac = col[:, None] * inv
    ang = jnp.concatenate([ar, ar, ac, ac], axis=-1)
    cos, sin = jnp.cos(ang), jnp.sin(ang)
    lane = jnp.arange(HEAD_DIM) % (2 * ROPE_FREQS)
    lo = lane < ROPE_FREQS
    return cos, jnp.where(lo, -sin, 0.0), jnp.where(lo, 0.0, sin)


def _vec(*rows):
    d = rows[0].shape[-1]
    out = jnp.zeros((SUBLANES, d), F32)
    for i, r in enumerate(rows):
        out = out.at[i].set(r.reshape(d))
    return out


def _tile(n, pref):
    t = pref
    while n % t:
        t //= 2
    return t


def kernel(x, c, ctx, c_ctx, ada_w, ada_b, norm_attn_g, norm_ffn_g, w_in, b_gate, q_norm_g, k_norm_g,
           w_attn_proj, w_four_proj, w_out, ffn_w_gate, ffn_w_up, ffn_w_down, router_w, moe_w_gate,
           moe_w_up, moe_w_down, final_norm_g):
    B, S, D = x.shape
    assert B == 1 and c.shape[0] == 1 and ctx.shape[0] == 1
    depth = ada_w.shape[0]
    q_w = w_attn_proj.shape[1]
    four_w = w_four_proj.shape[1]
    kv_w = (w_in.shape[2] - q_w - four_w - 2 * D) // 2
    n_heads, n_kv = q_w // HEAD_DIM, kv_w // HEAD_DIM

    cond = jnp.zeros((SUBLANES, D), F32).at[0].set(c[0]).at[1].set(c_ctx)
    mods = ada_modulation(cond, ada_w, ada_b)
    rope = _rope_tables(S)
    xl = x[0]
    xc = ctx[0]
    C = xc.shape[0]

    def mod(l, r, k):
        return mods[l, r, k * D:(k + 1) * D]

    def mixer(l, xs_, r, rope_, qkv_extra, tm):
        vec = _vec(norm_attn_g[l], mod(l, r, 0), mod(l, r, 1))
        qkg = _vec(q_norm_g[l], k_norm_g[l])
        qkv, f, gates = in_projection(xs_, vec, w_in[l].astype(BF16), b_gate[l].reshape(1, -1), qkg, rope_,
                                      q_w=q_w, kv_w=kv_w, four_w=four_w, tm=tm)
        return qkv, f, gates

    def finish_mixer(l, xs_, r, qkv, f, gates, qkv_extra, tm):
        n = xs_.shape[0]
        attn = attention(qkv, qkv_extra, n_heads=n_heads, n_kv=n_kv, tq=_tile(n, 256), tk=_tile(n, 512))
        four = fourier_mix(f)
        return merge(xs_, attn, four, gates, _vec(mod(l, r, 2)), w_attn_proj[l].astype(BF16),
                     w_four_proj[l].astype(BF16), w_out[l].astype(BF16), tm=tm)

    def channel_mixer(l, xs_, r, fin, tm):
        i = l // 2
        vec = _vec(norm_ffn_g[l], mod(l, r, 3), mod(l, r, 4), mod(l, r, 5), final_norm_g)
        if l % 2 == 0:
            y = dense_ffn(xs_, vec, ffn_w_gate[i].astype(BF16), ffn_w_up[i].astype(BF16),
                          ffn_w_down[i].astype(BF16), tm=tm, tf=512)
            if fin:
                y = final_norm(y, _vec(final_norm_g), tm=tm)
            return y
        f_exp = moe_w_gate.shape[3]
        tf = 768
        f_pad = ((f_exp + tf - 1) // tf) * tf
        wg = jnp.pad(moe_w_gate[i].astype(BF16), ((0, 0), (0, 0), (0, f_pad - f_exp)))
        wu = jnp.pad(moe_w_up[i].astype(BF16), ((0, 0), (0, 0), (0, f_pad - f_exp)))
        wd = jnp.pad(moe_w_down[i].astype(BF16), ((0, 0), (0, f_pad - f_exp), (0, 0)))
        return moe_layer(xs_, vec, router_w[i], wg, wu, wd, final_norm=fin,
                         tm_route=_tile(xs_.shape[0], 512), tm_rows=_tile(xs_.shape[0], 256),
                         tm_group=512, tf=tf)

    for l in range(depth):
        last = l == depth - 1
        tmc = _tile(C, 256)
        qkv_c, f_c, gates_c = mixer(l, xc, 1, None, None, tmc)
        if not last:
            xc_next = finish_mixer(l, xc, 1, qkv_c, f_c, gates_c, None, tmc)
            xc_next = channel_mixer(l, xc_next, 1, False, tmc)
        qkv, f, gates = mixer(l, xl, 0, rope, None, _tile(S, 512))
        xl = finish_mixer(l, xl, 0, qkv, f, gates, qkv_c, _tile(S, 256))
        xl = channel_mixer(l, xl, 0, last, _tile(S, 512))
        if not last:
            xc = xc_next
    return xl[None]
```

```python
import functools
import math

import numpy as np
import jax
import jax.numpy as jnp
from jax import lax
from jax.experimental import pallas as pl
from jax.experimental.pallas import tpu as pltpu

F32 = jnp.float32
BF16 = jnp.bfloat16

HEAD_DIM = 128
GRID_W = 64
ROPE_THETA = 10000.0
ROPE_FREQS = HEAD_DIM // 4
ATTN_SCALE = HEAD_DIM ** -0.5
N_FOURIER_GROUPS = 4
TOP_K = 2
EPS = 1e-6

LANES = 128
SUBLANES = 8
VMEM_LIMIT_BYTES = 58 * 1024 * 1024
NEG_INF = -1e30
LOG2E = math.log2(math.e)
BF16_SUBLANES = 16
VT_ROWS = HEAD_DIM + BF16_SUBLANES


def _cparams(*sem):
    return pltpu.CompilerParams(dimension_semantics=sem, vmem_limit_bytes=VMEM_LIMIT_BYTES)


def _norm_mod(x, g, shift, scale):
    ms = jnp.mean(x * x, axis=-1, keepdims=True)
    return (x * lax.rsqrt(ms + EPS) * g) * (1.0 + scale) + shift


def _silu(x):
    return x * jax.nn.sigmoid(x)


def _ada_kernel(cond_ref, w_ref, b_ref, o_ref):
    s = _silu(cond_ref[...]).astype(BF16)
    w = w_ref[0].astype(BF16)
    o_ref[0] = jnp.dot(s, w, preferred_element_type=F32) + b_ref[0]


def ada_modulation(cond, ada_w, ada_b, tn=1024):
    L, D, N = ada_w.shape
    return pl.pallas_call(
        _ada_kernel,
        grid=(L, N // tn),
        in_specs=[
            pl.BlockSpec((SUBLANES, D), lambda l, j: (0, 0)),
            pl.BlockSpec((1, D, tn), lambda l, j: (l, 0, j)),
            pl.BlockSpec((1, 1, tn), lambda l, j: (l, 0, j)),
        ],
        out_specs=pl.BlockSpec((1, SUBLANES, tn), lambda l, j: (l, 0, j)),
        out_shape=jax.ShapeDtypeStruct((L, SUBLANES, N), F32),
        compiler_params=_cparams("parallel", "parallel"),
        name="ada_modulation",
    )(cond, ada_w, ada_b.reshape(L, 1, N))


def _inproj_kernel(x_ref, vec_ref, w_ref, bg_ref, qkg_ref, cos_ref, sa_ref, sb_ref,
                   qt_ref, k_ref, vt_ref, f_ref, gates_ref, hs_ref, acc_ref, *, n_q_tiles, kv_w, use_rope):
    j = pl.program_id(1)

    @pl.when(j == 0)
    def _():
        hs_ref[...] = _norm_mod(x_ref[...], vec_ref[0:1], vec_ref[1:2], vec_ref[2:3]).astype(BF16)

    acc_ref[...] = jnp.dot(hs_ref[...], w_ref[...], preferred_element_type=F32)
    tn = acc_ref.shape[1]

    def head_norm(c0, g, scale):
        t = acc_ref[:, c0:c0 + HEAD_DIM]
        ms = jnp.mean(t * t, axis=-1, keepdims=True)
        y = t * lax.rsqrt(ms + EPS) * g
        if use_rope:
            y = (y * cos_ref[...] + pltpu.roll(y, HEAD_DIM - ROPE_FREQS, 1) * sa_ref[...]
                 + pltpu.roll(y, ROPE_FREQS, 1) * sb_ref[...])
        return y * scale if scale != 1.0 else y

    @pl.when(j < n_q_tiles)
    def _():
        for h in range(tn // HEAD_DIM):
            y = head_norm(h * HEAD_DIM, qkg_ref[0:1], ATTN_SCALE * LOG2E)
            qt_ref[h * HEAD_DIM:(h + 1) * HEAD_DIM, :] = y.T.astype(BF16)

    @pl.when(j == n_q_tiles)
    def _():
        for h in range(kv_w // HEAD_DIM):
            c0 = h * HEAD_DIM
            k_ref[:, c0:c0 + HEAD_DIM] = head_norm(c0, qkg_ref[1:2], 1.0).astype(BF16)
            r0 = h * VT_ROWS
            vt_ref[r0:r0 + HEAD_DIM, :] = acc_ref[:, kv_w + c0:kv_w + c0 + HEAD_DIM].T.astype(BF16)
            vt_ref[r0 + HEAD_DIM:r0 + VT_ROWS, :] = jnp.ones((VT_ROWS - HEAD_DIM, vt_ref.shape[1]), BF16)

    @pl.when(j == n_q_tiles + 1)
    def _():
        f_ref[...] = acc_ref[...].astype(BF16)

    @pl.when(j > n_q_tiles + 1)
    def _():
        gates_ref[...] = jax.nn.sigmoid(acc_ref[...] + bg_ref[...]).astype(BF16)


def in_projection(x, vec, w_in, b_gate, qkg, rope, *, q_w, kv_w, four_w, tm):
    R, D = x.shape
    tn = 2 * kv_w
    n_kv = kv_w // HEAD_DIM
    assert q_w % tn == 0 and four_w == tn and (2 * D) % tn == 0 and R % tm == 0 and tm % LANES == 0
    n_q = q_w // tn
    n_g = (2 * D) // tn
    nj = n_q + 2 + n_g
    use_rope = rope is not None
    if rope is None:
        rope = (jnp.zeros((SUBLANES, HEAD_DIM), F32),) * 3
        rope_spec = pl.BlockSpec((SUBLANES, HEAD_DIM), lambda i, j: (0, 0))
    else:
        rope_spec = pl.BlockSpec((tm, HEAD_DIM), lambda i, j: (i, 0))
    gate_col = lambda j: jnp.clip(j - (n_q + 2), 0, n_g - 1)
    kern = functools.partial(_inproj_kernel, n_q_tiles=n_q, kv_w=kv_w, use_rope=use_rope)
    return pl.pallas_call(
        kern,
        grid=(R // tm, nj),
        in_specs=[
            pl.BlockSpec((tm, D), lambda i, j: (i, 0)),
            pl.BlockSpec((SUBLANES, D), lambda i, j: (0, 0)),
            pl.BlockSpec((D, tn), lambda i, j: (0, j)),
            pl.BlockSpec((1, tn), lambda i, j: (0, gate_col(j))),
            pl.BlockSpec((SUBLANES, HEAD_DIM), lambda i, j: (0, 0)),
            rope_spec, rope_spec, rope_spec,
        ],
        out_specs=[
            pl.BlockSpec((tn, tm), lambda i, j: (jnp.minimum(j, n_q - 1), i)),
            pl.BlockSpec((tm, kv_w), lambda i, j: (i, 0)),
            pl.BlockSpec((n_kv * VT_ROWS, tm), lambda i, j: (0, i)),
            pl.BlockSpec((tm, tn), lambda i, j: (i, 0)),
            pl.BlockSpec((tm, tn), lambda i, j: (i, gate_col(j))),
        ],
        out_shape=[
            jax.ShapeDtypeStruct((q_w, R), BF16),
            jax.ShapeDtypeStruct((R, kv_w), BF16),
            jax.ShapeDtypeStruct((n_kv * VT_ROWS, R), BF16),
            jax.ShapeDtypeStruct((R, four_w), BF16),
            jax.ShapeDtypeStruct((R, 2 * D), BF16),
        ],
        scratch_shapes=[pltpu.VMEM((tm, D), BF16), pltpu.VMEM((tm, tn), F32)],
        compiler_params=_cparams("parallel", "arbitrary"),
        name="in_projection",
    )(x, vec, w_in, b_gate, qkg, *rope)


def _attn_kernel(*refs, group, tk, n_chunks, has_extra):
    if has_extra:
        qt_ref, k_ref, vt_ref, kx_ref, vxt_ref, o_ref, m_ref, acc_ref, sa_ref, sb_ref, sx_ref = refs
    else:
        qt_ref, k_ref, vt_ref, o_ref, m_ref, acc_ref, sa_ref, sb_ref = refs
    m_ref[...] = jnp.full(m_ref.shape, NEG_INF, F32)
    acc_ref[...] = jnp.zeros(acc_ref.shape, F32)

    def scores(k, s_ref):
        for g in range(group):
            s_ref[g] = jnp.dot(k, qt_ref[g * HEAD_DIM:(g + 1) * HEAD_DIM, :], preferred_element_type=F32)

    def softmax_pv(vt, s_ref):
        for g in range(group):
            s = s_ref[g]
            m_prev = m_ref[g]
            m_new = jnp.maximum(m_prev, jnp.max(s, axis=0, keepdims=True))
            alpha = jnp.exp2(m_prev - m_new)
            p = jnp.exp2((s - m_new).astype(BF16))
            acc_ref[g] = alpha * acc_ref[g] + jnp.dot(vt, p, preferred_element_type=F32)
            m_ref[g] = m_new

    def k_chunk(c):
        return k_ref[pl.ds(pl.multiple_of(c * tk, tk), tk), :]

    def vt_chunk(c):
        return vt_ref[:, pl.ds(pl.multiple_of(c * tk, tk), tk)]

    n_pairs = (n_chunks - 1) // 2
    scores(k_chunk(0), sa_ref)

    def pair(i, carry):
        c = 2 * i
        scores(k_chunk(c + 1), sb_ref)
        softmax_pv(vt_chunk(c), sa_ref)
        scores(k_chunk(c + 2), sa_ref)
        softmax_pv(vt_chunk(c + 1), sb_ref)
        return carry

    lax.fori_loop(0, n_pairs, pair, 0)
    c = 2 * n_pairs
    if n_chunks - c == 2:
        scores(k_chunk(c + 1), sb_ref)
    softmax_pv(vt_chunk(c), sa_ref)
    if has_extra:
        scores(kx_ref[...], sx_ref)
    if n_chunks - c == 2:
        softmax_pv(vt_chunk(c + 1), sb_ref)
    if has_extra:
        softmax_pv(vxt_ref[...], sx_ref)
    for g in range(group):
        o = acc_ref[g, :HEAD_DIM, :] / acc_ref[g, HEAD_DIM:HEAD_DIM + 1, :]
        o_ref[:, g * HEAD_DIM:(g + 1) * HEAD_DIM] = o.T.astype(BF16)


def attention(qt, k, vt, extra, *, n_kv, tq, tk):
    q_w, R = qt.shape
    group = q_w // HEAD_DIM // n_kv
    gw = group * HEAD_DIM
    assert R % tq == 0 and R % tk == 0
    in_specs = [
        pl.BlockSpec((gw, tq), lambda h, i: (h, i)),
        pl.BlockSpec((R, HEAD_DIM), lambda h, i: (0, h)),
        pl.BlockSpec((VT_ROWS, R), lambda h, i: (h, 0)),
    ]
    args = [qt, k, vt]
    scratch = [
        pltpu.VMEM((group, 1, tq), F32),
        pltpu.VMEM((group, VT_ROWS, tq), F32),
        pltpu.VMEM((group, tk, tq), F32),
        pltpu.VMEM((group, tk, tq), F32),
    ]
    if extra is not None:
        rx = extra[0].shape[0]
        in_specs += [
            pl.BlockSpec((rx, HEAD_DIM), lambda h, i: (0, h)),
            pl.BlockSpec((VT_ROWS, rx), lambda h, i: (h, 0)),
        ]
        args += list(extra)
        scratch.append(pltpu.VMEM((group, rx, tq), F32))
    kern = functools.partial(_attn_kernel, group=group, tk=tk, n_chunks=R // tk, has_extra=extra is not None)
    return pl.pallas_call(
        kern,
        grid=(n_kv, R // tq),
        in_specs=in_specs,
        out_specs=pl.BlockSpec((tq, gw), lambda h, i: (i, h)),
        out_shape=jax.ShapeDtypeStruct((R, q_w), BF16),
        scratch_shapes=scratch,
        compiler_params=_cparams("parallel", "parallel"),
        name="attention",
    )(*args)


def _dft_mats(n, scale):
    k = np.arange(n)
    ang = 2.0 * np.pi * ((k[:, None] * k[None, :]) % n) / n
    return (jnp.asarray(np.cos(ang) * scale, BF16), jnp.asarray(np.sin(ang) * scale, BF16))


def _four_small_kernel(u_ref, cn_ref, sn_ref, cc_ref, sc_ref, o_ref, *, gc):
    u = u_ref[...]
    for g in range(u.shape[1] // gc):
        ug = u[:, g * gc:(g + 1) * gc]
        a = jnp.dot(ug, cc_ref[...], preferred_element_type=F32).astype(BF16)
        b = jnp.dot(ug, sc_ref[...], preferred_element_type=F32).astype(BF16)
        o = (jnp.dot(cn_ref[...], a, preferred_element_type=F32)
             - jnp.dot(sn_ref[...], b, preferred_element_type=F32))
        o_ref[:, g * gc:(g + 1) * gc] = o.astype(BF16)


def fourier_mix_small(u):
    n, c = u.shape
    gc = c // N_FOURIER_GROUPS
    cn, sn = _dft_mats(n, n ** -0.5)
    cc, sc = _dft_mats(gc, gc ** -0.5)
    full = lambda a: pl.BlockSpec(a.shape, lambda: (0,) * a.ndim)
    return pl.pallas_call(
        functools.partial(_four_small_kernel, gc=gc),
        in_specs=[full(u), full(cn), full(sn), full(cc), full(sc)],
        out_specs=pl.BlockSpec((n, c), lambda: (0, 0)),
        out_shape=jax.ShapeDtypeStruct((n, c), BF16),
        compiler_params=pltpu.CompilerParams(vmem_limit_bytes=VMEM_LIMIT_BYTES),
        name="fourier_small",
    )(u, cn, sn, cc, sc)


def _four_stage1_kernel(u_ref, ca_ref, sa_ref, cw_ref, sw_ref, yr_ref, yi_ref, *, c, tb):
    u = u_ref[...]
    yr = jnp.dot(ca_ref[...], u, preferred_element_type=F32)
    yi = -jnp.dot(sa_ref[...], u, preferred_element_type=F32)
    for b in range(tb):
        cw = cw_ref[:, b * LANES:(b + 1) * LANES]
        sw = sw_ref[:, b * LANES:(b + 1) * LANES]
        for l in range(c // LANES):
            sl = slice(b * c + l * LANES, b * c + (l + 1) * LANES)
            r, i = yr[:, sl], yi[:, sl]
            yr_ref[:, sl] = (r * cw + i * sw).astype(BF16)
            yi_ref[:, sl] = (i * cw - r * sw).astype(BF16)


def _four_stage2_kernel(yr_ref, yi_ref, cb_ref, sb_ref, cc_ref, sc_ref, o_ref, *, nb, c, gc, tk1):
    for t in range(tk1):
        yr = yr_ref[t * nb:(t + 1) * nb, :]
        yi = yi_ref[t * nb:(t + 1) * nb, :]
        pr = (jnp.dot(cb_ref[...], yr, preferred_element_type=F32)
              + jnp.dot(sb_ref[...], yi, preferred_element_type=F32)).astype(BF16)
        pi = (jnp.dot(cb_ref[...], yi, preferred_element_type=F32)
              - jnp.dot(sb_ref[...], yr, preferred_element_type=F32)).astype(BF16)
        for g in range(c // gc):
            sl = slice(g * gc, (g + 1) * gc)
            o = (jnp.dot(pr[:, sl], cc_ref[...], preferred_element_type=F32)
                 + jnp.dot(pi[:, sl], sc_ref[...], preferred_element_type=F32))
            o_ref[:, t * c + g * gc:t * c + (g + 1) * gc] = o.astype(BF16)


def fourier_mix_large(u, *, na=128, tb=4, tk1=4):
    n, c = u.shape
    nb = n // na
    assert na * nb == n and nb % tb == 0 and na % tk1 == 0 and nb % SUBLANES == 0
    gc = c // N_FOURIER_GROUPS
    ca, sa = _dft_mats(na, na ** -0.5)
    cb, sb = _dft_mats(nb, nb ** -0.5)
    cc, sc = _dft_mats(gc, gc ** -0.5)
    k1 = lax.broadcasted_iota(jnp.int32, (na, nb * LANES), 0)
    b = lax.broadcasted_iota(jnp.int32, (na, nb * LANES), 1) // LANES
    ang = ((k1 * b) % n).astype(F32) * (2.0 * math.pi / n)
    cw, sw = jnp.cos(ang), jnp.sin(ang)
    const = lambda a: pl.BlockSpec(a.shape, lambda i: (0,) * a.ndim)
    yr, yi = pl.pallas_call(
        functools.partial(_four_stage1_kernel, c=c, tb=tb),
        grid=(nb // tb,),
        in_specs=[
            pl.BlockSpec((na, tb * c), lambda i: (0, i)),
            const(ca), const(sa),
            pl.BlockSpec((na, tb * LANES), lambda i: (0, i)),
            pl.BlockSpec((na, tb * LANES), lambda i: (0, i)),
        ],
        out_specs=[pl.BlockSpec((na, tb * c), lambda i: (0, i))] * 2,
        out_shape=[jax.ShapeDtypeStruct((na, nb * c), BF16)] * 2,
        compiler_params=_cparams("parallel"),
        name="fourier_stage1",
    )(u.reshape(na, nb * c), ca, sa, cw, sw)
    out = pl.pallas_call(
        functools.partial(_four_stage2_kernel, nb=nb, c=c, gc=gc, tk1=tk1),
        grid=(na // tk1,),
        in_specs=[
            pl.BlockSpec((tk1 * nb, c), lambda i: (i, 0)),
            pl.BlockSpec((tk1 * nb, c), lambda i: (i, 0)),
            const(cb), const(sb), const(cc), const(sc),
        ],
        out_specs=pl.BlockSpec((nb, tk1 * c), lambda i: (0, i)),
        out_shape=jax.ShapeDtypeStruct((nb, na * c), BF16),
        compiler_params=_cparams("parallel"),
        name="fourier_stage2",
    )(yr.reshape(na * nb, c), yi.reshape(na * nb, c), cb, sb, cc, sc)
    return out.reshape(n, c)


def fourier_mix(u):
    n = u.shape[0]
    if n % (128 * SUBLANES) == 0 and n >= 128 * 16:
        return fourier_mix_large(u)
    return fourier_mix_small(u)


def _merge_kernel(x_ref, attn_ref, four_ref, gates_ref, vec_ref, wa_ref, wf_ref, wo_ref, o_ref):
    d = x_ref.shape[1]
    a = jnp.dot(attn_ref[...], wa_ref[...], preferred_element_type=F32)
    f = jnp.dot(four_ref[...], wf_ref[...], preferred_element_type=F32)
    y = gates_ref[:, :d].astype(F32) * a + gates_ref[:, d:].astype(F32) * f
    y = jnp.dot(y.astype(BF16), wo_ref[...], preferred_element_type=F32)
    o_ref[...] = x_ref[...] + vec_ref[0:1] * y


def merge(x, attn, four, gates, vec, wa, wf, wo, *, tm):
    R, D = x.shape
    row = lambda w: pl.BlockSpec((tm, w), lambda i: (i, 0))
    const = lambda a: pl.BlockSpec(a.shape, lambda i: (0, 0), pipeline_mode=pl.Buffered(1))
    return pl.pallas_call(
        _merge_kernel,
        grid=(R // tm,),
        in_specs=[row(D), row(attn.shape[1]), row(four.shape[1]), row(gates.shape[1]),
                  pl.BlockSpec((SUBLANES, D), lambda i: (0, 0)), const(wa), const(wf), const(wo)],
        out_specs=row(D),
        out_shape=jax.ShapeDtypeStruct((R, D), F32),
        compiler_params=_cparams("parallel"),
        name="merge",
    )(x, attn, four, gates, vec, wa, wf, wo)


def _ffn_kernel(x_ref, vec_ref, wg_ref, wu_ref, wd_ref, o_ref, hs_ref, acc_ref):
    j = pl.program_id(1)

    @pl.when(j == 0)
    def _():
        hs_ref[...] = _norm_mod(x_ref[...], vec_ref[0:1], vec_ref[1:2], vec_ref[2:3]).astype(BF16)
        acc_ref[...] = jnp.zeros(acc_ref.shape, F32)

    h = hs_ref[...]
    g = jnp.dot(h, wg_ref[...], preferred_element_type=F32)
    u = jnp.dot(h, wu_ref[...], preferred_element_type=F32)
    acc_ref[...] += jnp.dot((_silu(g) * u).astype(BF16), wd_ref[...], preferred_element_type=F32)

    @pl.when(j == pl.num_programs(1) - 1)
    def _():
        o_ref[...] = x_ref[...] + vec_ref[3:4] * acc_ref[...]


def dense_ffn(x, vec, wg, wu, wd, *, tm, tf):
    R, D = x.shape
    F = wg.shape[1]
    assert R % tm == 0 and F % tf == 0
    return pl.pallas_call(
        _ffn_kernel,
        grid=(R // tm, F // tf),
        in_specs=[
            pl.BlockSpec((tm, D), lambda i, j: (i, 0)),
            pl.BlockSpec((SUBLANES, D), lambda i, j: (0, 0)),
            pl.BlockSpec((D, tf), lambda i, j: (0, j)),
            pl.BlockSpec((D, tf), lambda i, j: (0, j)),
            pl.BlockSpec((tf, D), lambda i, j: (j, 0)),
        ],
        out_specs=pl.BlockSpec((tm, D), lambda i, j: (i, 0)),
        out_shape=jax.ShapeDtypeStruct((R, D), F32),
        scratch_shapes=[pltpu.VMEM((tm, D), BF16), pltpu.VMEM((tm, D), F32)],
        compiler_params=_cparams("parallel", "arbitrary"),
        name="dense_ffn",
    )(x, vec, wg, wu, wd)


def _router_kernel(x_ref, vec_ref, rw_ref, h_ref, meta_ref, cnt_ref, run_ref, *, n_exp):
    i = pl.program_id(0)

    @pl.when(i == 0)
    def _():
        run_ref[...] = jnp.zeros(run_ref.shape, F32)

    h = _norm_mod(x_ref[...], vec_ref[0:1], vec_ref[1:2], vec_ref[2:3])
    h_ref[...] = h
    tm = h.shape[0]
    logits = [jnp.sum(h * rw_ref[e:e + 1], axis=-1, keepdims=True) for e in range(n_exp)]
    m1 = logits[0]
    for e in range(1, n_exp):
        m1 = jnp.maximum(m1, logits[e])
    i1 = jnp.full((tm, 1), n_exp, jnp.int32)
    for e in reversed(range(n_exp)):
        i1 = jnp.where(logits[e] == m1, e, i1)
    m2 = jnp.full((tm, 1), NEG_INF, F32)
    for e in range(n_exp):
        m2 = jnp.maximum(m2, jnp.where(i1 == e, NEG_INF, logits[e]))
    i2 = jnp.full((tm, 1), n_exp, jnp.int32)
    for e in reversed(range(n_exp)):
        i2 = jnp.where((logits[e] == m2) & (i1 != e), e, i2)
    e2 = jnp.exp(m2 - m1)
    w1 = 1.0 / (1.0 + e2)
    w2 = e2 / (1.0 + e2)

    lane = lax.broadcasted_iota(jnp.int32, (tm, LANES), 1)
    hit1 = lane == i1
    hit2 = lane == i2
    onehot = (hit1 | hit2).astype(F32)
    r = lax.broadcasted_iota(jnp.int32, (tm, tm), 0)
    cidx = lax.broadcasted_iota(jnp.int32, (tm, tm), 1)
    ltri = (cidx < r).astype(BF16)
    before = jnp.dot(ltri, onehot.astype(BF16), preferred_element_type=F32) + run_ref[0:1]
    rank1 = jnp.sum(jnp.where(hit1, before, 0.0), axis=-1, keepdims=True)
    rank2 = jnp.sum(jnp.where(hit2, before, 0.0), axis=-1, keepdims=True)
    run_ref[0:1] = run_ref[0:1] + jnp.sum(onehot, axis=0, keepdims=True)

    meta = jnp.where(lane == 0, i1.astype(F32), 0.0)
    meta = jnp.where(lane == 1, i2.astype(F32), meta)
    meta = jnp.where(lane == 2, w1, meta)
    meta = jnp.where(lane == 3, w2, meta)
    meta = jnp.where(lane == 4, rank1, meta)
    meta = jnp.where(lane == 5, rank2, meta)
    meta_ref[...] = meta
    cnt_ref[...] = run_ref[...]


def moe_router(x, vec, router_w, *, tm):
    R, D = x.shape
    n_exp = router_w.shape[1]
    rw = jnp.zeros((SUBLANES * ((n_exp + SUBLANES - 1) // SUBLANES), D), F32).at[:n_exp].set(router_w.T)
    return pl.pallas_call(
        functools.partial(_router_kernel, n_exp=n_exp),
        grid=(R // tm,),
        in_specs=[
            pl.BlockSpec((tm, D), lambda i: (i, 0)),
            pl.BlockSpec((SUBLANES, D), lambda i: (0, 0)),
            pl.BlockSpec(rw.shape, lambda i: (0, 0)),
        ],
        out_specs=[
            pl.BlockSpec((tm, D), lambda i: (i, 0)),
            pl.BlockSpec((tm, LANES), lambda i: (i, 0)),
            pl.BlockSpec((SUBLANES, LANES), lambda i: (0, 0)),
        ],
        out_shape=[
            jax.ShapeDtypeStruct((R, D), F32),
            jax.ShapeDtypeStruct((R, LANES), F32),
            jax.ShapeDtypeStruct((SUBLANES, LANES), F32),
        ],
        scratch_shapes=[pltpu.VMEM((SUBLANES, LANES), F32)],
        compiler_params=_cparams("arbitrary"),
        name="moe_router",
    )(x, vec, rw)


def _row_copy(src_ref, s, dst_ref, d, sem):
    return pltpu.make_async_copy(src_ref.at[pl.ds(s, 1)], dst_ref.at[pl.ds(d, 1)], sem)


def _dispatch_kernel(dest_ref, h_ref, xs_in_ref, xs_ref, sem):
    del xs_in_ref
    tm = h_ref.shape[0]
    base = pl.program_id(0) * tm

    def start(t, c):
        for s in range(TOP_K):
            _row_copy(h_ref, t, xs_ref, dest_ref[TOP_K * (base + t) + s], sem).start()
        return c

    def wait(t, c):
        for s in range(TOP_K):
            _row_copy(h_ref, t, xs_ref, dest_ref[TOP_K * (base + t) + s], sem).wait()
        return c

    lax.fori_loop(0, tm, start, 0)
    lax.fori_loop(0, tm, wait, 0)


def moe_dispatch(h, dest, n_rows, *, tm):
    R, D = h.shape
    return pl.pallas_call(
        _dispatch_kernel,
        grid_spec=pltpu.PrefetchScalarGridSpec(
            num_scalar_prefetch=1,
            grid=(R // tm,),
            in_specs=[pl.BlockSpec((tm, D), lambda i, dest: (i, 0)),
                      pl.BlockSpec(memory_space=pl.ANY)],
            out_specs=pl.BlockSpec(memory_space=pl.ANY),
            scratch_shapes=[pltpu.SemaphoreType.DMA(())],
        ),
        out_shape=jax.ShapeDtypeStruct((n_rows, D), F32),
        input_output_aliases={2: 0},
        compiler_params=_cparams("arbitrary"),
        name="moe_dispatch",
    )(dest, h, jnp.zeros((n_rows, D), F32))


def _gffn_kernel(te_ref, nv_ref, xs_ref, wg_ref, wu_ref, wd_ref, ys_ref, hb_ref, acc_ref):
    t = pl.program_id(0)
    j = pl.program_id(1)
    valid = t < nv_ref[0]

    @pl.when(valid & (j == 0))
    def _():
        hb_ref[...] = xs_ref[...].astype(BF16)
        acc_ref[...] = jnp.zeros(acc_ref.shape, F32)

    @pl.when(valid)
    def _():
        h = hb_ref[...]
        g = jnp.dot(h, wg_ref[0], preferred_element_type=F32)
        u = jnp.dot(h, wu_ref[0], preferred_element_type=F32)
        acc_ref[...] += jnp.dot((_silu(g) * u).astype(BF16), wd_ref[0], preferred_element_type=F32)

    last = j == pl.num_programs(1) - 1

    @pl.when(valid & last)
    def _():
        ys_ref[...] = acc_ref[...]

    @pl.when(jnp.logical_not(valid) & last)
    def _():
        ys_ref[...] = jnp.zeros(ys_ref.shape, F32)


def moe_grouped_ffn(xs, tile_expert, n_valid, wg, wu, wd, *, tm, tf):
    n_rows, D = xs.shape
    F = wg.shape[2]
    nf = F // tf
    assert n_rows % tm == 0 and F % tf == 0

    def wcol(t, j, te, nv):
        return (te[t], 0, jnp.where(t < nv[0], j, nf - 1))

    def wrow(t, j, te, nv):
        return (te[t], jnp.where(t < nv[0], j, nf - 1), 0)

    return pl.pallas_call(
        _gffn_kernel,
        grid_spec=pltpu.PrefetchScalarGridSpec(
            num_scalar_prefetch=2,
            grid=(n_rows // tm, nf),
            in_specs=[
                pl.BlockSpec((tm, D), lambda t, j, te, nv: (t, 0)),
                pl.BlockSpec((1, D, tf), wcol),
                pl.BlockSpec((1, D, tf), wcol),
                pl.BlockSpec((1, tf, D), wrow),
            ],
            out_specs=pl.BlockSpec((tm, D), lambda t, j, te, nv: (t, 0)),
            scratch_shapes=[pltpu.VMEM((tm, D), BF16), pltpu.VMEM((tm, D), F32)],
        ),
        out_shape=jax.ShapeDtypeStruct((n_rows, D), F32),
        compiler_params=_cparams("parallel", "arbitrary"),
        name="moe_grouped_ffn",
    )(tile_expert, n_valid, xs, wg, wu, wd)


def _combine_kernel(dest_ref, x_ref, meta_ref, vec_ref, ys_ref, o_ref, b0_ref, b1_ref, sem, *, final_norm):
    tm = x_ref.shape[0]
    base = pl.program_id(0) * tm
    bufs = (b0_ref, b1_ref)

    def start(t, c):
        for s in range(TOP_K):
            _row_copy(ys_ref, dest_ref[TOP_K * (base + t) + s], bufs[s], t, sem).start()
        return c

    def wait(t, c):
        for s in range(TOP_K):
            _row_copy(ys_ref, dest_ref[TOP_K * (base + t) + s], bufs[s], t, sem).wait()
        return c

    lax.fori_loop(0, tm, start, 0)
    lax.fori_loop(0, tm, wait, 0)
    y = meta_ref[:, 2:3] * b0_ref[...] + meta_ref[:, 3:4] * b1_ref[...]
    out = x_ref[...] + vec_ref[3:4] * y
    if final_norm:
        ms = jnp.mean(out * out, axis=-1, keepdims=True)
        out = out * lax.rsqrt(ms + EPS) * vec_ref[4:5]
    o_ref[...] = out


def moe_combine(x, meta, vec, ys, dest, *, tm, final_norm):
    R, D = x.shape
    return pl.pallas_call(
        functools.partial(_combine_kernel, final_norm=final_norm),
        grid_spec=pltpu.PrefetchScalarGridSpec(
            num_scalar_prefetch=1,
            grid=(R // tm,),
            in_specs=[
                pl.BlockSpec((tm, D), lambda i, dest: (i, 0)),
                pl.BlockSpec((tm, LANES), lambda i, dest: (i, 0)),
                pl.BlockSpec((SUBLANES, D), lambda i, dest: (0, 0)),
                pl.BlockSpec(memory_space=pl.ANY),
            ],
            out_specs=pl.BlockSpec((tm, D), lambda i, dest: (i, 0)),
            scratch_shapes=[pltpu.VMEM((tm, D), F32), pltpu.VMEM((tm, D), F32),
                            pltpu.SemaphoreType.DMA(())],
        ),
        out_shape=jax.ShapeDtypeStruct((R, D), F32),
        compiler_params=_cparams("arbitrary"),
        name="moe_combine",
    )(dest, x, meta, vec, ys)


def moe_layer(x, vec, router_w, wg, wu, wd, *, final_norm, tm_route, tm_rows, tm_group, tf):
    R, D = x.shape
    n_exp = router_w.shape[1]
    h, meta, counts = moe_router(x, vec, router_w, tm=tm_route)
    counts = counts[0, :n_exp].astype(jnp.int32)
    padded = ((counts + tm_group - 1) // tm_group) * tm_group
    ends = jnp.cumsum(padded)
    offsets = ends - padded
    eid = meta[:, 0:TOP_K].astype(jnp.int32)
    rank = meta[:, 4:4 + TOP_K].astype(jnp.int32)
    dest = (offsets[eid] + rank).reshape(-1)
    n_tiles = (TOP_K * R) // tm_group + n_exp
    tile_start = jnp.arange(n_tiles, dtype=jnp.int32) * tm_group
    n_valid = ends[-1] // tm_group
    tile_expert = jnp.sum(tile_start[:, None] >= ends[None, :], axis=1).astype(jnp.int32)
    last_expert = jnp.take(tile_expert, jnp.maximum(n_valid - 1, 0))
    tile_expert = jnp.where(jnp.arange(n_tiles) < n_valid, tile_expert, last_expert)
    xs = moe_dispatch(h, dest, n_tiles * tm_group, tm=tm_rows)
    ys = moe_grouped_ffn(xs, tile_expert, n_valid.reshape(1).astype(jnp.int32), wg, wu, wd,
                         tm=tm_group, tf=tf)
    return moe_combine(x, meta, vec, ys, dest, tm=tm_rows, final_norm=final_norm)


def _final_norm_kernel(x_ref, g_ref, o_ref):
    x = x_ref[...]
    ms = jnp.mean(x * x, axis=-1, keepdims=True)
    o_ref[...] = x * lax.rsqrt(ms + EPS) * g_ref[0:1]


def final_norm(x, vec, *, tm):
    R, D = x.shape
    return pl.pallas_call(
        _final_norm_kernel,
        grid=(R // tm,),
        in_specs=[pl.BlockSpec((tm, D), lambda i: (i, 0)), pl.BlockSpec((SUBLANES, D), lambda i: (0, 0))],
        out_specs=pl.BlockSpec((tm, D), lambda i: (i, 0)),
        out_shape=jax.ShapeDtypeStruct((R, D), F32),
        compiler_params=_cparams("parallel"),
        name="final_norm",
    )(x, vec)


def _rope_tables(n_tokens):
    rows = n_tokens // GRID_W
    row = jnp.repeat(jnp.arange(rows, dtype=F32), GRID_W)
    col = jnp.tile(jnp.arange(GRID_W, dtype=F32), rows)
    inv = ROPE_THETA ** (-jnp.arange(ROPE_FREQS, dtype=F32) / ROPE_FREQS)
    ar = row[:, None] * inv
    ac = col[:, None] * inv
    ang = jnp.concatenate([ar, ar, ac, ac], axis=-1)
    cos, sin = jnp.cos(ang), jnp.sin(ang)
    lane = jnp.arange(HEAD_DIM) % (2 * ROPE_FREQS)
    lo = lane < ROPE_FREQS
    return cos, jnp.where(lo, -sin, 0.0), jnp.where(lo, 0.0, sin)


def _vec(*rows):
    d = rows[0].shape[-1]
    out = jnp.zeros((SUBLANES, d), F32)
    for i, r in enumerate(rows):
        out = out.at[i].set(r.reshape(d))
    return out


def _tile(n, pref):
    t = pref
    while n % t:
        t //= 2
    return t


def kernel(x, c, ctx, c_ctx, ada_w, ada_b, norm_attn_g, norm_ffn_g, w_in, b_gate, q_norm_g, k_norm_g,
           w_attn_proj, w_four_proj, w_out, ffn_w_gate, ffn_w_up, ffn_w_down, router_w, moe_w_gate,
           moe_w_up, moe_w_down, final_norm_g):
    B, S, D = x.shape
    assert B == 1 and c.shape[0] == 1 and ctx.shape[0] == 1
    depth = ada_w.shape[0]
    q_w = w_attn_proj.shape[1]
    four_w = w_four_proj.shape[1]
    kv_w = (w_in.shape[2] - q_w - four_w - 2 * D) // 2
    n_kv = kv_w // HEAD_DIM

    cond = jnp.zeros((SUBLANES, D), F32).at[0].set(c[0]).at[1].set(c_ctx)
    mods = ada_modulation(cond, ada_w, ada_b)
    rope = _rope_tables(S)
    xl = x[0]
    xc = ctx[0]
    C = xc.shape[0]

    def mod(l, r, k):
        return mods[l, r, k * D:(k + 1) * D]

    def project(l, xs_, r, rope_, tm):
        vec = _vec(norm_attn_g[l], mod(l, r, 0), mod(l, r, 1))
        qkg = _vec(q_norm_g[l], k_norm_g[l])
        return in_projection(xs_, vec, w_in[l].astype(BF16), b_gate[l].reshape(1, -1), qkg, rope_,
                             q_w=q_w, kv_w=kv_w, four_w=four_w, tm=tm)

    def finish_mixer(l, xs_, r, proj, extra, tm):
        qt, k, vt, f, gates = proj
        n = xs_.shape[0]
        attn = attention(qt, k, vt, extra, n_kv=n_kv, tq=_tile(n, 256), tk=_tile(n, 512))
        four = fourier_mix(f)
        return merge(xs_, attn, four, gates, _vec(mod(l, r, 2)), w_attn_proj[l].astype(BF16),
                     w_four_proj[l].astype(BF16), w_out[l].astype(BF16), tm=tm)

    def channel_mixer(l, xs_, r, fin, tm):
        i = l // 2
        vec = _vec(norm_ffn_g[l], mod(l, r, 3), mod(l, r, 4), mod(l, r, 5), final_norm_g)
        if l % 2 == 0:
            y = dense_ffn(xs_, vec, ffn_w_gate[i].astype(BF16), ffn_w_up[i].astype(BF16),
                          ffn_w_down[i].astype(BF16), tm=tm, tf=512)
            if fin:
                y = final_norm(y, _vec(final_norm_g), tm=tm)
            return y
        f_exp = moe_w_gate.shape[3]
        tf = 768
        f_pad = ((f_exp + tf - 1) // tf) * tf
        wg = jnp.pad(moe_w_gate[i].astype(BF16), ((0, 0), (0, 0), (0, f_pad - f_exp)))
        wu = jnp.pad(moe_w_up[i].astype(BF16), ((0, 0), (0, 0), (0, f_pad - f_exp)))
        wd = jnp.pad(moe_w_down[i].astype(BF16), ((0, 0), (0, f_pad - f_exp), (0, 0)))
        return moe_layer(xs_, vec, router_w[i], wg, wu, wd, final_norm=fin,
                         tm_route=_tile(xs_.shape[0], 512), tm_rows=_tile(xs_.shape[0], 256),
                         tm_group=512, tf=tf)

    for l in range(depth):
        last = l == depth - 1
        tmc = _tile(C, 256)
        proj_c = project(l, xc, 1, None, tmc)
        if not last:
            xc_next = finish_mixer(l, xc, 1, proj_c, None, tmc)
            xc_next = channel_mixer(l, xc_next, 1, False, tmc)
        proj = project(l, xl, 0, rope, _tile(S, 512))
        xl = finish_mixer(l, xl, 0, proj, (proj_c[1], proj_c[2]), _tile(S, 256))
        xl = channel_mixer(l, xl, 0, last, _tile(S, 512))
        if not last:
            xc = xc_next
    return xl[None]
```

```python
import functools
import math

import numpy as np
import jax
import jax.numpy as jnp
from jax import lax
from jax.experimental import pallas as pl
from jax.experimental.pallas import tpu as pltpu

F32 = jnp.float32
BF16 = jnp.bfloat16

HEAD_DIM = 128
GRID_W = 64
ROPE_THETA = 10000.0
ROPE_FREQS = HEAD_DIM // 4
ATTN_SCALE = HEAD_DIM ** -0.5
N_FOURIER_GROUPS = 4
TOP_K = 2
EPS = 1e-6

LANES = 128
SUBLANES = 8
VMEM_LIMIT_BYTES = 58 * 1024 * 1024
NEG_INF = -1e30
LOG2E = math.log2(math.e)
BF16_SUBLANES = 16
VT_ROWS = HEAD_DIM + BF16_SUBLANES
SCORE_BOUND_SLACK = 1.02
MAX_SCORE_BOUND = 60.0


def _cparams(*sem):
    return pltpu.CompilerParams(dimension_semantics=sem, vmem_limit_bytes=VMEM_LIMIT_BYTES)


def _norm_mod(x, g, shift, scale):
    ms = jnp.mean(x * x, axis=-1, keepdims=True)
    return (x * lax.rsqrt(ms + EPS) * g) * (1.0 + scale) + shift


def _silu(x):
    return x * jax.nn.sigmoid(x)


def _ada_kernel(cond_ref, w_ref, b_ref, o_ref):
    s = _silu(cond_ref[...]).astype(BF16)
    w = w_ref[0].astype(BF16)
    o_ref[0] = jnp.dot(s, w, preferred_element_type=F32) + b_ref[0]


def ada_modulation(cond, ada_w, ada_b, tn=1024):
    L, D, N = ada_w.shape
    return pl.pallas_call(
        _ada_kernel,
        grid=(L, N // tn),
        in_specs=[
            pl.BlockSpec((SUBLANES, D), lambda l, j: (0, 0)),
            pl.BlockSpec((1, D, tn), lambda l, j: (l, 0, j)),
            pl.BlockSpec((1, 1, tn), lambda l, j: (l, 0, j)),
        ],
        out_specs=pl.BlockSpec((1, SUBLANES, tn), lambda l, j: (l, 0, j)),
        out_shape=jax.ShapeDtypeStruct((L, SUBLANES, N), F32),
        compiler_params=_cparams("parallel", "parallel"),
        name="ada_modulation",
    )(cond, ada_w, ada_b.reshape(L, 1, N))


def _inproj_kernel(x_ref, vec_ref, w_ref, bg_ref, qkg_ref, cos_ref, sa_ref, sb_ref,
                   qt_ref, k_ref, vt_ref, f_ref, gates_ref, hs_ref, acc_ref, *, n_q_tiles, kv_w, use_rope):
    j = pl.program_id(1)

    @pl.when(j == 0)
    def _():
        hs_ref[...] = _norm_mod(x_ref[...], vec_ref[0:1], vec_ref[1:2], vec_ref[2:3]).astype(BF16)

    acc_ref[...] = jnp.dot(hs_ref[...], w_ref[...], preferred_element_type=F32)
    tn = acc_ref.shape[1]

    def head_norm(c0, g, scale):
        t = acc_ref[:, c0:c0 + HEAD_DIM]
        ms = jnp.mean(t * t, axis=-1, keepdims=True)
        y = t * lax.rsqrt(ms + EPS) * g
        if use_rope:
            y = (y * cos_ref[...] + pltpu.roll(y, HEAD_DIM - ROPE_FREQS, 1) * sa_ref[...]
                 + pltpu.roll(y, ROPE_FREQS, 1) * sb_ref[...])
        return y * scale if scale != 1.0 else y

    @pl.when(j < n_q_tiles)
    def _():
        for h in range(tn // HEAD_DIM):
            y = head_norm(h * HEAD_DIM, qkg_ref[0:1], ATTN_SCALE * LOG2E)
            qt_ref[h * HEAD_DIM:(h + 1) * HEAD_DIM, :] = y.T.astype(BF16)

    @pl.when(j == n_q_tiles)
    def _():
        for h in range(kv_w // HEAD_DIM):
            c0 = h * HEAD_DIM
            k_ref[:, c0:c0 + HEAD_DIM] = head_norm(c0, qkg_ref[1:2], 1.0).astype(BF16)
            r0 = h * VT_ROWS
            vt_ref[r0:r0 + HEAD_DIM, :] = acc_ref[:, kv_w + c0:kv_w + c0 + HEAD_DIM].T.astype(BF16)
            vt_ref[r0 + HEAD_DIM:r0 + VT_ROWS, :] = jnp.ones((VT_ROWS - HEAD_DIM, vt_ref.shape[1]), BF16)

    @pl.when(j == n_q_tiles + 1)
    def _():
        f_ref[...] = acc_ref[...].astype(BF16)

    @pl.when(j > n_q_tiles + 1)
    def _():
        gates_ref[...] = jax.nn.sigmoid(acc_ref[...] + bg_ref[...]).astype(BF16)


def in_projection(x, vec, w_in, b_gate, qkg, rope, *, q_w, kv_w, four_w, tm):
    R, D = x.shape
    tn = 2 * kv_w
    n_kv = kv_w // HEAD_DIM
    assert q_w % tn == 0 and four_w == tn and (2 * D) % tn == 0 and R % tm == 0 and tm % LANES == 0
    n_q = q_w // tn
    n_g = (2 * D) // tn
    nj = n_q + 2 + n_g
    use_rope = rope is not None
    if rope is None:
        rope = (jnp.zeros((SUBLANES, HEAD_DIM), F32),) * 3
        rope_spec = pl.BlockSpec((SUBLANES, HEAD_DIM), lambda i, j: (0, 0))
    else:
        rope_spec = pl.BlockSpec((tm, HEAD_DIM), lambda i, j: (i, 0))
    gate_col = lambda j: jnp.clip(j - (n_q + 2), 0, n_g - 1)
    kern = functools.partial(_inproj_kernel, n_q_tiles=n_q, kv_w=kv_w, use_rope=use_rope)
    return pl.pallas_call(
        kern,
        grid=(R // tm, nj),
        in_specs=[
            pl.BlockSpec((tm, D), lambda i, j: (i, 0)),
            pl.BlockSpec((SUBLANES, D), lambda i, j: (0, 0)),
            pl.BlockSpec((D, tn), lambda i, j: (0, j)),
            pl.BlockSpec((1, tn), lambda i, j: (0, gate_col(j))),
            pl.BlockSpec((SUBLANES, HEAD_DIM), lambda i, j: (0, 0)),
            rope_spec, rope_spec, rope_spec,
        ],
        out_specs=[
            pl.BlockSpec((tn, tm), lambda i, j: (jnp.minimum(j, n_q - 1), i)),
            pl.BlockSpec((tm, kv_w), lambda i, j: (i, 0)),
            pl.BlockSpec((n_kv * VT_ROWS, tm), lambda i, j: (0, i)),
            pl.BlockSpec((tm, tn), lambda i, j: (i, 0)),
            pl.BlockSpec((tm, tn), lambda i, j: (i, gate_col(j))),
        ],
        out_shape=[
            jax.ShapeDtypeStruct((q_w, R), BF16),
            jax.ShapeDtypeStruct((R, kv_w), BF16),
            jax.ShapeDtypeStruct((n_kv * VT_ROWS, R), BF16),
            jax.ShapeDtypeStruct((R, four_w), BF16),
            jax.ShapeDtypeStruct((R, 2 * D), BF16),
        ],
        scratch_shapes=[pltpu.VMEM((tm, D), BF16), pltpu.VMEM((tm, tn), F32)],
        compiler_params=_cparams("parallel", "arbitrary"),
        name="in_projection",
    )(x, vec, w_in, b_gate, qkg, *rope)


def _attn_kernel(*refs, group, tk, n_chunks, has_extra):
    if has_extra:
        qt_ref, k_ref, vt_ref, kx_ref, vxt_ref, o_ref, m_ref, acc_ref, sa_ref, sb_ref, sx_ref = refs
    else:
        qt_ref, k_ref, vt_ref, o_ref, m_ref, acc_ref, sa_ref, sb_ref = refs
    m_ref[...] = jnp.full(m_ref.shape, NEG_INF, F32)
    acc_ref[...] = jnp.zeros(acc_ref.shape, F32)

    def scores(k, s_ref):
        for g in range(group):
            s_ref[g] = jnp.dot(k, qt_ref[g * HEAD_DIM:(g + 1) * HEAD_DIM, :], preferred_element_type=F32)

    def softmax_pv(vt, s_ref):
        for g in range(group):
            s = s_ref[g]
            m_prev = m_ref[g]
            m_new = jnp.maximum(m_prev, jnp.max(s, axis=0, keepdims=True))
            alpha = jnp.exp2(m_prev - m_new)
            p = jnp.exp2((s - m_new).astype(BF16))
            acc_ref[g] = alpha * acc_ref[g] + jnp.dot(vt, p, preferred_element_type=F32)
            m_ref[g] = m_new

    def k_chunk(c):
        return k_ref[pl.ds(pl.multiple_of(c * tk, tk), tk), :]

    def vt_chunk(c):
        return vt_ref[:, pl.ds(pl.multiple_of(c * tk, tk), tk)]

    n_pairs = (n_chunks - 1) // 2
    scores(k_chunk(0), sa_ref)

    def pair(i, carry):
        c = 2 * i
        scores(k_chunk(c + 1), sb_ref)
        softmax_pv(vt_chunk(c), sa_ref)
        scores(k_chunk(c + 2), sa_ref)
        softmax_pv(vt_chunk(c + 1), sb_ref)
        return carry

    lax.fori_loop(0, n_pairs, pair, 0)
    c = 2 * n_pairs
    if n_chunks - c == 2:
        scores(k_chunk(c + 1), sb_ref)
    softmax_pv(vt_chunk(c), sa_ref)
    if has_extra:
        scores(kx_ref[...], sx_ref)
    if n_chunks - c == 2:
        softmax_pv(vt_chunk(c + 1), sb_ref)
    if has_extra:
        softmax_pv(vxt_ref[...], sx_ref)
    for g in range(group):
        o = acc_ref[g, :HEAD_DIM, :] / acc_ref[g, HEAD_DIM:HEAD_DIM + 1, :]
        o_ref[:, g * HEAD_DIM:(g + 1) * HEAD_DIM] = o.T.astype(BF16)


def attention(qt, k, vt, extra, *, n_kv, tq, tk):
    q_w, R = qt.shape
    group = q_w // HEAD_DIM // n_kv
    gw = group * HEAD_DIM
    assert R % tq == 0 and R % tk == 0
    in_specs = [
        pl.BlockSpec((gw, tq), lambda h, i: (h, i)),
        pl.BlockSpec((R, HEAD_DIM), lambda h, i: (0, h)),
        pl.BlockSpec((VT_ROWS, R), lambda h, i: (h, 0)),
    ]
    args = [qt, k, vt]
    scratch = [
        pltpu.VMEM((group, 1, tq), F32),
        pltpu.VMEM((group, VT_ROWS, tq), F32),
        pltpu.VMEM((group, tk, tq), F32),
        pltpu.VMEM((group, tk, tq), F32),
    ]
    if extra is not None:
        rx = extra[0].shape[0]
        in_specs += [
            pl.BlockSpec((rx, HEAD_DIM), lambda h, i: (0, h)),
            pl.BlockSpec((VT_ROWS, rx), lambda h, i: (h, 0)),
        ]
        args += list(extra)
        scratch.append(pltpu.VMEM((group, rx, tq), F32))
    kern = functools.partial(_attn_kernel, group=group, tk=tk, n_chunks=R // tk, has_extra=extra is not None)
    return pl.pallas_call(
        kern,
        grid=(n_kv, R // tq),
        in_specs=in_specs,
        out_specs=pl.BlockSpec((tq, gw), lambda h, i: (i, h)),
        out_shape=jax.ShapeDtypeStruct((R, q_w), BF16),
        scratch_shapes=scratch,
        compiler_params=_cparams("parallel", "parallel"),
        name="attention",
    )(*args)


def _attn_bounded_kernel(bound_ref, qt_ref, k_ref, vt_ref, kx_ref, vxt_ref, o_ref, acc_ref, sa_ref, sb_ref, sx_ref,
                         *, group, tk, n_chunks):
    acc_ref[...] = jnp.zeros(acc_ref.shape, F32)
    bound = bound_ref[0]

    def scores(k, s_ref, g):
        s_ref[g] = jnp.dot(k, qt_ref[g * HEAD_DIM:(g + 1) * HEAD_DIM, :], preferred_element_type=F32)

    def exp_pv(vt, s_ref, g):
        p = jnp.exp2((s_ref[g] - bound).astype(BF16))
        acc_ref[g] += jnp.dot(vt, p, preferred_element_type=F32)

    def k_chunk(c):
        return k_ref[pl.ds(pl.multiple_of(c * tk, tk), tk), :]

    def vt_chunk(c):
        return vt_ref[:, pl.ds(pl.multiple_of(c * tk, tk), tk)]

    def step(k_next, s_next, vt_cur, s_cur):
        for g in range(group):
            if k_next is not None:
                scores(k_next, s_next, g)
            exp_pv(vt_cur, s_cur, g)

    n_pairs = (n_chunks - 1) // 2
    for g in range(group):
        scores(k_chunk(0), sa_ref, g)

    def pair(i, carry):
        c = 2 * i
        step(k_chunk(c + 1), sb_ref, vt_chunk(c), sa_ref)
        step(k_chunk(c + 2), sa_ref, vt_chunk(c + 1), sb_ref)
        return carry

    lax.fori_loop(0, n_pairs, pair, 0)
    c = 2 * n_pairs
    if n_chunks - c == 2:
        step(k_chunk(c + 1), sb_ref, vt_chunk(c), sa_ref)
        step(kx_ref[...], sx_ref, vt_chunk(c + 1), sb_ref)
    else:
        step(kx_ref[...], sx_ref, vt_chunk(c), sa_ref)
    step(None, None, vxt_ref[...], sx_ref)
    for g in range(group):
        o = acc_ref[g, :HEAD_DIM, :] / acc_ref[g, HEAD_DIM:HEAD_DIM + 1, :]
        o_ref[:, g * HEAD_DIM:(g + 1) * HEAD_DIM] = o.T.astype(BF16)


def attention_bounded(qt, k, vt, extra, bound, *, n_kv, tq, tk):
    q_w, R = qt.shape
    group = q_w // HEAD_DIM // n_kv
    gw = group * HEAD_DIM
    rx = extra[0].shape[0]
    assert R % tq == 0 and R % tk == 0
    kern = functools.partial(_attn_bounded_kernel, group=group, tk=tk, n_chunks=R // tk)
    return pl.pallas_call(
        kern,
        grid=(n_kv, R // tq),
        in_specs=[
            pl.BlockSpec(memory_space=pltpu.SMEM),
            pl.BlockSpec((gw, tq), lambda h, i: (h, i)),
            pl.BlockSpec((R, HEAD_DIM), lambda h, i: (0, h)),
            pl.BlockSpec((VT_ROWS, R), lambda h, i: (h, 0)),
            pl.BlockSpec((rx, HEAD_DIM), lambda h, i: (0, h)),
            pl.BlockSpec((VT_ROWS, rx), lambda h, i: (h, 0)),
        ],
        out_specs=pl.BlockSpec((tq, gw), lambda h, i: (i, h)),
        out_shape=jax.ShapeDtypeStruct((R, q_w), BF16),
        scratch_shapes=[pltpu.VMEM((group, VT_ROWS, tq), F32), pltpu.VMEM((group, tk, tq), F32),
                        pltpu.VMEM((group, tk, tq), F32), pltpu.VMEM((group, rx, tq), F32)],
        compiler_params=_cparams("parallel", "parallel"),
        name="attention_bounded",
    )(bound, qt, k, vt, *extra)


def _dft_mats(n, scale):
    k = np.arange(n)
    ang = 2.0 * np.pi * ((k[:, None] * k[None, :]) % n) / n
    return (jnp.asarray(np.cos(ang) * scale, BF16), jnp.asarray(np.sin(ang) * scale, BF16))


def _four_small_kernel(u_ref, cn_ref, sn_ref, cc_ref, sc_ref, o_ref, *, gc):
    u = u_ref[...]
    for g in range(u.shape[1] // gc):
        ug = u[:, g * gc:(g + 1) * gc]
        a = jnp.dot(ug, cc_ref[...], preferred_element_type=F32).astype(BF16)
        b = jnp.dot(ug, sc_ref[...], preferred_element_type=F32).astype(BF16)
        o = (jnp.dot(cn_ref[...], a, preferred_element_type=F32)
             - jnp.dot(sn_ref[...], b, preferred_element_type=F32))
        o_ref[:, g * gc:(g + 1) * gc] = o.astype(BF16)


def fourier_mix_small(u):
    n, c = u.shape
    gc = c // N_FOURIER_GROUPS
    cn, sn = _dft_mats(n, n ** -0.5)
    cc, sc = _dft_mats(gc, gc ** -0.5)
    full = lambda a: pl.BlockSpec(a.shape, lambda: (0,) * a.ndim)
    return pl.pallas_call(
        functools.partial(_four_small_kernel, gc=gc),
        in_specs=[full(u), full(cn), full(sn), full(cc), full(sc)],
        out_specs=pl.BlockSpec((n, c), lambda: (0, 0)),
        out_shape=jax.ShapeDtypeStruct((n, c), BF16),
        compiler_params=pltpu.CompilerParams(vmem_limit_bytes=VMEM_LIMIT_BYTES),
        name="fourier_small",
    )(u, cn, sn, cc, sc)


def _four_stage1_kernel(u_ref, ca_ref, sa_ref, cw_ref, sw_ref, yr_ref, yi_ref, *, c, tb):
    u = u_ref[...]
    yr = jnp.dot(ca_ref[...], u, preferred_element_type=F32)
    yi = -jnp.dot(sa_ref[...], u, preferred_element_type=F32)
    for b in range(tb):
        cw = cw_ref[:, b * LANES:(b + 1) * LANES]
        sw = sw_ref[:, b * LANES:(b + 1) * LANES]
        for l in range(c // LANES):
            sl = slice(b * c + l * LANES, b * c + (l + 1) * LANES)
            r, i = yr[:, sl], yi[:, sl]
            yr_ref[:, sl] = (r * cw + i * sw).astype(BF16)
            yi_ref[:, sl] = (i * cw - r * sw).astype(BF16)


def _four_stage2_kernel(yr_ref, yi_ref, cb_ref, sb_ref, cc_ref, sc_ref, o_ref, *, nb, c, gc, tk1):
    for t in range(tk1):
        yr = yr_ref[t * nb:(t + 1) * nb, :]
        yi = yi_ref[t * nb:(t + 1) * nb, :]
        pr = (jnp.dot(cb_ref[...], yr, preferred_element_type=F32)
              + jnp.dot(sb_ref[...], yi, preferred_element_type=F32)).astype(BF16)
        pi = (jnp.dot(cb_ref[...], yi, preferred_element_type=F32)
              - jnp.dot(sb_ref[...], yr, preferred_element_type=F32)).astype(BF16)
        for g in range(c // gc):
            sl = slice(g * gc, (g + 1) * gc)
            o = (jnp.dot(pr[:, sl], cc_ref[...], preferred_element_type=F32)
                 + jnp.dot(pi[:, sl], sc_ref[...], preferred_element_type=F32))
            o_ref[:, t * c + g * gc:t * c + (g + 1) * gc] = o.astype(BF16)


def fourier_mix_large(u, *, na=128, tb=4, tk1=4):
    n, c = u.shape
    nb = n // na
    assert na * nb == n and nb % tb == 0 and na % tk1 == 0 and nb % SUBLANES == 0
    gc = c // N_FOURIER_GROUPS
    ca, sa = _dft_mats(na, na ** -0.5)
    cb, sb = _dft_mats(nb, nb ** -0.5)
    cc, sc = _dft_mats(gc, gc ** -0.5)
    k1 = lax.broadcasted_iota(jnp.int32, (na, nb * LANES), 0)
    b = lax.broadcasted_iota(jnp.int32, (na, nb * LANES), 1) // LANES
    ang = ((k1 * b) % n).astype(F32) * (2.0 * math.pi / n)
    cw, sw = jnp.cos(ang), jnp.sin(ang)
    const = lambda a: pl.BlockSpec(a.shape, lambda i: (0,) * a.ndim)
    yr, yi = pl.pallas_call(
        functools.partial(_four_stage1_kernel, c=c, tb=tb),
        grid=(nb // tb,),
        in_specs=[
            pl.BlockSpec((na, tb * c), lambda i: (0, i)),
            const(ca), const(sa),
            pl.BlockSpec((na, tb * LANES), lambda i: (0, i)),
            pl.BlockSpec((na, tb * LANES), lambda i: (0, i)),
        ],
        out_specs=[pl.BlockSpec((na, tb * c), lambda i: (0, i))] * 2,
        out_shape=[jax.ShapeDtypeStruct((na, nb * c), BF16)] * 2,
        compiler_params=_cparams("parallel"),
        name="fourier_stage1",
    )(u.reshape(na, nb * c), ca, sa, cw, sw)
    out = pl.pallas_call(
        functools.partial(_four_stage2_kernel, nb=nb, c=c, gc=gc, tk1=tk1),
        grid=(na // tk1,),
        in_specs=[
            pl.BlockSpec((tk1 * nb, c), lambda i: (i, 0)),
            pl.BlockSpec((tk1 * nb, c), lambda i: (i, 0)),
            const(cb), const(sb), const(cc), const(sc),
        ],
        out_specs=pl.BlockSpec((nb, tk1 * c), lambda i: (0, i)),
        out_shape=jax.ShapeDtypeStruct((nb, na * c), BF16),
        compiler_params=_cparams("parallel"),
        name="fourier_stage2",
    )(yr.reshape(na * nb, c), yi.reshape(na * nb, c), cb, sb, cc, sc)
    return out.reshape(n, c)


def fourier_mix(u):
    n = u.shape[0]
    if n % (128 * SUBLANES) == 0 and n >= 128 * 16:
        return fourier_mix_large(u)
    return fourier_mix_small(u)


def _merge_kernel(x_ref, attn_ref, four_ref, gates_ref, vec_ref, wa_ref, wf_ref, wo_ref, o_ref):
    d = x_ref.shape[1]
    a = jnp.dot(attn_ref[...], wa_ref[...], preferred_element_type=F32)
    f = jnp.dot(four_ref[...], wf_ref[...], preferred_element_type=F32)
    y = gates_ref[:, :d].astype(F32) * a + gates_ref[:, d:].astype(F32) * f
    y = jnp.dot(y.astype(BF16), wo_ref[...], preferred_element_type=F32)
    o_ref[...] = x_ref[...] + vec_ref[0:1] * y


def merge(x, attn, four, gates, vec, wa, wf, wo, *, tm):
    R, D = x.shape
    row = lambda w: pl.BlockSpec((tm, w), lambda i: (i, 0))
    const = lambda a: pl.BlockSpec(a.shape, lambda i: (0, 0), pipeline_mode=pl.Buffered(1))
    return pl.pallas_call(
        _merge_kernel,
        grid=(R // tm,),
        in_specs=[row(D), row(attn.shape[1]), row(four.shape[1]), row(gates.shape[1]),
                  pl.BlockSpec((SUBLANES, D), lambda i: (0, 0)), const(wa), const(wf), const(wo)],
        out_specs=row(D),
        out_shape=jax.ShapeDtypeStruct((R, D), F32),
        compiler_params=_cparams("parallel"),
        name="merge",
    )(x, attn, four, gates, vec, wa, wf, wo)


def _ffn_kernel(x_ref, vec_ref, wg_ref, wu_ref, wd_ref, o_ref, hs_ref, acc_ref):
    j = pl.program_id(1)

    @pl.when(j == 0)
    def _():
        hs_ref[...] = _norm_mod(x_ref[...], vec_ref[0:1], vec_ref[1:2], vec_ref[2:3]).astype(BF16)
        acc_ref[...] = jnp.zeros(acc_ref.shape, F32)

    h = hs_ref[...]
    g = jnp.dot(h, wg_ref[...], preferred_element_type=F32)
    u = jnp.dot(h, wu_ref[...], preferred_element_type=F32)
    acc_ref[...] += jnp.dot((_silu(g) * u).astype(BF16), wd_ref[...], preferred_element_type=F32)

    @pl.when(j == pl.num_programs(1) - 1)
    def _():
        o_ref[...] = x_ref[...] + vec_ref[3:4] * acc_ref[...]


def dense_ffn(x, vec, wg, wu, wd, *, tm, tf):
    R, D = x.shape
    F = wg.shape[1]
    assert R % tm == 0 and F % tf == 0
    return pl.pallas_call(
        _ffn_kernel,
        grid=(R // tm, F // tf),
        in_specs=[
            pl.BlockSpec((tm, D), lambda i, j: (i, 0)),
            pl.BlockSpec((SUBLANES, D), lambda i, j: (0, 0)),
            pl.BlockSpec((D, tf), lambda i, j: (0, j)),
            pl.BlockSpec((D, tf), lambda i, j: (0, j)),
            pl.BlockSpec((tf, D), lambda i, j: (j, 0)),
        ],
        out_specs=pl.BlockSpec((tm, D), lambda i, j: (i, 0)),
        out_shape=jax.ShapeDtypeStruct((R, D), F32),
        scratch_shapes=[pltpu.VMEM((tm, D), BF16), pltpu.VMEM((tm, D), F32)],
        compiler_params=_cparams("parallel", "arbitrary"),
        name="dense_ffn",
    )(x, vec, wg, wu, wd)


def _router_kernel(x_ref, vec_ref, rw_ref, h_ref, meta_ref, cnt_ref, run_ref, *, n_exp):
    i = pl.program_id(0)

    @pl.when(i == 0)
    def _():
        run_ref[...] = jnp.zeros(run_ref.shape, F32)

    h = _norm_mod(x_ref[...], vec_ref[0:1], vec_ref[1:2], vec_ref[2:3])
    h_ref[...] = h
    tm = h.shape[0]
    logits = [jnp.sum(h * rw_ref[e:e + 1], axis=-1, keepdims=True) for e in range(n_exp)]
    m1 = logits[0]
    for e in range(1, n_exp):
        m1 = jnp.maximum(m1, logits[e])
    i1 = jnp.full((tm, 1), n_exp, jnp.int32)
    for e in reversed(range(n_exp)):
        i1 = jnp.where(logits[e] == m1, e, i1)
    m2 = jnp.full((tm, 1), NEG_INF, F32)
    for e in range(n_exp):
        m2 = jnp.maximum(m2, jnp.where(i1 == e, NEG_INF, logits[e]))
    i2 = jnp.full((tm, 1), n_exp, jnp.int32)
    for e in reversed(range(n_exp)):
        i2 = jnp.where((logits[e] == m2) & (i1 != e), e, i2)
    e2 = jnp.exp(m2 - m1)
    w1 = 1.0 / (1.0 + e2)
    w2 = e2 / (1.0 + e2)

    lane = lax.broadcasted_iota(jnp.int32, (tm, LANES), 1)
    hit1 = lane == i1
    hit2 = lane == i2
    onehot = (hit1 | hit2).astype(F32)
    r = lax.broadcasted_iota(jnp.int32, (tm, tm), 0)
    cidx = lax.broadcasted_iota(jnp.int32, (tm, tm), 1)
    ltri = (cidx < r).astype(BF16)
    before = jnp.dot(ltri, onehot.astype(BF16), preferred_element_type=F32) + run_ref[0:1]
    rank1 = jnp.sum(jnp.where(hit1, before, 0.0), axis=-1, keepdims=True)
    rank2 = jnp.sum(jnp.where(hit2, before, 0.0), axis=-1, keepdims=True)
    run_ref[0:1] = run_ref[0:1] + jnp.sum(onehot, axis=0, keepdims=True)

    meta = jnp.where(lane == 0, i1.astype(F32), 0.0)
    meta = jnp.where(lane == 1, i2.astype(F32), meta)
    meta = jnp.where(lane == 2, w1, meta)
    meta = jnp.where(lane == 3, w2, meta)
    meta = jnp.where(lane == 4, rank1, meta)
    meta = jnp.where(lane == 5, rank2, meta)
    meta_ref[...] = meta
    cnt_ref[...] = run_ref[...]


def moe_router(x, vec, router_w, *, tm):
    R, D = x.shape
    n_exp = router_w.shape[1]
    rw = jnp.zeros((SUBLANES * ((n_exp + SUBLANES - 1) // SUBLANES), D), F32).at[:n_exp].set(router_w.T)
    return pl.pallas_call(
        functools.partial(_router_kernel, n_exp=n_exp),
        grid=(R // tm,),
        in_specs=[
            pl.BlockSpec((tm, D), lambda i: (i, 0)),
            pl.BlockSpec((SUBLANES, D), lambda i: (0, 0)),
            pl.BlockSpec(rw.shape, lambda i: (0, 0)),
        ],
        out_specs=[
            pl.BlockSpec((tm, D), lambda i: (i, 0)),
            pl.BlockSpec((tm, LANES), lambda i: (i, 0)),
            pl.BlockSpec((SUBLANES, LANES), lambda i: (0, 0)),
        ],
        out_shape=[
            jax.ShapeDtypeStruct((R, D), F32),
            jax.ShapeDtypeStruct((R, LANES), F32),
            jax.ShapeDtypeStruct((SUBLANES, LANES), F32),
        ],
        scratch_shapes=[pltpu.VMEM((SUBLANES, LANES), F32)],
        compiler_params=_cparams("arbitrary"),
        name="moe_router",
    )(x, vec, rw)


def _row_copy(src_ref, s, dst_ref, d, sem):
    return pltpu.make_async_copy(src_ref.at[pl.ds(s, 1)], dst_ref.at[pl.ds(d, 1)], sem)


def _dispatch_kernel(dest_ref, h_ref, xs_in_ref, xs_ref, sem):
    del xs_in_ref
    tm = h_ref.shape[0]
    base = pl.program_id(0) * tm

    def start(t, c):
        for s in range(TOP_K):
            _row_copy(h_ref, t, xs_ref, dest_ref[TOP_K * (base + t) + s], sem).start()
        return c

    def wait(t, c):
        for s in range(TOP_K):
            _row_copy(h_ref, t, xs_ref, dest_ref[TOP_K * (base + t) + s], sem).wait()
        return c

    lax.fori_loop(0, tm, start, 0)
    lax.fori_loop(0, tm, wait, 0)


def moe_dispatch(h, dest, n_rows, *, tm):
    R, D = h.shape
    return pl.pallas_call(
        _dispatch_kernel,
        grid_spec=pltpu.PrefetchScalarGridSpec(
            num_scalar_prefetch=1,
            grid=(R // tm,),
            in_specs=[pl.BlockSpec((tm, D), lambda i, dest: (i, 0)),
                      pl.BlockSpec(memory_space=pl.ANY)],
            out_specs=pl.BlockSpec(memory_space=pl.ANY),
            scratch_shapes=[pltpu.SemaphoreType.DMA(())],
        ),
        out_shape=jax.ShapeDtypeStruct((n_rows, D), F32),
        input_output_aliases={2: 0},
        compiler_params=_cparams("arbitrary"),
        name="moe_dispatch",
    )(dest, h, jnp.zeros((n_rows, D), F32))


def _gffn_kernel(te_ref, nv_ref, xs_ref, wg_ref, wu_ref, wd_ref, ys_ref, hb_ref, acc_ref):
    t = pl.program_id(0)
    j = pl.program_id(1)
    valid = t < nv_ref[0]

    @pl.when(valid & (j == 0))
    def _():
        hb_ref[...] = xs_ref[...].astype(BF16)
        acc_ref[...] = jnp.zeros(acc_ref.shape, F32)

    @pl.when(valid)
    def _():
        h = hb_ref[...]
        g = jnp.dot(h, wg_ref[0], preferred_element_type=F32)
        u = jnp.dot(h, wu_ref[0], preferred_element_type=F32)
        acc_ref[...] += jnp.dot((_silu(g) * u).astype(BF16), wd_ref[0], preferred_element_type=F32)

    last = j == pl.num_programs(1) - 1

    @pl.when(valid & last)
    def _():
        ys_ref[...] = acc_ref[...]

    @pl.when(jnp.logical_not(valid) & last)
    def _():
        ys_ref[...] = jnp.zeros(ys_ref.shape, F32)


def moe_grouped_ffn(xs, tile_expert, n_valid, wg, wu, wd, *, tm, tf):
    n_rows, D = xs.shape
    F = wg.shape[2]
    nf = F // tf
    assert n_rows % tm == 0 and F % tf == 0

    def wcol(t, j, te, nv):
        return (te[t], 0, jnp.where(t < nv[0], j, nf - 1))

    def wrow(t, j, te, nv):
        return (te[t], jnp.where(t < nv[0], j, nf - 1), 0)

    return pl.pallas_call(
        _gffn_kernel,
        grid_spec=pltpu.PrefetchScalarGridSpec(
            num_scalar_prefetch=2,
            grid=(n_rows // tm, nf),
            in_specs=[
                pl.BlockSpec((tm, D), lambda t, j, te, nv: (t, 0)),
                pl.BlockSpec((1, D, tf), wcol),
                pl.BlockSpec((1, D, tf), wcol),
                pl.BlockSpec((1, tf, D), wrow),
            ],
            out_specs=pl.BlockSpec((tm, D), lambda t, j, te, nv: (t, 0)),
            scratch_shapes=[pltpu.VMEM((tm, D), BF16), pltpu.VMEM((tm, D), F32)],
        ),
        out_shape=jax.ShapeDtypeStruct((n_rows, D), F32),
        compiler_params=_cparams("parallel", "arbitrary"),
        name="moe_grouped_ffn",
    )(tile_expert, n_valid, xs, wg, wu, wd)


def _combine_kernel(dest_ref, x_ref, meta_ref, vec_ref, ys_ref, o_ref, b0_ref, b1_ref, sem, *, final_norm):
    tm = x_ref.shape[0]
    base = pl.program_id(0) * tm
    bufs = (b0_ref, b1_ref)

    def start(t, c):
        for s in range(TOP_K):
            _row_copy(ys_ref, dest_ref[TOP_K * (base + t) + s], bufs[s], t, sem).start()
        return c

    def wait(t, c):
        for s in range(TOP_K):
            _row_copy(ys_ref, dest_ref[TOP_K * (base + t) + s], bufs[s], t, sem).wait()
        return c

    lax.fori_loop(0, tm, start, 0)
    lax.fori_loop(0, tm, wait, 0)
    y = meta_ref[:, 2:3] * b0_ref[...] + meta_ref[:, 3:4] * b1_ref[...]
    out = x_ref[...] + vec_ref[3:4] * y
    if final_norm:
        ms = jnp.mean(out * out, axis=-1, keepdims=True)
        out = out * lax.rsqrt(ms + EPS) * vec_ref[4:5]
    o_ref[...] = out


def moe_combine(x, meta, vec, ys, dest, *, tm, final_norm):
    R, D = x.shape
    return pl.pallas_call(
        functools.partial(_combine_kernel, final_norm=final_norm),
        grid_spec=pltpu.PrefetchScalarGridSpec(
            num_scalar_prefetch=1,
            grid=(R // tm,),
            in_specs=[
                pl.BlockSpec((tm, D), lambda i, dest: (i, 0)),
                pl.BlockSpec((tm, LANES), lambda i, dest: (i, 0)),
                pl.BlockSpec((SUBLANES, D), lambda i, dest: (0, 0)),
                pl.BlockSpec(memory_space=pl.ANY),
            ],
            out_specs=pl.BlockSpec((tm, D), lambda i, dest: (i, 0)),
            scratch_shapes=[pltpu.VMEM((tm, D), F32), pltpu.VMEM((tm, D), F32),
                            pltpu.SemaphoreType.DMA(())],
        ),
        out_shape=jax.ShapeDtypeStruct((R, D), F32),
        compiler_params=_cparams("arbitrary"),
        name="moe_combine",
    )(dest, x, meta, vec, ys)


def moe_layer(x, vec, router_w, wg, wu, wd, *, final_norm, tm_route, tm_rows, tm_group, tf):
    R, D = x.shape
    n_exp = router_w.shape[1]
    h, meta, counts = moe_router(x, vec, router_w, tm=tm_route)
    counts = counts[0, :n_exp].astype(jnp.int32)
    padded = ((counts + tm_group - 1) // tm_group) * tm_group
    ends = jnp.cumsum(padded)
    offsets = ends - padded
    eid = meta[:, 0:TOP_K].astype(jnp.int32)
    rank = meta[:, 4:4 + TOP_K].astype(jnp.int32)
    dest = (offsets[eid] + rank).reshape(-1)
    n_tiles = (TOP_K * R) // tm_group + n_exp
    tile_start = jnp.arange(n_tiles, dtype=jnp.int32) * tm_group
    n_valid = ends[-1] // tm_group
    tile_expert = jnp.sum(tile_start[:, None] >= ends[None, :], axis=1).astype(jnp.int32)
    last_expert = jnp.take(tile_expert, jnp.maximum(n_valid - 1, 0))
    tile_expert = jnp.where(jnp.arange(n_tiles) < n_valid, tile_expert, last_expert)
    xs = moe_dispatch(h, dest, n_tiles * tm_group, tm=tm_rows)
    ys = moe_grouped_ffn(xs, tile_expert, n_valid.reshape(1).astype(jnp.int32), wg, wu, wd,
                         tm=tm_group, tf=tf)
    return moe_combine(x, meta, vec, ys, dest, tm=tm_rows, final_norm=final_norm)


def _final_norm_kernel(x_ref, g_ref, o_ref):
    x = x_ref[...]
    ms = jnp.mean(x * x, axis=-1, keepdims=True)
    o_ref[...] = x * lax.rsqrt(ms + EPS) * g_ref[0:1]


def final_norm(x, vec, *, tm):
    R, D = x.shape
    return pl.pallas_call(
        _final_norm_kernel,
        grid=(R // tm,),
        in_specs=[pl.BlockSpec((tm, D), lambda i: (i, 0)), pl.BlockSpec((SUBLANES, D), lambda i: (0, 0))],
        out_specs=pl.BlockSpec((tm, D), lambda i: (i, 0)),
        out_shape=jax.ShapeDtypeStruct((R, D), F32),
        compiler_params=_cparams("parallel"),
        name="final_norm",
    )(x, vec)


def _rope_tables(n_tokens):
    rows = n_tokens // GRID_W
    row = jnp.repeat(jnp.arange(rows, dtype=F32), GRID_W)
    col = jnp.tile(jnp.arange(GRID_W, dtype=F32), rows)
    inv = ROPE_THETA ** (-jnp.arange(ROPE_FREQS, dtype=F32) / ROPE_FREQS)
    ar = row[:, None] * inv
    ac = col[:, None] * inv
    ang = jnp.concatenate([ar, ar, ac, ac], axis=-1)
    cos, sin = jnp.cos(ang), jnp.sin(ang)
    lane = jnp.arange(HEAD_DIM) % (2 * ROPE_FREQS)
    lo = lane < ROPE_FREQS
    return cos, jnp.where(lo, -sin, 0.0), jnp.where(lo, 0.0, sin)


def _vec(*rows):
    d = rows[0].shape[-1]
    out = jnp.zeros((SUBLANES, d), F32)
    for i, r in enumerate(rows):
        out = out.at[i].set(r.reshape(d))
    return out


def _tile(n, pref):
    t = pref
    while n % t:
        t //= 2
    return t


def kernel(x, c, ctx, c_ctx, ada_w, ada_b, norm_attn_g, norm_ffn_g, w_in, b_gate, q_norm_g, k_norm_g,
           w_attn_proj, w_four_proj, w_out, ffn_w_gate, ffn_w_up, ffn_w_down, router_w, moe_w_gate,
           moe_w_up, moe_w_down, final_norm_g):
    B, S, D = x.shape
    assert B == 1 and c.shape[0] == 1 and ctx.shape[0] == 1
    depth = ada_w.shape[0]
    q_w = w_attn_proj.shape[1]
    four_w = w_four_proj.shape[1]
    kv_w = (w_in.shape[2] - q_w - four_w - 2 * D) // 2
    n_kv = kv_w // HEAD_DIM

    cond = jnp.zeros((SUBLANES, D), F32).at[0].set(c[0]).at[1].set(c_ctx)
    mods = ada_modulation(cond, ada_w, ada_b)
    rope = _rope_tables(S)
    xl = x[0]
    xc = ctx[0]
    C = xc.shape[0]

    def mod(l, r, k):
        return mods[l, r, k * D:(k + 1) * D]

    def project(l, xs_, r, rope_, tm):
        vec = _vec(norm_attn_g[l], mod(l, r, 0), mod(l, r, 1))
        qkg = _vec(q_norm_g[l], k_norm_g[l])
        return in_projection(xs_, vec, w_in[l].astype(BF16), b_gate[l].reshape(1, -1), qkg, rope_,
                             q_w=q_w, kv_w=kv_w, four_w=four_w, tm=tm)

    def finish_mixer(l, xs_, r, proj, extra, tm):
        qt, k, vt, f, gates = proj
        n = xs_.shape[0]
        tq, tk = _tile(n, 512), _tile(n, 512)
        if extra is None:
            attn = attention(qt, k, vt, None, n_kv=n_kv, tq=tq, tk=tk)
        else:
            bound = (SCORE_BOUND_SLACK * HEAD_DIM * ATTN_SCALE * LOG2E
                     * jnp.max(jnp.abs(q_norm_g[l])) * jnp.max(jnp.abs(k_norm_g[l]))).reshape(1)
            attn = lax.cond(
                bound[0] < MAX_SCORE_BOUND,
                lambda: attention_bounded(qt, k, vt, extra, bound, n_kv=n_kv, tq=tq, tk=tk),
                lambda: attention(qt, k, vt, extra, n_kv=n_kv, tq=tq, tk=tk))
        four = fourier_mix(f)
        return merge(xs_, attn, four, gates, _vec(mod(l, r, 2)), w_attn_proj[l].astype(BF16),
                     w_four_proj[l].astype(BF16), w_out[l].astype(BF16), tm=tm)

    def channel_mixer(l, xs_, r, fin, tm):
        i = l // 2
        vec = _vec(norm_ffn_g[l], mod(l, r, 3), mod(l, r, 4), mod(l, r, 5), final_norm_g)
        if l % 2 == 0:
            y = dense_ffn(xs_, vec, ffn_w_gate[i].astype(BF16), ffn_w_up[i].astype(BF16),
                          ffn_w_down[i].astype(BF16), tm=tm, tf=512)
            if fin:
                y = final_norm(y, _vec(final_norm_g), tm=tm)
            return y
        f_exp = moe_w_gate.shape[3]
        tf = 768
        f_pad = ((f_exp + tf - 1) // tf) * tf
        wg = jnp.pad(moe_w_gate[i], ((0, 0), (0, 0), (0, f_pad - f_exp))).astype(BF16)
        wu = jnp.pad(moe_w_up[i], ((0, 0), (0, 0), (0, f_pad - f_exp))).astype(BF16)
        wd = jnp.pad(moe_w_down[i], ((0, 0), (0, f_pad - f_exp), (0, 0))).astype(BF16)
        return moe_layer(xs_, vec, router_w[i], wg, wu, wd, final_norm=fin,
                         tm_route=_tile(xs_.shape[0], 512), tm_rows=_tile(xs_.shape[0], 256),
                         tm_group=512, tf=tf)

    for l in range(depth):
        last = l == depth - 1
        tmc = _tile(C, 256)
        proj_c = project(l, xc, 1, None, tmc)
        if not last:
            xc_next = finish_mixer(l, xc, 1, proj_c, None, tmc)
            xc_next = channel_mixer(l, xc_next, 1, False, tmc)
        proj = project(l, xl, 0, rope, _tile(S, 512))
        xl = finish_mixer(l, xl, 0, proj, (proj_c[1], proj_c[2]), _tile(S, 256))
        xl = channel_mixer(l, xl, 0, last, _tile(S, 512))
        if not last:
            xc = xc_next
    return xl[None]
```

```python
import functools
import math

import numpy as np
import jax
import jax.numpy as jnp
from jax import lax
from jax.experimental import pallas as pl
from jax.experimental.pallas import tpu as pltpu

F32 = jnp.float32
BF16 = jnp.bfloat16

HEAD_DIM = 128
GRID_W = 64
ROPE_THETA = 10000.0
ROPE_FREQS = HEAD_DIM // 4
ATTN_SCALE = HEAD_DIM ** -0.5
N_FOURIER_GROUPS = 4
TOP_K = 2
EPS = 1e-6

LANES = 128
SUBLANES = 8
VMEM_LIMIT_BYTES = 58 * 1024 * 1024
NEG_INF = -1e30
LOG2E = math.log2(math.e)
BF16_SUBLANES = 16
VT_ROWS = HEAD_DIM + BF16_SUBLANES
SCORE_BOUND_SLACK = 1.02
MAX_SCORE_BOUND = 60.0


def _cparams(*sem):
    return pltpu.CompilerParams(dimension_semantics=sem, vmem_limit_bytes=VMEM_LIMIT_BYTES)


def _norm_mod(x, g, shift, scale):
    ms = jnp.mean(x * x, axis=-1, keepdims=True)
    return (x * lax.rsqrt(ms + EPS) * g) * (1.0 + scale) + shift


def _silu(x):
    return x * jax.nn.sigmoid(x)


def _ada_kernel(cond_ref, w_ref, b_ref, o_ref):
    s = _silu(cond_ref[...]).astype(BF16)
    w = w_ref[0].astype(BF16)
    o_ref[0] = jnp.dot(s, w, preferred_element_type=F32) + b_ref[0]


def ada_modulation(cond, ada_w, ada_b, tn=1024):
    L, D, N = ada_w.shape
    return pl.pallas_call(
        _ada_kernel,
        grid=(L, N // tn),
        in_specs=[
            pl.BlockSpec((SUBLANES, D), lambda l, j: (0, 0)),
            pl.BlockSpec((1, D, tn), lambda l, j: (l, 0, j)),
            pl.BlockSpec((1, 1, tn), lambda l, j: (l, 0, j)),
        ],
        out_specs=pl.BlockSpec((1, SUBLANES, tn), lambda l, j: (l, 0, j)),
        out_shape=jax.ShapeDtypeStruct((L, SUBLANES, N), F32),
        compiler_params=_cparams("parallel", "parallel"),
        name="ada_modulation",
    )(cond, ada_w, ada_b.reshape(L, 1, N))


def _inproj_kernel(x_ref, vec_ref, w_ref, bg_ref, qkg_ref, cos_ref, sa_ref, sb_ref,
                   qt_ref, k_ref, vt_ref, f_ref, gates_ref, hs_ref, *, q_w, kv_w, four_w, tn, use_rope):
    hs_ref[...] = _norm_mod(x_ref[...], vec_ref[0:1], vec_ref[1:2], vec_ref[2:3]).astype(BF16)
    n_tiles = w_ref.shape[1] // tn
    gate0 = q_w + 2 * kv_w + four_w

    def matmul(j):
        return jnp.dot(hs_ref[...], w_ref[:, j * tn:(j + 1) * tn], preferred_element_type=F32)

    def head_norm(t, g, scale):
        ms = jnp.mean(t * t, axis=-1, keepdims=True)
        y = t * lax.rsqrt(ms + EPS) * g
        if use_rope:
            y = (y * cos_ref[...] + pltpu.roll(y, HEAD_DIM - ROPE_FREQS, 1) * sa_ref[...]
                 + pltpu.roll(y, ROPE_FREQS, 1) * sb_ref[...])
        return y * scale if scale != 1.0 else y

    def epilogue(j, acc):
        for c in range(0, tn, HEAD_DIM):
            col = j * tn + c
            t = acc[:, c:c + HEAD_DIM]
            if col < q_w:
                qt_ref[col:col + HEAD_DIM, :] = head_norm(t, qkg_ref[0:1], ATTN_SCALE * LOG2E).T.astype(BF16)
            elif col < q_w + kv_w:
                k_ref[:, col - q_w:col - q_w + HEAD_DIM] = head_norm(t, qkg_ref[1:2], 1.0).astype(BF16)
            elif col < q_w + 2 * kv_w:
                r0 = (col - q_w - kv_w) // HEAD_DIM * VT_ROWS
                vt_ref[r0:r0 + HEAD_DIM, :] = t.T.astype(BF16)
                vt_ref[r0 + HEAD_DIM:r0 + VT_ROWS, :] = jnp.ones((VT_ROWS - HEAD_DIM, vt_ref.shape[1]), BF16)
            elif col < gate0:
                f_ref[:, col - gate0 + four_w:col - gate0 + four_w + HEAD_DIM] = t.astype(BF16)
            else:
                g0 = col - gate0
                gates_ref[:, g0:g0 + HEAD_DIM] = jax.nn.sigmoid(t + bg_ref[:, g0:g0 + HEAD_DIM]).astype(BF16)

    acc = matmul(0)
    for j in range(n_tiles):
        nxt = matmul(j + 1) if j + 1 < n_tiles else None
        epilogue(j, acc)
        acc = nxt


def in_projection(x, vec, w_in, b_gate, qkg, rope, *, q_w, kv_w, four_w, tm, tn=1024):
    R, D = x.shape
    n_cols = w_in.shape[1]
    n_kv = kv_w // HEAD_DIM
    assert n_cols == q_w + 2 * kv_w + four_w + 2 * D and n_cols % tn == 0 and R % tm == 0 and tm % LANES == 0
    use_rope = rope is not None
    if rope is None:
        rope = (jnp.zeros((SUBLANES, HEAD_DIM), F32),) * 3
        rope_spec = pl.BlockSpec((SUBLANES, HEAD_DIM), lambda i: (0, 0))
    else:
        rope_spec = pl.BlockSpec((tm, HEAD_DIM), lambda i: (i, 0))
    kern = functools.partial(_inproj_kernel, q_w=q_w, kv_w=kv_w, four_w=four_w, tn=tn, use_rope=use_rope)
    return pl.pallas_call(
        kern,
        grid=(R // tm,),
        in_specs=[
            pl.BlockSpec((tm, D), lambda i: (i, 0)),
            pl.BlockSpec((SUBLANES, D), lambda i: (0, 0)),
            pl.BlockSpec((D, n_cols), lambda i: (0, 0), pipeline_mode=pl.Buffered(1)),
            pl.BlockSpec((1, 2 * D), lambda i: (0, 0)),
            pl.BlockSpec((SUBLANES, HEAD_DIM), lambda i: (0, 0)),
            rope_spec, rope_spec, rope_spec,
        ],
        out_specs=[
            pl.BlockSpec((q_w, tm), lambda i: (0, i)),
            pl.BlockSpec((tm, kv_w), lambda i: (i, 0)),
            pl.BlockSpec((n_kv * VT_ROWS, tm), lambda i: (0, i)),
            pl.BlockSpec((tm, four_w), lambda i: (i, 0)),
            pl.BlockSpec((tm, 2 * D), lambda i: (i, 0)),
        ],
        out_shape=[
            jax.ShapeDtypeStruct((q_w, R), BF16),
            jax.ShapeDtypeStruct((R, kv_w), BF16),
            jax.ShapeDtypeStruct((n_kv * VT_ROWS, R), BF16),
            jax.ShapeDtypeStruct((R, four_w), BF16),
            jax.ShapeDtypeStruct((R, 2 * D), BF16),
        ],
        scratch_shapes=[pltpu.VMEM((tm, D), BF16)],
        compiler_params=_cparams("parallel"),
        name="in_projection",
    )(x, vec, w_in, b_gate, qkg, *rope)


def _attn_kernel(*refs, group, tk, n_chunks, has_extra):
    if has_extra:
        qt_ref, k_ref, vt_ref, kx_ref, vxt_ref, o_ref, m_ref, acc_ref, sa_ref, sb_ref, sx_ref = refs
    else:
        qt_ref, k_ref, vt_ref, o_ref, m_ref, acc_ref, sa_ref, sb_ref = refs
    m_ref[...] = jnp.full(m_ref.shape, NEG_INF, F32)
    acc_ref[...] = jnp.zeros(acc_ref.shape, F32)

    def scores(k, s_ref):
        for g in range(group):
            s_ref[g] = jnp.dot(k, qt_ref[g * HEAD_DIM:(g + 1) * HEAD_DIM, :], preferred_element_type=F32)

    def softmax_pv(vt, s_ref):
        for g in range(group):
            s = s_ref[g]
            m_prev = m_ref[g]
            m_new = jnp.maximum(m_prev, jnp.max(s, axis=0, keepdims=True))
            alpha = jnp.exp2(m_prev - m_new)
            p = jnp.exp2((s - m_new).astype(BF16))
            acc_ref[g] = alpha * acc_ref[g] + jnp.dot(vt, p, preferred_element_type=F32)
            m_ref[g] = m_new

    def k_chunk(c):
        return k_ref[pl.ds(pl.multiple_of(c * tk, tk), tk), :]

    def vt_chunk(c):
        return vt_ref[:, pl.ds(pl.multiple_of(c * tk, tk), tk)]

    n_pairs = (n_chunks - 1) // 2
    scores(k_chunk(0), sa_ref)

    def pair(i, carry):
        c = 2 * i
        scores(k_chunk(c + 1), sb_ref)
        softmax_pv(vt_chunk(c), sa_ref)
        scores(k_chunk(c + 2), sa_ref)
        softmax_pv(vt_chunk(c + 1), sb_ref)
        return carry

    lax.fori_loop(0, n_pairs, pair, 0)
    c = 2 * n_pairs
    if n_chunks - c == 2:
        scores(k_chunk(c + 1), sb_ref)
    softmax_pv(vt_chunk(c), sa_ref)
    if has_extra:
        scores(kx_ref[...], sx_ref)
    if n_chunks - c == 2:
        softmax_pv(vt_chunk(c + 1), sb_ref)
    if has_extra:
        softmax_pv(vxt_ref[...], sx_ref)
    for g in range(group):
        o = acc_ref[g, :HEAD_DIM, :] / acc_ref[g, HEAD_DIM:HEAD_DIM + 1, :]
        o_ref[:, g * HEAD_DIM:(g + 1) * HEAD_DIM] = o.T.astype(BF16)


def attention(qt, k, vt, extra, *, n_kv, tq, tk):
    q_w, R = qt.shape
    group = q_w // HEAD_DIM // n_kv
    gw = group * HEAD_DIM
    assert R % tq == 0 and R % tk == 0
    in_specs = [
        pl.BlockSpec((gw, tq), lambda h, i: (h, i)),
        pl.BlockSpec((R, HEAD_DIM), lambda h, i: (0, h)),
        pl.BlockSpec((VT_ROWS, R), lambda h, i: (h, 0)),
    ]
    args = [qt, k, vt]
    scratch = [
        pltpu.VMEM((group, 1, tq), F32),
        pltpu.VMEM((group, VT_ROWS, tq), F32),
        pltpu.VMEM((group, tk, tq), F32),
        pltpu.VMEM((group, tk, tq), F32),
    ]
    if extra is not None:
        rx = extra[0].shape[0]
        in_specs += [
            pl.BlockSpec((rx, HEAD_DIM), lambda h, i: (0, h)),
            pl.BlockSpec((VT_ROWS, rx), lambda h, i: (h, 0)),
        ]
        args += list(extra)
        scratch.append(pltpu.VMEM((group, rx, tq), F32))
    kern = functools.partial(_attn_kernel, group=group, tk=tk, n_chunks=R // tk, has_extra=extra is not None)
    return pl.pallas_call(
        kern,
        grid=(n_kv, R // tq),
        in_specs=in_specs,
        out_specs=pl.BlockSpec((tq, gw), lambda h, i: (i, h)),
        out_shape=jax.ShapeDtypeStruct((R, q_w), BF16),
        scratch_shapes=scratch,
        compiler_params=_cparams("parallel", "parallel"),
        name="attention",
    )(*args)


def _attn_bounded_kernel(bound_ref, qt_ref, k_ref, vt_ref, kx_ref, vxt_ref, o_ref, acc_ref, sa_ref, sb_ref, sx_ref,
                         *, group, tk, n_chunks):
    acc_ref[...] = jnp.zeros(acc_ref.shape, F32)
    bound = bound_ref[0]

    def scores(k, s_ref, g):
        s_ref[g] = jnp.dot(k, qt_ref[g * HEAD_DIM:(g + 1) * HEAD_DIM, :], preferred_element_type=F32)

    def exp_pv(vt, s_ref, g):
        p = jnp.exp2((s_ref[g] - bound).astype(BF16))
        acc_ref[g] += jnp.dot(vt, p, preferred_element_type=F32)

    def k_chunk(c):
        return k_ref[pl.ds(pl.multiple_of(c * tk, tk), tk), :]

    def vt_chunk(c):
        return vt_ref[:, pl.ds(pl.multiple_of(c * tk, tk), tk)]

    def step(k_next, s_next, vt_cur, s_cur):
        for g in range(group):
            if k_next is not None:
                scores(k_next, s_next, g)
            exp_pv(vt_cur, s_cur, g)

    n_pairs = (n_chunks - 1) // 2
    for g in range(group):
        scores(k_chunk(0), sa_ref, g)

    def pair(i, carry):
        c = 2 * i
        step(k_chunk(c + 1), sb_ref, vt_chunk(c), sa_ref)
        step(k_chunk(c + 2), sa_ref, vt_chunk(c + 1), sb_ref)
        return carry

    lax.fori_loop(0, n_pairs, pair, 0)
    c = 2 * n_pairs
    if n_chunks - c == 2:
        step(k_chunk(c + 1), sb_ref, vt_chunk(c), sa_ref)
        step(kx_ref[...], sx_ref, vt_chunk(c + 1), sb_ref)
    else:
        step(kx_ref[...], sx_ref, vt_chunk(c), sa_ref)
    step(None, None, vxt_ref[...], sx_ref)
    for g in range(group):
        o = acc_ref[g, :HEAD_DIM, :] / acc_ref[g, HEAD_DIM:HEAD_DIM + 1, :]
        o_ref[:, g * HEAD_DIM:(g + 1) * HEAD_DIM] = o.T.astype(BF16)


def attention_bounded(qt, k, vt, extra, bound, *, n_kv, tq, tk):
    q_w, R = qt.shape
    group = q_w // HEAD_DIM // n_kv
    gw = group * HEAD_DIM
    rx = extra[0].shape[0]
    assert R % tq == 0 and R % tk == 0
    kern = functools.partial(_attn_bounded_kernel, group=group, tk=tk, n_chunks=R // tk)
    return pl.pallas_call(
        kern,
        grid=(n_kv, R // tq),
        in_specs=[
            pl.BlockSpec(memory_space=pltpu.SMEM),
            pl.BlockSpec((gw, tq), lambda h, i: (h, i)),
            pl.BlockSpec((R, HEAD_DIM), lambda h, i: (0, h)),
            pl.BlockSpec((VT_ROWS, R), lambda h, i: (h, 0)),
            pl.BlockSpec((rx, HEAD_DIM), lambda h, i: (0, h)),
            pl.BlockSpec((VT_ROWS, rx), lambda h, i: (h, 0)),
        ],
        out_specs=pl.BlockSpec((tq, gw), lambda h, i: (i, h)),
        out_shape=jax.ShapeDtypeStruct((R, q_w), BF16),
        scratch_shapes=[pltpu.VMEM((group, VT_ROWS, tq), F32), pltpu.VMEM((group, tk, tq), F32),
                        pltpu.VMEM((group, tk, tq), F32), pltpu.VMEM((group, rx, tq), F32)],
        compiler_params=_cparams("parallel", "parallel"),
        name="attention_bounded",
    )(bound, qt, k, vt, *extra)


def _dft_mats(n, scale):
    k = np.arange(n)
    ang = 2.0 * np.pi * ((k[:, None] * k[None, :]) % n) / n
    return (jnp.asarray(np.cos(ang) * scale, BF16), jnp.asarray(np.sin(ang) * scale, BF16))


def _four_small_kernel(u_ref, cn_ref, sn_ref, cc_ref, sc_ref, o_ref, *, gc):
    u = u_ref[...]
    for g in range(u.shape[1] // gc):
        ug = u[:, g * gc:(g + 1) * gc]
        a = jnp.dot(ug, cc_ref[...], preferred_element_type=F32).astype(BF16)
        b = jnp.dot(ug, sc_ref[...], preferred_element_type=F32).astype(BF16)
        o = (jnp.dot(cn_ref[...], a, preferred_element_type=F32)
             - jnp.dot(sn_ref[...], b, preferred_element_type=F32))
        o_ref[:, g * gc:(g + 1) * gc] = o.astype(BF16)


def fourier_mix_small(u):
    n, c = u.shape
    gc = c // N_FOURIER_GROUPS
    cn, sn = _dft_mats(n, n ** -0.5)
    cc, sc = _dft_mats(gc, gc ** -0.5)
    full = lambda a: pl.BlockSpec(a.shape, lambda: (0,) * a.ndim)
    return pl.pallas_call(
        functools.partial(_four_small_kernel, gc=gc),
        in_specs=[full(u), full(cn), full(sn), full(cc), full(sc)],
        out_specs=pl.BlockSpec((n, c), lambda: (0, 0)),
        out_shape=jax.ShapeDtypeStruct((n, c), BF16),
        compiler_params=pltpu.CompilerParams(vmem_limit_bytes=VMEM_LIMIT_BYTES),
        name="fourier_small",
    )(u, cn, sn, cc, sc)


def _four_stage1_kernel(u_ref, ca_ref, sa_ref, cw_ref, sw_ref, yr_ref, yi_ref, *, c, tb):
    u = u_ref[...]
    yr = jnp.dot(ca_ref[...], u, preferred_element_type=F32)
    yi = -jnp.dot(sa_ref[...], u, preferred_element_type=F32)
    for b in range(tb):
        cw = cw_ref[:, b * LANES:(b + 1) * LANES]
        sw = sw_ref[:, b * LANES:(b + 1) * LANES]
        for l in range(c // LANES):
            sl = slice(b * c + l * LANES, b * c + (l + 1) * LANES)
            r, i = yr[:, sl], yi[:, sl]
            yr_ref[:, sl] = (r * cw + i * sw).astype(BF16)
            yi_ref[:, sl] = (i * cw - r * sw).astype(BF16)


def _four_stage2_kernel(yr_ref, yi_ref, cb_ref, sb_ref, cc_ref, sc_ref, o_ref, *, nb, c, gc, tk1):
    for t in range(tk1):
        yr = yr_ref[t * nb:(t + 1) * nb, :]
        yi = yi_ref[t * nb:(t + 1) * nb, :]
        pr = (jnp.dot(cb_ref[...], yr, preferred_element_type=F32)
              + jnp.dot(sb_ref[...], yi, preferred_element_type=F32)).astype(BF16)
        pi = (jnp.dot(cb_ref[...], yi, preferred_element_type=F32)
              - jnp.dot(sb_ref[...], yr, preferred_element_type=F32)).astype(BF16)
        for g in range(c // gc):
            sl = slice(g * gc, (g + 1) * gc)
            o = (jnp.dot(pr[:, sl], cc_ref[...], preferred_element_type=F32)
                 + jnp.dot(pi[:, sl], sc_ref[...], preferred_element_type=F32))
            o_ref[:, t * c + g * gc:t * c + (g + 1) * gc] = o.astype(BF16)


def fourier_mix_large(u, *, na=128, tb=4, tk1=4):
    n, c = u.shape
    nb = n // na
    assert na * nb == n and nb % tb == 0 and na % tk1 == 0 and nb % SUBLANES == 0
    gc = c // N_FOURIER_GROUPS
    ca, sa = _dft_mats(na, na ** -0.5)
    cb, sb = _dft_mats(nb, nb ** -0.5)
    cc, sc = _dft_mats(gc, gc ** -0.5)
    k1 = lax.broadcasted_iota(jnp.int32, (na, nb * LANES), 0)
    b = lax.broadcasted_iota(jnp.int32, (na, nb * LANES), 1) // LANES
    ang = ((k1 * b) % n).astype(F32) * (2.0 * math.pi / n)
    cw, sw = jnp.cos(ang), jnp.sin(ang)
    const = lambda a: pl.BlockSpec(a.shape, lambda i: (0,) * a.ndim)
    yr, yi = pl.pallas_call(
        functools.partial(_four_stage1_kernel, c=c, tb=tb),
        grid=(nb // tb,),
        in_specs=[
            pl.BlockSpec((na, tb * c), lambda i: (0, i)),
            const(ca), const(sa),
            pl.BlockSpec((na, tb * LANES), lambda i: (0, i)),
            pl.BlockSpec((na, tb * LANES), lambda i: (0, i)),
        ],
        out_specs=[pl.BlockSpec((na, tb * c), lambda i: (0, i))] * 2,
        out_shape=[jax.ShapeDtypeStruct((na, nb * c), BF16)] * 2,
        compiler_params=_cparams("parallel"),
        name="fourier_stage1",
    )(u.reshape(na, nb * c), ca, sa, cw, sw)
    out = pl.pallas_call(
        functools.partial(_four_stage2_kernel, nb=nb, c=c, gc=gc, tk1=tk1),
        grid=(na // tk1,),
        in_specs=[
            pl.BlockSpec((tk1 * nb, c), lambda i: (i, 0)),
            pl.BlockSpec((tk1 * nb, c), lambda i: (i, 0)),
            const(cb), const(sb), const(cc), const(sc),
        ],
        out_specs=pl.BlockSpec((nb, tk1 * c), lambda i: (0, i)),
        out_shape=jax.ShapeDtypeStruct((nb, na * c), BF16),
        compiler_params=_cparams("parallel"),
        name="fourier_stage2",
    )(yr.reshape(na * nb, c), yi.reshape(na * nb, c), cb, sb, cc, sc)
    return out.reshape(n, c)


def fourier_mix(u):
    n = u.shape[0]
    if n % (128 * SUBLANES) == 0 and n >= 128 * 16:
        return fourier_mix_large(u)
    return fourier_mix_small(u)


def _merge_kernel(x_ref, attn_ref, four_ref, gates_ref, vec_ref, wa_ref, wf_ref, wo_ref, o_ref):
    d = x_ref.shape[1]
    a = jnp.dot(attn_ref[...], wa_ref[...], preferred_element_type=F32)
    f = jnp.dot(four_ref[...], wf_ref[...], preferred_element_type=F32)
    y = gates_ref[:, :d].astype(F32) * a + gates_ref[:, d:].astype(F32) * f
    y = jnp.dot(y.astype(BF16), wo_ref[...], preferred_element_type=F32)
    o_ref[...] = x_ref[...] + vec_ref[0:1] * y


def merge(x, attn, four, gates, vec, wa, wf, wo, *, tm):
    R, D = x.shape
    row = lambda w: pl.BlockSpec((tm, w), lambda i: (i, 0))
    const = lambda a: pl.BlockSpec(a.shape, lambda i: (0, 0), pipeline_mode=pl.Buffered(1))
    return pl.pallas_call(
        _merge_kernel,
        grid=(R // tm,),
        in_specs=[row(D), row(attn.shape[1]), row(four.shape[1]), row(gates.shape[1]),
                  pl.BlockSpec((SUBLANES, D), lambda i: (0, 0)), const(wa), const(wf), const(wo)],
        out_specs=row(D),
        out_shape=jax.ShapeDtypeStruct((R, D), F32),
        compiler_params=_cparams("parallel"),
        name="merge",
    )(x, attn, four, gates, vec, wa, wf, wo)


def _ffn_kernel(x_ref, vec_ref, wg_ref, wu_ref, wd_ref, o_ref, hs_ref, acc_ref):
    j = pl.program_id(1)

    @pl.when(j == 0)
    def _():
        hs_ref[...] = _norm_mod(x_ref[...], vec_ref[0:1], vec_ref[1:2], vec_ref[2:3]).astype(BF16)
        acc_ref[...] = jnp.zeros(acc_ref.shape, F32)

    h = hs_ref[...]
    g = jnp.dot(h, wg_ref[...], preferred_element_type=F32)
    u = jnp.dot(h, wu_ref[...], preferred_element_type=F32)
    acc_ref[...] += jnp.dot((_silu(g) * u).astype(BF16), wd_ref[...], preferred_element_type=F32)

    @pl.when(j == pl.num_programs(1) - 1)
    def _():
        o_ref[...] = x_ref[...] + vec_ref[3:4] * acc_ref[...]


def dense_ffn(x, vec, wg, wu, wd, *, tm, tf):
    R, D = x.shape
    F = wg.shape[1]
    assert R % tm == 0 and F % tf == 0
    return pl.pallas_call(
        _ffn_kernel,
        grid=(R // tm, F // tf),
        in_specs=[
            pl.BlockSpec((tm, D), lambda i, j: (i, 0)),
            pl.BlockSpec((SUBLANES, D), lambda i, j: (0, 0)),
            pl.BlockSpec((D, tf), lambda i, j: (0, j)),
            pl.BlockSpec((D, tf), lambda i, j: (0, j)),
            pl.BlockSpec((tf, D), lambda i, j: (j, 0)),
        ],
        out_specs=pl.BlockSpec((tm, D), lambda i, j: (i, 0)),
        out_shape=jax.ShapeDtypeStruct((R, D), F32),
        scratch_shapes=[pltpu.VMEM((tm, D), BF16), pltpu.VMEM((tm, D), F32)],
        compiler_params=_cparams("parallel", "arbitrary"),
        name="dense_ffn",
    )(x, vec, wg, wu, wd)


def _cast_pad_kernel(w_ref, o_ref, *, axis, n_valid):
    j = pl.program_id(1)

    @pl.when(j < n_valid)
    def _():
        o_ref[...] = w_ref[...].astype(BF16)

    @pl.when(j >= n_valid)
    def _():
        o_ref[...] = jnp.zeros(o_ref.shape, BF16)


def cast_pad(w, n_pad, *, axis, tc=256):
    E = w.shape[0]
    n = w.shape[axis]
    assert n % tc == 0 and n_pad % tc == 0
    n_valid = n // tc
    blk = (1, tc, w.shape[2]) if axis == 1 else (1, w.shape[1], tc)
    pick = lambda j: jnp.minimum(j, n_valid - 1)
    in_map = (lambda e, j: (e, pick(j), 0)) if axis == 1 else (lambda e, j: (e, 0, pick(j)))
    out_map = (lambda e, j: (e, j, 0)) if axis == 1 else (lambda e, j: (e, 0, j))
    out_shape = tuple(n_pad if a == axis else s for a, s in enumerate(w.shape))
    return pl.pallas_call(
        functools.partial(_cast_pad_kernel, axis=axis, n_valid=n_valid),
        grid=(E, n_pad // tc),
        in_specs=[pl.BlockSpec(blk, in_map)],
        out_specs=pl.BlockSpec(blk, out_map),
        out_shape=jax.ShapeDtypeStruct(out_shape, BF16),
        compiler_params=_cparams("parallel", "arbitrary"),
        name="cast_pad",
    )(w)


def _router_kernel(x_ref, vec_ref, rw_ref, h_ref, meta_ref, cnt_ref, run_ref, *, n_exp):
    i = pl.program_id(0)

    @pl.when(i == 0)
    def _():
        run_ref[...] = jnp.zeros(run_ref.shape, F32)

    h = _norm_mod(x_ref[...], vec_ref[0:1], vec_ref[1:2], vec_ref[2:3])
    h_ref[...] = h
    tm = h.shape[0]
    logits = [jnp.sum(h * rw_ref[e:e + 1], axis=-1, keepdims=True) for e in range(n_exp)]
    m1 = logits[0]
    for e in range(1, n_exp):
        m1 = jnp.maximum(m1, logits[e])
    i1 = jnp.full((tm, 1), n_exp, jnp.int32)
    for e in reversed(range(n_exp)):
        i1 = jnp.where(logits[e] == m1, e, i1)
    m2 = jnp.full((tm, 1), NEG_INF, F32)
    for e in range(n_exp):
        m2 = jnp.maximum(m2, jnp.where(i1 == e, NEG_INF, logits[e]))
    i2 = jnp.full((tm, 1), n_exp, jnp.int32)
    for e in reversed(range(n_exp)):
        i2 = jnp.where((logits[e] == m2) & (i1 != e), e, i2)
    e2 = jnp.exp(m2 - m1)
    w1 = 1.0 / (1.0 + e2)
    w2 = e2 / (1.0 + e2)

    lane = lax.broadcasted_iota(jnp.int32, (tm, LANES), 1)
    hit1 = lane == i1
    hit2 = lane == i2
    onehot = (hit1 | hit2).astype(F32)
    r = lax.broadcasted_iota(jnp.int32, (tm, tm), 0)
    cidx = lax.broadcasted_iota(jnp.int32, (tm, tm), 1)
    ltri = (cidx < r).astype(BF16)
    before = jnp.dot(ltri, onehot.astype(BF16), preferred_element_type=F32) + run_ref[0:1]
    rank1 = jnp.sum(jnp.where(hit1, before, 0.0), axis=-1, keepdims=True)
    rank2 = jnp.sum(jnp.where(hit2, before, 0.0), axis=-1, keepdims=True)
    run_ref[0:1] = run_ref[0:1] + jnp.sum(onehot, axis=0, keepdims=True)

    meta = jnp.where(lane == 0, i1.astype(F32), 0.0)
    meta = jnp.where(lane == 1, i2.astype(F32), meta)
    meta = jnp.where(lane == 2, w1, meta)
    meta = jnp.where(lane == 3, w2, meta)
    meta = jnp.where(lane == 4, rank1, meta)
    meta = jnp.where(lane == 5, rank2, meta)
    meta_ref[...] = meta
    cnt_ref[...] = run_ref[...]


def moe_router(x, vec, router_w, *, tm):
    R, D = x.shape
    n_exp = router_w.shape[1]
    rw = jnp.zeros((SUBLANES * ((n_exp + SUBLANES - 1) // SUBLANES), D), F32).at[:n_exp].set(router_w.T)
    return pl.pallas_call(
        functools.partial(_router_kernel, n_exp=n_exp),
        grid=(R // tm,),
        in_specs=[
            pl.BlockSpec((tm, D), lambda i: (i, 0)),
            pl.BlockSpec((SUBLANES, D), lambda i: (0, 0)),
            pl.BlockSpec(rw.shape, lambda i: (0, 0)),
        ],
        out_specs=[
            pl.BlockSpec((tm, D), lambda i: (i, 0)),
            pl.BlockSpec((tm, LANES), lambda i: (i, 0)),
            pl.BlockSpec((SUBLANES, LANES), lambda i: (0, 0)),
        ],
        out_shape=[
            jax.ShapeDtypeStruct((R, D), F32),
            jax.ShapeDtypeStruct((R, LANES), F32),
            jax.ShapeDtypeStruct((SUBLANES, LANES), F32),
        ],
        scratch_shapes=[pltpu.VMEM((SUBLANES, LANES), F32)],
        compiler_params=_cparams("arbitrary"),
        name="moe_router",
    )(x, vec, rw)


def _row_copy(src_ref, s, dst_ref, d, sem):
    return pltpu.make_async_copy(src_ref.at[pl.ds(s, 1)], dst_ref.at[pl.ds(d, 1)], sem)


def _dispatch_kernel(dest_ref, h_ref, xs_in_ref, xs_ref, sem):
    del xs_in_ref
    tm = h_ref.shape[0]
    base = pl.program_id(0) * tm

    def start(t, c):
        for s in range(TOP_K):
            _row_copy(h_ref, t, xs_ref, dest_ref[TOP_K * (base + t) + s], sem).start()
        return c

    def wait(t, c):
        for s in range(TOP_K):
            _row_copy(h_ref, t, xs_ref, dest_ref[TOP_K * (base + t) + s], sem).wait()
        return c

    lax.fori_loop(0, tm, start, 0)
    lax.fori_loop(0, tm, wait, 0)


def moe_dispatch(h, dest, n_rows, *, tm):
    R, D = h.shape
    return pl.pallas_call(
        _dispatch_kernel,
        grid_spec=pltpu.PrefetchScalarGridSpec(
            num_scalar_prefetch=1,
            grid=(R // tm,),
            in_specs=[pl.BlockSpec((tm, D), lambda i, dest: (i, 0)),
                      pl.BlockSpec(memory_space=pl.ANY)],
            out_specs=pl.BlockSpec(memory_space=pl.ANY),
            scratch_shapes=[pltpu.SemaphoreType.DMA(())],
        ),
        out_shape=jax.ShapeDtypeStruct((n_rows, D), F32),
        input_output_aliases={2: 0},
        compiler_params=_cparams("arbitrary"),
        name="moe_dispatch",
    )(dest, h, jnp.zeros((n_rows, D), F32))


def _gffn_kernel(te_ref, nv_ref, xs_ref, wg_ref, wu_ref, wd_ref, ys_ref, hb_ref, acc_ref):
    t = pl.program_id(0)
    j = pl.program_id(1)
    valid = t < nv_ref[0]

    @pl.when(valid & (j == 0))
    def _():
        hb_ref[...] = xs_ref[...].astype(BF16)
        acc_ref[...] = jnp.zeros(acc_ref.shape, F32)

    @pl.when(valid)
    def _():
        h = hb_ref[...]
        g = jnp.dot(h, wg_ref[0], preferred_element_type=F32)
        u = jnp.dot(h, wu_ref[0], preferred_element_type=F32)
        acc_ref[...] += jnp.dot((_silu(g) * u).astype(BF16), wd_ref[0], preferred_element_type=F32)

    last = j == pl.num_programs(1) - 1

    @pl.when(valid & last)
    def _():
        ys_ref[...] = acc_ref[...]

    @pl.when(jnp.logical_not(valid) & last)
    def _():
        ys_ref[...] = jnp.zeros(ys_ref.shape, F32)


def moe_grouped_ffn(xs, tile_expert, n_valid, wg, wu, wd, *, tm, tf):
    n_rows, D = xs.shape
    F = wg.shape[2]
    nf = F // tf
    assert n_rows % tm == 0 and F % tf == 0

    def wcol(t, j, te, nv):
        return (te[t], 0, jnp.where(t < nv[0], j, nf - 1))

    def wrow(t, j, te, nv):
        return (te[t], jnp.where(t < nv[0], j, nf - 1), 0)

    return pl.pallas_call(
        _gffn_kernel,
        grid_spec=pltpu.PrefetchScalarGridSpec(
            num_scalar_prefetch=2,
            grid=(n_rows // tm, nf),
            in_specs=[
                pl.BlockSpec((tm, D), lambda t, j, te, nv: (t, 0)),
                pl.BlockSpec((1, D, tf), wcol),
                pl.BlockSpec((1, D, tf), wcol),
                pl.BlockSpec((1, tf, D), wrow),
            ],
            out_specs=pl.BlockSpec((tm, D), lambda t, j, te, nv: (t, 0)),
            scratch_shapes=[pltpu.VMEM((tm, D), BF16), pltpu.VMEM((tm, D), F32)],
        ),
        out_shape=jax.ShapeDtypeStruct((n_rows, D), F32),
        compiler_params=_cparams("parallel", "arbitrary"),
        name="moe_grouped_ffn",
    )(tile_expert, n_valid, xs, wg, wu, wd)


def _combine_kernel(dest_ref, x_ref, meta_ref, vec_ref, ys_ref, o_ref, b0_ref, b1_ref, sem, *, final_norm):
    tm = x_ref.shape[0]
    base = pl.program_id(0) * tm
    bufs = (b0_ref, b1_ref)

    def start(t, c):
        for s in range(TOP_K):
            _row_copy(ys_ref, dest_ref[TOP_K * (base + t) + s], bufs[s], t, sem).start()
        return c

    def wait(t, c):
        for s in range(TOP_K):
            _row_copy(ys_ref, dest_ref[TOP_K * (base + t) + s], bufs[s], t, sem).wait()
        return c

    lax.fori_loop(0, tm, start, 0)
    lax.fori_loop(0, tm, wait, 0)
    y = meta_ref[:, 2:3] * b0_ref[...] + meta_ref[:, 3:4] * b1_ref[...]
    out = x_ref[...] + vec_ref[3:4] * y
    if final_norm:
        ms = jnp.mean(out * out, axis=-1, keepdims=True)
        out = out * lax.rsqrt(ms + EPS) * vec_ref[4:5]
    o_ref[...] = out


def moe_combine(x, meta, vec, ys, dest, *, tm, final_norm):
    R, D = x.shape
    return pl.pallas_call(
        functools.partial(_combine_kernel, final_norm=final_norm),
        grid_spec=pltpu.PrefetchScalarGridSpec(
            num_scalar_prefetch=1,
            grid=(R // tm,),
            in_specs=[
                pl.BlockSpec((tm, D), lambda i, dest: (i, 0)),
                pl.BlockSpec((tm, LANES), lambda i, dest: (i, 0)),
                pl.BlockSpec((SUBLANES, D), lambda i, dest: (0, 0)),
                pl.BlockSpec(memory_space=pl.ANY),
            ],
            out_specs=pl.BlockSpec((tm, D), lambda i, dest: (i, 0)),
            scratch_shapes=[pltpu.VMEM((tm, D), F32), pltpu.VMEM((tm, D), F32),
                            pltpu.SemaphoreType.DMA(())],
        ),
        out_shape=jax.ShapeDtypeStruct((R, D), F32),
        compiler_params=_cparams("arbitrary"),
        name="moe_combine",
    )(dest, x, meta, vec, ys)


def moe_layer(x, vec, router_w, wg, wu, wd, *, final_norm, tm_route, tm_rows, tm_group, tf):
    R, D = x.shape
    n_exp = router_w.shape[1]
    h, meta, counts = moe_router(x, vec, router_w, tm=tm_route)
    counts = counts[0, :n_exp].astype(jnp.int32)
    padded = ((counts + tm_group - 1) // tm_group) * tm_group
    ends = jnp.cumsum(padded)
    offsets = ends - padded
    eid = meta[:, 0:TOP_K].astype(jnp.int32)
    rank = meta[:, 4:4 + TOP_K].astype(jnp.int32)
    dest = (offsets[eid] + rank).reshape(-1)
    n_tiles = (TOP_K * R) // tm_group + n_exp
    tile_start = jnp.arange(n_tiles, dtype=jnp.int32) * tm_group
    n_valid = ends[-1] // tm_group
    tile_expert = jnp.sum(tile_start[:, None] >= ends[None, :], axis=1).astype(jnp.int32)
    last_expert = jnp.take(tile_expert, jnp.maximum(n_valid - 1, 0))
    tile_expert = jnp.where(jnp.arange(n_tiles) < n_valid, tile_expert, last_expert)
    xs = moe_dispatch(h, dest, n_tiles * tm_group, tm=tm_rows)
    ys = moe_grouped_ffn(xs, tile_expert, n_valid.reshape(1).astype(jnp.int32), wg, wu, wd,
                         tm=tm_group, tf=tf)
    return moe_combine(x, meta, vec, ys, dest, tm=tm_rows, final_norm=final_norm)


def _final_norm_kernel(x_ref, g_ref, o_ref):
    x = x_ref[...]
    ms = jnp.mean(x * x, axis=-1, keepdims=True)
    o_ref[...] = x * lax.rsqrt(ms + EPS) * g_ref[0:1]


def final_norm(x, vec, *, tm):
    R, D = x.shape
    return pl.pallas_call(
        _final_norm_kernel,
        grid=(R // tm,),
        in_specs=[pl.BlockSpec((tm, D), lambda i: (i, 0)), pl.BlockSpec((SUBLANES, D), lambda i: (0, 0))],
        out_specs=pl.BlockSpec((tm, D), lambda i: (i, 0)),
        out_shape=jax.ShapeDtypeStruct((R, D), F32),
        compiler_params=_cparams("parallel"),
        name="final_norm",
    )(x, vec)


def _rope_tables(n_tokens):
    rows = n_tokens // GRID_W
    row = jnp.repeat(jnp.arange(rows, dtype=F32), GRID_W)
    col = jnp.tile(jnp.arange(GRID_W, dtype=F32), rows)
    inv = ROPE_THETA ** (-jnp.arange(ROPE_FREQS, dtype=F32) / ROPE_FREQS)
    ar = row[:, None] * inv
    ac = col[:, None] * inv
    ang = jnp.concatenate([ar, ar, ac, ac], axis=-1)
    cos, sin = jnp.cos(ang), jnp.sin(ang)
    lane = jnp.arange(HEAD_DIM) % (2 * ROPE_FREQS)
    lo = lane < ROPE_FREQS
    return cos, jnp.where(lo, -sin, 0.0), jnp.where(lo, 0.0, sin)


def _vec(*rows):
    d = rows[0].shape[-1]
    out = jnp.zeros((SUBLANES, d), F32)
    for i, r in enumerate(rows):
        out = out.at[i].set(r.reshape(d))
    return out


def _tile(n, pref):
    t = pref
    while n % t:
        t //= 2
    return t


def kernel(x, c, ctx, c_ctx, ada_w, ada_b, norm_attn_g, norm_ffn_g, w_in, b_gate, q_norm_g, k_norm_g,
           w_attn_proj, w_four_proj, w_out, ffn_w_gate, ffn_w_up, ffn_w_down, router_w, moe_w_gate,
           moe_w_up, moe_w_down, final_norm_g):
    B, S, D = x.shape
    assert B == 1 and c.shape[0] == 1 and ctx.shape[0] == 1
    depth = ada_w.shape[0]
    q_w = w_attn_proj.shape[1]
    four_w = w_four_proj.shape[1]
    kv_w = (w_in.shape[2] - q_w - four_w - 2 * D) // 2
    n_kv = kv_w // HEAD_DIM

    cond = jnp.zeros((SUBLANES, D), F32).at[0].set(c[0]).at[1].set(c_ctx)
    mods = ada_modulation(cond, ada_w, ada_b)
    rope = _rope_tables(S)
    xl = x[0]
    xc = ctx[0]
    C = xc.shape[0]

    def mod(l, r, k):
        return mods[l, r, k * D:(k + 1) * D]

    def project(l, xs_, r, rope_, tm):
        vec = _vec(norm_attn_g[l], mod(l, r, 0), mod(l, r, 1))
        qkg = _vec(q_norm_g[l], k_norm_g[l])
        return in_projection(xs_, vec, w_in[l].astype(BF16), b_gate[l].reshape(1, -1), qkg, rope_,
                             q_w=q_w, kv_w=kv_w, four_w=four_w, tm=tm)

    def finish_mixer(l, xs_, r, proj, extra, tm):
        qt, k, vt, f, gates = proj
        n = xs_.shape[0]
        tq, tk = _tile(n, 512), _tile(n, 512)
        if extra is None:
            attn = attention(qt, k, vt, None, n_kv=n_kv, tq=tq, tk=tk)
        else:
            bound = (SCORE_BOUND_SLACK * HEAD_DIM * ATTN_SCALE * LOG2E
                     * jnp.max(jnp.abs(q_norm_g[l])) * jnp.max(jnp.abs(k_norm_g[l]))).reshape(1)
            attn = lax.cond(
                bound[0] < MAX_SCORE_BOUND,
                lambda: attention_bounded(qt, k, vt, extra, bound, n_kv=n_kv, tq=tq, tk=tk),
                lambda: attention(qt, k, vt, extra, n_kv=n_kv, tq=tq, tk=tk))
        four = fourier_mix(f)
        return merge(xs_, attn, four, gates, _vec(mod(l, r, 2)), w_attn_proj[l].astype(BF16),
                     w_four_proj[l].astype(BF16), w_out[l].astype(BF16), tm=tm)

    def channel_mixer(l, xs_, r, fin, tm):
        i = l // 2
        vec = _vec(norm_ffn_g[l], mod(l, r, 3), mod(l, r, 4), mod(l, r, 5), final_norm_g)
        if l % 2 == 0:
            y = dense_ffn(xs_, vec, ffn_w_gate[i].astype(BF16), ffn_w_up[i].astype(BF16),
                          ffn_w_down[i].astype(BF16), tm=tm, tf=512)
            if fin:
                y = final_norm(y, _vec(final_norm_g), tm=tm)
            return y
        f_exp = moe_w_gate.shape[3]
        tf = 768
        f_pad = ((f_exp + tf - 1) // tf) * tf
        wg = cast_pad(moe_w_gate[i], f_pad, axis=2)
        wu = cast_pad(moe_w_up[i], f_pad, axis=2)
        wd = cast_pad(moe_w_down[i], f_pad, axis=1)
        return moe_layer(xs_, vec, router_w[i], wg, wu, wd, final_norm=fin,
                         tm_route=_tile(xs_.shape[0], 512), tm_rows=_tile(xs_.shape[0], 256),
                         tm_group=512, tf=tf)

    for l in range(depth):
        last = l == depth - 1
        tmc = _tile(C, 256)
        proj_c = project(l, xc, 1, None, tmc)
        if not last:
            xc_next = finish_mixer(l, xc, 1, proj_c, None, tmc)
            xc_next = channel_mixer(l, xc_next, 1, False, tmc)
        proj = project(l, xl, 0, rope, _tile(S, 256))
        xl = finish_mixer(l, xl, 0, proj, (proj_c[1], proj_c[2]), _tile(S, 256))
        xl = channel_mixer(l, xl, 0, last, _tile(S, 512))
        if not last:
            xc = xc_next
    return xl[None]
```

```python
import functools
import math

import numpy as np
import jax
import jax.numpy as jnp
from jax import lax
from jax.experimental import pallas as pl
from jax.experimental.pallas import tpu as pltpu

F32 = jnp.float32
BF16 = jnp.bfloat16

HEAD_DIM = 128
GRID_W = 64
ROPE_THETA = 10000.0
ROPE_FREQS = HEAD_DIM // 4
ATTN_SCALE = HEAD_DIM ** -0.5
N_FOURIER_GROUPS = 4
TOP_K = 2
EPS = 1e-6

LANES = 128
SUBLANES = 8
VMEM_LIMIT_BYTES = 58 * 1024 * 1024
NEG_INF = -1e30
LOG2E = math.log2(math.e)
BF16_SUBLANES = 16
VT_ROWS = HEAD_DIM + BF16_SUBLANES
SCORE_BOUND_SLACK = 1.02
ROW_DMA_UNROLL = 8
MAX_SCORE_BOUND = 60.0


def _cparams(*sem):
    return pltpu.CompilerParams(dimension_semantics=sem, vmem_limit_bytes=VMEM_LIMIT_BYTES)


def _norm_mod(x, g, shift, scale):
    ms = jnp.mean(x * x, axis=-1, keepdims=True)
    return (x * lax.rsqrt(ms + EPS) * g) * (1.0 + scale) + shift


def _silu(x):
    return x * jax.nn.sigmoid(x)


def _ada_kernel(cond_ref, w_ref, b_ref, o_ref):
    s = _silu(cond_ref[...]).astype(BF16)
    w = w_ref[0].astype(BF16)
    o_ref[0] = jnp.dot(s, w, preferred_element_type=F32) + b_ref[0]


def ada_modulation(cond, ada_w, ada_b, tn=1024):
    L, D, N = ada_w.shape
    return pl.pallas_call(
        _ada_kernel,
        grid=(L, N // tn),
        in_specs=[
            pl.BlockSpec((SUBLANES, D), lambda l, j: (0, 0)),
            pl.BlockSpec((1, D, tn), lambda l, j: (l, 0, j)),
            pl.BlockSpec((1, 1, tn), lambda l, j: (l, 0, j)),
        ],
        out_specs=pl.BlockSpec((1, SUBLANES, tn), lambda l, j: (l, 0, j)),
        out_shape=jax.ShapeDtypeStruct((L, SUBLANES, N), F32),
        compiler_params=_cparams("parallel", "parallel"),
        name="ada_modulation",
    )(cond, ada_w, ada_b.reshape(L, 1, N))


def _inproj_kernel(x_ref, vec_ref, w_ref, bg_ref, qkg_ref, cos_ref, sa_ref, sb_ref,
                   qt_ref, k_ref, vt_ref, f_ref, gates_ref, hs_ref, *, q_w, kv_w, four_w, tn, use_rope):
    hs_ref[...] = _norm_mod(x_ref[...], vec_ref[0:1], vec_ref[1:2], vec_ref[2:3]).astype(BF16)
    n_tiles = w_ref.shape[1] // tn
    gate0 = q_w + 2 * kv_w + four_w

    def matmul(j):
        return jnp.dot(hs_ref[...], w_ref[:, j * tn:(j + 1) * tn], preferred_element_type=F32)

    def head_norm(t, g, scale):
        ms = jnp.mean(t * t, axis=-1, keepdims=True)
        y = t * lax.rsqrt(ms + EPS) * g
        if use_rope:
            y = (y * cos_ref[...] + pltpu.roll(y, HEAD_DIM - ROPE_FREQS, 1) * sa_ref[...]
                 + pltpu.roll(y, ROPE_FREQS, 1) * sb_ref[...])
        return y * scale if scale != 1.0 else y

    def epilogue(j, acc):
        for c in range(0, tn, HEAD_DIM):
            col = j * tn + c
            t = acc[:, c:c + HEAD_DIM]
            if col < q_w:
                qt_ref[col:col + HEAD_DIM, :] = head_norm(t, qkg_ref[0:1], ATTN_SCALE * LOG2E).T.astype(BF16)
            elif col < q_w + kv_w:
                k_ref[:, col - q_w:col - q_w + HEAD_DIM] = head_norm(t, qkg_ref[1:2], 1.0).astype(BF16)
            elif col < q_w + 2 * kv_w:
                r0 = (col - q_w - kv_w) // HEAD_DIM * VT_ROWS
                vt_ref[r0:r0 + HEAD_DIM, :] = t.T.astype(BF16)
                vt_ref[r0 + HEAD_DIM:r0 + VT_ROWS, :] = jnp.ones((VT_ROWS - HEAD_DIM, vt_ref.shape[1]), BF16)
            elif col < gate0:
                f_ref[:, col - gate0 + four_w:col - gate0 + four_w + HEAD_DIM] = t.astype(BF16)
            else:
                g0 = col - gate0
                gates_ref[:, g0:g0 + HEAD_DIM] = jax.nn.sigmoid(t + bg_ref[:, g0:g0 + HEAD_DIM]).astype(BF16)

    acc = matmul(0)
    for j in range(n_tiles):
        nxt = matmul(j + 1) if j + 1 < n_tiles else None
        epilogue(j, acc)
        acc = nxt


def in_projection(x, vec, w_in, b_gate, qkg, rope, *, q_w, kv_w, four_w, tm, tn=1024):
    R, D = x.shape
    n_cols = w_in.shape[1]
    n_kv = kv_w // HEAD_DIM
    assert n_cols == q_w + 2 * kv_w + four_w + 2 * D and n_cols % tn == 0 and R % tm == 0 and tm % LANES == 0
    use_rope = rope is not None
    if rope is None:
        rope = (jnp.zeros((SUBLANES, HEAD_DIM), F32),) * 3
        rope_spec = pl.BlockSpec((SUBLANES, HEAD_DIM), lambda i: (0, 0))
    else:
        rope_spec = pl.BlockSpec((tm, HEAD_DIM), lambda i: (i, 0))
    kern = functools.partial(_inproj_kernel, q_w=q_w, kv_w=kv_w, four_w=four_w, tn=tn, use_rope=use_rope)
    return pl.pallas_call(
        kern,
        grid=(R // tm,),
        in_specs=[
            pl.BlockSpec((tm, D), lambda i: (i, 0)),
            pl.BlockSpec((SUBLANES, D), lambda i: (0, 0)),
            pl.BlockSpec((D, n_cols), lambda i: (0, 0), pipeline_mode=pl.Buffered(1)),
            pl.BlockSpec((1, 2 * D), lambda i: (0, 0)),
            pl.BlockSpec((SUBLANES, HEAD_DIM), lambda i: (0, 0)),
            rope_spec, rope_spec, rope_spec,
        ],
        out_specs=[
            pl.BlockSpec((q_w, tm), lambda i: (0, i)),
            pl.BlockSpec((tm, kv_w), lambda i: (i, 0)),
            pl.BlockSpec((n_kv * VT_ROWS, tm), lambda i: (0, i)),
            pl.BlockSpec((tm, four_w), lambda i: (i, 0)),
            pl.BlockSpec((tm, 2 * D), lambda i: (i, 0)),
        ],
        out_shape=[
            jax.ShapeDtypeStruct((q_w, R), BF16),
            jax.ShapeDtypeStruct((R, kv_w), BF16),
            jax.ShapeDtypeStruct((n_kv * VT_ROWS, R), BF16),
            jax.ShapeDtypeStruct((R, four_w), BF16),
            jax.ShapeDtypeStruct((R, 2 * D), BF16),
        ],
        scratch_shapes=[pltpu.VMEM((tm, D), BF16)],
        compiler_params=_cparams("parallel"),
        name="in_projection",
    )(x, vec, w_in, b_gate, qkg, *rope)


def _attn_kernel(*refs, group, tk, n_chunks, has_extra):
    if has_extra:
        qt_ref, k_ref, vt_ref, kx_ref, vxt_ref, o_ref, m_ref, acc_ref, sa_ref, sb_ref, sx_ref = refs
    else:
        qt_ref, k_ref, vt_ref, o_ref, m_ref, acc_ref, sa_ref, sb_ref = refs
    m_ref[...] = jnp.full(m_ref.shape, NEG_INF, F32)
    acc_ref[...] = jnp.zeros(acc_ref.shape, F32)

    def scores(k, s_ref):
        for g in range(group):
            s_ref[g] = jnp.dot(k, qt_ref[g * HEAD_DIM:(g + 1) * HEAD_DIM, :], preferred_element_type=F32)

    def softmax_pv(vt, s_ref):
        for g in range(group):
            s = s_ref[g]
            m_prev = m_ref[g]
            m_new = jnp.maximum(m_prev, jnp.max(s, axis=0, keepdims=True))
            alpha = jnp.exp2(m_prev - m_new)
            p = jnp.exp2((s - m_new).astype(BF16))
            acc_ref[g] = alpha * acc_ref[g] + jnp.dot(vt, p, preferred_element_type=F32)
            m_ref[g] = m_new

    def k_chunk(c):
        return k_ref[pl.ds(pl.multiple_of(c * tk, tk), tk), :]

    def vt_chunk(c):
        return vt_ref[:, pl.ds(pl.multiple_of(c * tk, tk), tk)]

    n_pairs = (n_chunks - 1) // 2
    scores(k_chunk(0), sa_ref)

    def pair(i, carry):
        c = 2 * i
        scores(k_chunk(c + 1), sb_ref)
        softmax_pv(vt_chunk(c), sa_ref)
        scores(k_chunk(c + 2), sa_ref)
        softmax_pv(vt_chunk(c + 1), sb_ref)
        return carry

    lax.fori_loop(0, n_pairs, pair, 0)
    c = 2 * n_pairs
    if n_chunks - c == 2:
        scores(k_chunk(c + 1), sb_ref)
    softmax_pv(vt_chunk(c), sa_ref)
    if has_extra:
        scores(kx_ref[...], sx_ref)
    if n_chunks - c == 2:
        softmax_pv(vt_chunk(c + 1), sb_ref)
    if has_extra:
        softmax_pv(vxt_ref[...], sx_ref)
    for g in range(group):
        o = acc_ref[g, :HEAD_DIM, :] / acc_ref[g, HEAD_DIM:HEAD_DIM + 1, :]
        o_ref[:, g * HEAD_DIM:(g + 1) * HEAD_DIM] = o.T.astype(BF16)


def attention(qt, k, vt, extra, *, n_kv, tq, tk):
    q_w, R = qt.shape
    group = q_w // HEAD_DIM // n_kv
    gw = group * HEAD_DIM
    assert R % tq == 0 and R % tk == 0
    in_specs = [
        pl.BlockSpec((gw, tq), lambda h, i: (h, i)),
        pl.BlockSpec((R, HEAD_DIM), lambda h, i: (0, h)),
        pl.BlockSpec((VT_ROWS, R), lambda h, i: (h, 0)),
    ]
    args = [qt, k, vt]
    scratch = [
        pltpu.VMEM((group, 1, tq), F32),
        pltpu.VMEM((group, VT_ROWS, tq), F32),
        pltpu.VMEM((group, tk, tq), F32),
        pltpu.VMEM((group, tk, tq), F32),
    ]
    if extra is not None:
        rx = extra[0].shape[0]
        in_specs += [
            pl.BlockSpec((rx, HEAD_DIM), lambda h, i: (0, h)),
            pl.BlockSpec((VT_ROWS, rx), lambda h, i: (h, 0)),
        ]
        args += list(extra)
        scratch.append(pltpu.VMEM((group, rx, tq), F32))
    kern = functools.partial(_attn_kernel, group=group, tk=tk, n_chunks=R // tk, has_extra=extra is not None)
    return pl.pallas_call(
        kern,
        grid=(n_kv, R // tq),
        in_specs=in_specs,
        out_specs=pl.BlockSpec((tq, gw), lambda h, i: (i, h)),
        out_shape=jax.ShapeDtypeStruct((R, q_w), BF16),
        scratch_shapes=scratch,
        compiler_params=_cparams("parallel", "parallel"),
        name="attention",
    )(*args)


def _attn_bounded_kernel(bound_ref, qt_ref, k_ref, vt_ref, kx_ref, vxt_ref, o_ref, acc_ref, sa_ref, sb_ref, sx_ref,
                         *, group, tk, n_chunks):
    acc_ref[...] = jnp.zeros(acc_ref.shape, F32)
    bound = bound_ref[0]

    def scores(k, s_ref, g):
        s_ref[g] = jnp.dot(k, qt_ref[g * HEAD_DIM:(g + 1) * HEAD_DIM, :], preferred_element_type=F32)

    def exp_pv(vt, s_ref, g):
        p = jnp.exp2((s_ref[g] - bound).astype(BF16))
        acc_ref[g] += jnp.dot(vt, p, preferred_element_type=F32)

    def k_chunk(c):
        return k_ref[pl.ds(pl.multiple_of(c * tk, tk), tk), :]

    def vt_chunk(c):
        return vt_ref[:, pl.ds(pl.multiple_of(c * tk, tk), tk)]

    def step(k_next, s_next, vt_cur, s_cur):
        for g in range(group):
            if k_next is not None:
                scores(k_next, s_next, g)
            exp_pv(vt_cur, s_cur, g)

    n_pairs = (n_chunks - 1) // 2
    for g in range(group):
        scores(k_chunk(0), sa_ref, g)

    def pair(i, carry):
        c = 2 * i
        step(k_chunk(c + 1), sb_ref, vt_chunk(c), sa_ref)
        step(k_chunk(c + 2), sa_ref, vt_chunk(c + 1), sb_ref)
        return carry

    lax.fori_loop(0, n_pairs, pair, 0)
    c = 2 * n_pairs
    if n_chunks - c == 2:
        step(k_chunk(c + 1), sb_ref, vt_chunk(c), sa_ref)
        step(kx_ref[...], sx_ref, vt_chunk(c + 1), sb_ref)
    else:
        step(kx_ref[...], sx_ref, vt_chunk(c), sa_ref)
    step(None, None, vxt_ref[...], sx_ref)
    for g in range(group):
        o = acc_ref[g, :HEAD_DIM, :] / acc_ref[g, HEAD_DIM:HEAD_DIM + 1, :]
        o_ref[:, g * HEAD_DIM:(g + 1) * HEAD_DIM] = o.T.astype(BF16)


def attention_bounded(qt, k, vt, extra, bound, *, n_kv, tq, tk):
    q_w, R = qt.shape
    group = q_w // HEAD_DIM // n_kv
    gw = group * HEAD_DIM
    rx = extra[0].shape[0]
    assert R % tq == 0 and R % tk == 0
    kern = functools.partial(_attn_bounded_kernel, group=group, tk=tk, n_chunks=R // tk)
    return pl.pallas_call(
        kern,
        grid=(n_kv, R // tq),
        in_specs=[
            pl.BlockSpec(memory_space=pltpu.SMEM),
            pl.BlockSpec((gw, tq), lambda h, i: (h, i)),
            pl.BlockSpec((R, HEAD_DIM), lambda h, i: (0, h)),
            pl.BlockSpec((VT_ROWS, R), lambda h, i: (h, 0)),
            pl.BlockSpec((rx, HEAD_DIM), lambda h, i: (0, h)),
            pl.BlockSpec((VT_ROWS, rx), lambda h, i: (h, 0)),
        ],
        out_specs=pl.BlockSpec((tq, gw), lambda h, i: (i, h)),
        out_shape=jax.ShapeDtypeStruct((R, q_w), BF16),
        scratch_shapes=[pltpu.VMEM((group, VT_ROWS, tq), F32), pltpu.VMEM((group, tk, tq), F32),
                        pltpu.VMEM((group, tk, tq), F32), pltpu.VMEM((group, rx, tq), F32)],
        compiler_params=_cparams("parallel", "parallel"),
        name="attention_bounded",
    )(bound, qt, k, vt, *extra)


def _dft_mats(n, scale):
    k = np.arange(n)
    ang = 2.0 * np.pi * ((k[:, None] * k[None, :]) % n) / n
    return (jnp.asarray(np.cos(ang) * scale, BF16), jnp.asarray(np.sin(ang) * scale, BF16))


def _four_small_kernel(u_ref, cn_ref, sn_ref, cc_ref, sc_ref, o_ref, *, gc):
    u = u_ref[...]
    for g in range(u.shape[1] // gc):
        ug = u[:, g * gc:(g + 1) * gc]
        a = jnp.dot(ug, cc_ref[...], preferred_element_type=F32).astype(BF16)
        b = jnp.dot(ug, sc_ref[...], preferred_element_type=F32).astype(BF16)
        o = (jnp.dot(cn_ref[...], a, preferred_element_type=F32)
             - jnp.dot(sn_ref[...], b, preferred_element_type=F32))
        o_ref[:, g * gc:(g + 1) * gc] = o.astype(BF16)


def fourier_mix_small(u):
    n, c = u.shape
    gc = c // N_FOURIER_GROUPS
    cn, sn = _dft_mats(n, n ** -0.5)
    cc, sc = _dft_mats(gc, gc ** -0.5)
    full = lambda a: pl.BlockSpec(a.shape, lambda: (0,) * a.ndim)
    return pl.pallas_call(
        functools.partial(_four_small_kernel, gc=gc),
        in_specs=[full(u), full(cn), full(sn), full(cc), full(sc)],
        out_specs=pl.BlockSpec((n, c), lambda: (0, 0)),
        out_shape=jax.ShapeDtypeStruct((n, c), BF16),
        compiler_params=pltpu.CompilerParams(vmem_limit_bytes=VMEM_LIMIT_BYTES),
        name="fourier_small",
    )(u, cn, sn, cc, sc)


def _four_stage1_kernel(u_ref, ca_ref, sa_ref, cw_ref, sw_ref, yr_ref, yi_ref, *, c, tb):
    u = u_ref[...]
    yr = jnp.dot(ca_ref[...], u, preferred_element_type=F32)
    yi = -jnp.dot(sa_ref[...], u, preferred_element_type=F32)
    for b in range(tb):
        cw = cw_ref[:, b * LANES:(b + 1) * LANES]
        sw = sw_ref[:, b * LANES:(b + 1) * LANES]
        for l in range(c // LANES):
            sl = slice(b * c + l * LANES, b * c + (l + 1) * LANES)
            r, i = yr[:, sl], yi[:, sl]
            yr_ref[:, sl] = (r * cw + i * sw).astype(BF16)
            yi_ref[:, sl] = (i * cw - r * sw).astype(BF16)


def _four_stage2_kernel(yr_ref, yi_ref, cb_ref, sb_ref, cc_ref, sc_ref, o_ref, *, nb, c, gc, tk1):
    for t in range(tk1):
        yr = yr_ref[t * nb:(t + 1) * nb, :]
        yi = yi_ref[t * nb:(t + 1) * nb, :]
        pr = (jnp.dot(cb_ref[...], yr, preferred_element_type=F32)
              + jnp.dot(sb_ref[...], yi, preferred_element_type=F32)).astype(BF16)
        pi = (jnp.dot(cb_ref[...], yi, preferred_element_type=F32)
              - jnp.dot(sb_ref[...], yr, preferred_element_type=F32)).astype(BF16)
        for g in range(c // gc):
            sl = slice(g * gc, (g + 1) * gc)
            o = (jnp.dot(pr[:, sl], cc_ref[...], preferred_element_type=F32)
                 + jnp.dot(pi[:, sl], sc_ref[...], preferred_element_type=F32))
            o_ref[:, t * c + g * gc:t * c + (g + 1) * gc] = o.astype(BF16)


def fourier_mix_large(u, *, na=128, tb=4, tk1=4):
    n, c = u.shape
    nb = n // na
    assert na * nb == n and nb % tb == 0 and na % tk1 == 0 and nb % SUBLANES == 0
    gc = c // N_FOURIER_GROUPS
    ca, sa = _dft_mats(na, na ** -0.5)
    cb, sb = _dft_mats(nb, nb ** -0.5)
    cc, sc = _dft_mats(gc, gc ** -0.5)
    k1 = lax.broadcasted_iota(jnp.int32, (na, nb * LANES), 0)
    b = lax.broadcasted_iota(jnp.int32, (na, nb * LANES), 1) // LANES
    ang = ((k1 * b) % n).astype(F32) * (2.0 * math.pi / n)
    cw, sw = jnp.cos(ang), jnp.sin(ang)
    const = lambda a: pl.BlockSpec(a.shape, lambda i: (0,) * a.ndim)
    yr, yi = pl.pallas_call(
        functools.partial(_four_stage1_kernel, c=c, tb=tb),
        grid=(nb // tb,),
        in_specs=[
            pl.BlockSpec((na, tb * c), lambda i: (0, i)),
            const(ca), const(sa),
            pl.BlockSpec((na, tb * LANES), lambda i: (0, i)),
            pl.BlockSpec((na, tb * LANES), lambda i: (0, i)),
        ],
        out_specs=[pl.BlockSpec((na, tb * c), lambda i: (0, i))] * 2,
        out_shape=[jax.ShapeDtypeStruct((na, nb * c), BF16)] * 2,
        compiler_params=_cparams("parallel"),
        name="fourier_stage1",
    )(u.reshape(na, nb * c), ca, sa, cw, sw)
    out = pl.pallas_call(
        functools.partial(_four_stage2_kernel, nb=nb, c=c, gc=gc, tk1=tk1),
        grid=(na // tk1,),
        in_specs=[
            pl.BlockSpec((tk1 * nb, c), lambda i: (i, 0)),
            pl.BlockSpec((tk1 * nb, c), lambda i: (i, 0)),
            const(cb), const(sb), const(cc), const(sc),
        ],
        out_specs=pl.BlockSpec((nb, tk1 * c), lambda i: (0, i)),
        out_shape=jax.ShapeDtypeStruct((nb, na * c), BF16),
        compiler_params=_cparams("parallel"),
        name="fourier_stage2",
    )(yr.reshape(na * nb, c), yi.reshape(na * nb, c), cb, sb, cc, sc)
    return out.reshape(n, c)


def fourier_mix(u):
    n = u.shape[0]
    if n % (128 * SUBLANES) == 0 and n >= 128 * 16:
        return fourier_mix_large(u)
    return fourier_mix_small(u)


def _merge_kernel(x_ref, attn_ref, four_ref, gates_ref, vec_ref, wa_ref, wf_ref, wo_ref, o_ref):
    d = x_ref.shape[1]
    a = jnp.dot(attn_ref[...], wa_ref[...], preferred_element_type=F32)
    f = jnp.dot(four_ref[...], wf_ref[...], preferred_element_type=F32)
    y = gates_ref[:, :d].astype(F32) * a + gates_ref[:, d:].astype(F32) * f
    y = jnp.dot(y.astype(BF16), wo_ref[...], preferred_element_type=F32)
    o_ref[...] = x_ref[...] + vec_ref[0:1] * y


def merge(x, attn, four, gates, vec, wa, wf, wo, *, tm):
    R, D = x.shape
    row = lambda w: pl.BlockSpec((tm, w), lambda i: (i, 0))
    const = lambda a: pl.BlockSpec(a.shape, lambda i: (0, 0), pipeline_mode=pl.Buffered(1))
    return pl.pallas_call(
        _merge_kernel,
        grid=(R // tm,),
        in_specs=[row(D), row(attn.shape[1]), row(four.shape[1]), row(gates.shape[1]),
                  pl.BlockSpec((SUBLANES, D), lambda i: (0, 0)), const(wa), const(wf), const(wo)],
        out_specs=row(D),
        out_shape=jax.ShapeDtypeStruct((R, D), F32),
        compiler_params=_cparams("parallel"),
        name="merge",
    )(x, attn, four, gates, vec, wa, wf, wo)


def _ffn_kernel(x_ref, vec_ref, wg_ref, wu_ref, wd_ref, o_ref, hs_ref, acc_ref):
    j = pl.program_id(1)

    @pl.when(j == 0)
    def _():
        hs_ref[...] = _norm_mod(x_ref[...], vec_ref[0:1], vec_ref[1:2], vec_ref[2:3]).astype(BF16)
        acc_ref[...] = jnp.zeros(acc_ref.shape, F32)

    h = hs_ref[...]
    g = jnp.dot(h, wg_ref[...], preferred_element_type=F32)
    u = jnp.dot(h, wu_ref[...], preferred_element_type=F32)
    acc_ref[...] += jnp.dot((_silu(g) * u).astype(BF16), wd_ref[...], preferred_element_type=F32)

    @pl.when(j == pl.num_programs(1) - 1)
    def _():
        o_ref[...] = x_ref[...] + vec_ref[3:4] * acc_ref[...]


def dense_ffn(x, vec, wg, wu, wd, *, tm, tf):
    R, D = x.shape
    F = wg.shape[1]
    assert R % tm == 0 and F % tf == 0
    return pl.pallas_call(
        _ffn_kernel,
        grid=(R // tm, F // tf),
        in_specs=[
            pl.BlockSpec((tm, D), lambda i, j: (i, 0)),
            pl.BlockSpec((SUBLANES, D), lambda i, j: (0, 0)),
            pl.BlockSpec((D, tf), lambda i, j: (0, j)),
            pl.BlockSpec((D, tf), lambda i, j: (0, j)),
            pl.BlockSpec((tf, D), lambda i, j: (j, 0)),
        ],
        out_specs=pl.BlockSpec((tm, D), lambda i, j: (i, 0)),
        out_shape=jax.ShapeDtypeStruct((R, D), F32),
        scratch_shapes=[pltpu.VMEM((tm, D), BF16), pltpu.VMEM((tm, D), F32)],
        compiler_params=_cparams("parallel", "arbitrary"),
        name="dense_ffn",
    )(x, vec, wg, wu, wd)


def _router_kernel(x_ref, vec_ref, rw_ref, h_ref, meta_ref, cnt_ref, run_ref, *, n_exp):
    i = pl.program_id(0)

    @pl.when(i == 0)
    def _():
        run_ref[...] = jnp.zeros(run_ref.shape, F32)

    h = _norm_mod(x_ref[...], vec_ref[0:1], vec_ref[1:2], vec_ref[2:3])
    h_ref[...] = h
    tm = h.shape[0]
    logits = [jnp.sum(h * rw_ref[e:e + 1], axis=-1, keepdims=True) for e in range(n_exp)]
    m1 = logits[0]
    for e in range(1, n_exp):
        m1 = jnp.maximum(m1, logits[e])
    i1 = jnp.full((tm, 1), n_exp, jnp.int32)
    for e in reversed(range(n_exp)):
        i1 = jnp.where(logits[e] == m1, e, i1)
    m2 = jnp.full((tm, 1), NEG_INF, F32)
    for e in range(n_exp):
        m2 = jnp.maximum(m2, jnp.where(i1 == e, NEG_INF, logits[e]))
    i2 = jnp.full((tm, 1), n_exp, jnp.int32)
    for e in reversed(range(n_exp)):
        i2 = jnp.where((logits[e] == m2) & (i1 != e), e, i2)
    e2 = jnp.exp(m2 - m1)
    w1 = 1.0 / (1.0 + e2)
    w2 = e2 / (1.0 + e2)

    lane = lax.broadcasted_iota(jnp.int32, (tm, LANES), 1)
    hit1 = lane == i1
    hit2 = lane == i2
    onehot = (hit1 | hit2).astype(F32)
    r = lax.broadcasted_iota(jnp.int32, (tm, tm), 0)
    cidx = lax.broadcasted_iota(jnp.int32, (tm, tm), 1)
    ltri = (cidx < r).astype(BF16)
    before = jnp.dot(ltri, onehot.astype(BF16), preferred_element_type=F32) + run_ref[0:1]
    rank1 = jnp.sum(jnp.where(hit1, before, 0.0), axis=-1, keepdims=True)
    rank2 = jnp.sum(jnp.where(hit2, before, 0.0), axis=-1, keepdims=True)
    run_ref[0:1] = run_ref[0:1] + jnp.sum(onehot, axis=0, keepdims=True)

    meta = jnp.where(lane == 0, i1.astype(F32), 0.0)
    meta = jnp.where(lane == 1, i2.astype(F32), meta)
    meta = jnp.where(lane == 2, w1, meta)
    meta = jnp.where(lane == 3, w2, meta)
    meta = jnp.where(lane == 4, rank1, meta)
    meta = jnp.where(lane == 5, rank2, meta)
    meta_ref[...] = meta
    cnt_ref[...] = run_ref[...]


def moe_router(x, vec, router_w, *, tm):
    R, D = x.shape
    n_exp = router_w.shape[1]
    rw = jnp.zeros((SUBLANES * ((n_exp + SUBLANES - 1) // SUBLANES), D), F32).at[:n_exp].set(router_w.T)
    return pl.pallas_call(
        functools.partial(_router_kernel, n_exp=n_exp),
        grid=(R // tm,),
        in_specs=[
            pl.BlockSpec((tm, D), lambda i: (i, 0)),
            pl.BlockSpec((SUBLANES, D), lambda i: (0, 0)),
            pl.BlockSpec(rw.shape, lambda i: (0, 0)),
        ],
        out_specs=[
            pl.BlockSpec((tm, D), lambda i: (i, 0)),
            pl.BlockSpec((tm, LANES), lambda i: (i, 0)),
            pl.BlockSpec((SUBLANES, LANES), lambda i: (0, 0)),
        ],
        out_shape=[
            jax.ShapeDtypeStruct((R, D), F32),
            jax.ShapeDtypeStruct((R, LANES), F32),
            jax.ShapeDtypeStruct((SUBLANES, LANES), F32),
        ],
        scratch_shapes=[pltpu.VMEM((SUBLANES, LANES), F32)],
        compiler_params=_cparams("arbitrary"),
        name="moe_router",
    )(x, vec, rw)


def _row_copy(src_ref, s, dst_ref, d, sem):
    return pltpu.make_async_copy(src_ref.at[pl.ds(s, 1)], dst_ref.at[pl.ds(d, 1)], sem)


def _dispatch_kernel(dest_ref, h_ref, xs_in_ref, xs_ref, sem):
    del xs_in_ref
    tm = h_ref.shape[0]
    base = pl.program_id(0) * tm

    def start(t, c):
        for s in range(TOP_K):
            _row_copy(h_ref, t, xs_ref, dest_ref[TOP_K * (base + t) + s], sem).start()
        return c

    def wait(t, c):
        for s in range(TOP_K):
            _row_copy(h_ref, t, xs_ref, dest_ref[TOP_K * (base + t) + s], sem).wait()
        return c

    lax.fori_loop(0, tm, start, 0, unroll=ROW_DMA_UNROLL)
    lax.fori_loop(0, tm, wait, 0, unroll=ROW_DMA_UNROLL)


def moe_dispatch(h, dest, n_rows, *, tm):
    R, D = h.shape
    return pl.pallas_call(
        _dispatch_kernel,
        grid_spec=pltpu.PrefetchScalarGridSpec(
            num_scalar_prefetch=1,
            grid=(R // tm,),
            in_specs=[pl.BlockSpec((tm, D), lambda i, dest: (i, 0)),
                      pl.BlockSpec(memory_space=pl.ANY)],
            out_specs=pl.BlockSpec(memory_space=pl.ANY),
            scratch_shapes=[pltpu.SemaphoreType.DMA(())],
        ),
        out_shape=jax.ShapeDtypeStruct((n_rows, D), F32),
        input_output_aliases={2: 0},
        compiler_params=_cparams("arbitrary"),
        name="moe_dispatch",
    )(dest, h, jnp.zeros((n_rows, D), F32))


def _gffn_up_kernel(te_ref, nv_ref, xs_ref, wg_ref, wu_ref, a_ref, *, tc):
    valid = pl.program_id(0) < nv_ref[0]

    @pl.when(valid)
    def _():
        h = xs_ref[...].astype(BF16)
        n_cols = a_ref.shape[1]
        for c0 in range(0, n_cols, tc):
            c1 = min(c0 + tc, n_cols)
            g = jnp.dot(h, wg_ref[0, :, c0:c1], preferred_element_type=F32)
            u = jnp.dot(h, wu_ref[0, :, c0:c1], preferred_element_type=F32)
            a_ref[:, c0:c1] = (_silu(g) * u).astype(BF16)

    @pl.when(jnp.logical_not(valid))
    def _():
        a_ref[...] = jnp.zeros(a_ref.shape, BF16)


def _gffn_down_kernel(te_ref, nv_ref, a_ref, wd_ref, ys_ref):
    valid = pl.program_id(0) < nv_ref[0]

    @pl.when(valid)
    def _():
        ys_ref[...] = jnp.dot(a_ref[...], wd_ref[0], preferred_element_type=F32)

    @pl.when(jnp.logical_not(valid))
    def _():
        ys_ref[...] = jnp.zeros(ys_ref.shape, F32)


def moe_grouped_ffn(xs, tile_expert, n_valid, wg, wu, wd, *, tm, tc=512):
    n_rows, D = xs.shape
    F = wg.shape[2]
    assert n_rows % tm == 0 and F % LANES == 0
    expert_w = lambda shape: pl.BlockSpec(shape, lambda t, te, nv: (te[t], 0, 0), pipeline_mode=pl.Buffered(1))
    a = pl.pallas_call(
        functools.partial(_gffn_up_kernel, tc=tc),
        grid_spec=pltpu.PrefetchScalarGridSpec(
            num_scalar_prefetch=2,
            grid=(n_rows // tm,),
            in_specs=[
                pl.BlockSpec((tm, D), lambda t, te, nv: (t, 0)),
                expert_w((1, D, F)),
                expert_w((1, D, F)),
            ],
            out_specs=pl.BlockSpec((tm, F), lambda t, te, nv: (t, 0)),
        ),
        out_shape=jax.ShapeDtypeStruct((n_rows, F), BF16),
        compiler_params=_cparams("parallel"),
        name="moe_ffn_up",
    )(tile_expert, n_valid, xs, wg, wu)
    return pl.pallas_call(
        _gffn_down_kernel,
        grid_spec=pltpu.PrefetchScalarGridSpec(
            num_scalar_prefetch=2,
            grid=(n_rows // tm,),
            in_specs=[
                pl.BlockSpec((tm, F), lambda t, te, nv: (t, 0)),
                pl.BlockSpec((1, F, D), lambda t, te, nv: (te[t], 0, 0)),
            ],
            out_specs=pl.BlockSpec((tm, D), lambda t, te, nv: (t, 0)),
        ),
        out_shape=jax.ShapeDtypeStruct((n_rows, D), F32),
        compiler_params=_cparams("parallel"),
        name="moe_ffn_down",
    )(tile_expert, n_valid, a, wd)


def _combine_kernel(dest_ref, x_ref, meta_ref, vec_ref, ys_ref, o_ref, b0_ref, b1_ref, sem, *, final_norm):
    tm = x_ref.shape[0]
    base = pl.program_id(0) * tm
    bufs = (b0_ref, b1_ref)

    def start(t, c):
        for s in range(TOP_K):
            _row_copy(ys_ref, dest_ref[TOP_K * (base + t) + s], bufs[s], t, sem).start()
        return c

    def wait(t, c):
        for s in range(TOP_K):
            _row_copy(ys_ref, dest_ref[TOP_K * (base + t) + s], bufs[s], t, sem).wait()
        return c

    lax.fori_loop(0, tm, start, 0, unroll=ROW_DMA_UNROLL)
    lax.fori_loop(0, tm, wait, 0, unroll=ROW_DMA_UNROLL)
    y = meta_ref[:, 2:3] * b0_ref[...] + meta_ref[:, 3:4] * b1_ref[...]
    out = x_ref[...] + vec_ref[3:4] * y
    if final_norm:
        ms = jnp.mean(out * out, axis=-1, keepdims=True)
        out = out * lax.rsqrt(ms + EPS) * vec_ref[4:5]
    o_ref[...] = out


def moe_combine(x, meta, vec, ys, dest, *, tm, final_norm):
    R, D = x.shape
    return pl.pallas_call(
        functools.partial(_combine_kernel, final_norm=final_norm),
        grid_spec=pltpu.PrefetchScalarGridSpec(
            num_scalar_prefetch=1,
            grid=(R // tm,),
            in_specs=[
                pl.BlockSpec((tm, D), lambda i, dest: (i, 0)),
                pl.BlockSpec((tm, LANES), lambda i, dest: (i, 0)),
                pl.BlockSpec((SUBLANES, D), lambda i, dest: (0, 0)),
                pl.BlockSpec(memory_space=pl.ANY),
            ],
            out_specs=pl.BlockSpec((tm, D), lambda i, dest: (i, 0)),
            scratch_shapes=[pltpu.VMEM((tm, D), F32), pltpu.VMEM((tm, D), F32),
                            pltpu.SemaphoreType.DMA(())],
        ),
        out_shape=jax.ShapeDtypeStruct((R, D), F32),
        compiler_params=_cparams("arbitrary"),
        name="moe_combine",
    )(dest, x, meta, vec, ys)


def moe_layer(x, vec, router_w, wg, wu, wd, *, final_norm, tm_route, tm_rows, tm_group):
    R, D = x.shape
    n_exp = router_w.shape[1]
    h, meta, counts = moe_router(x, vec, router_w, tm=tm_route)
    counts = counts[0, :n_exp].astype(jnp.int32)
    padded = ((counts + tm_group - 1) // tm_group) * tm_group
    ends = jnp.cumsum(padded)
    offsets = ends - padded
    eid = meta[:, 0:TOP_K].astype(jnp.int32)
    rank = meta[:, 4:4 + TOP_K].astype(jnp.int32)
    dest = (offsets[eid] + rank).reshape(-1)
    n_tiles = (TOP_K * R) // tm_group + n_exp
    tile_start = jnp.arange(n_tiles, dtype=jnp.int32) * tm_group
    n_valid = ends[-1] // tm_group
    tile_expert = jnp.sum(tile_start[:, None] >= ends[None, :], axis=1).astype(jnp.int32)
    last_expert = jnp.take(tile_expert, jnp.maximum(n_valid - 1, 0))
    tile_expert = jnp.where(jnp.arange(n_tiles) < n_valid, tile_expert, last_expert)
    xs = moe_dispatch(h, dest, n_tiles * tm_group, tm=tm_rows)
    ys = moe_grouped_ffn(xs, tile_expert, n_valid.reshape(1).astype(jnp.int32), wg, wu, wd,
                         tm=tm_group)
    return moe_combine(x, meta, vec, ys, dest, tm=tm_rows, final_norm=final_norm)


def _final_norm_kernel(x_ref, g_ref, o_ref):
    x = x_ref[...]
    ms = jnp.mean(x * x, axis=-1, keepdims=True)
    o_ref[...] = x * lax.rsqrt(ms + EPS) * g_ref[0:1]


def final_norm(x, vec, *, tm):
    R, D = x.shape
    return pl.pallas_call(
        _final_norm_kernel,
        grid=(R // tm,),
        in_specs=[pl.BlockSpec((tm, D), lambda i: (i, 0)), pl.BlockSpec((SUBLANES, D), lambda i: (0, 0))],
        out_specs=pl.BlockSpec((tm, D), lambda i: (i, 0)),
        out_shape=jax.ShapeDtypeStruct((R, D), F32),
        compiler_params=_cparams("parallel"),
        name="final_norm",
    )(x, vec)


def _rope_tables(n_tokens):
    rows = n_tokens // GRID_W
    row = jnp.repeat(jnp.arange(rows, dtype=F32), GRID_W)
    col = jnp.tile(jnp.arange(GRID_W, dtype=F32), rows)
    inv = ROPE_THETA ** (-jnp.arange(ROPE_FREQS, dtype=F32) / ROPE_FREQS)
    ar = row[:, None] * inv
    ac = col[:, None] * inv
    ang = jnp.concatenate([ar, ar, ac, ac], axis=-1)
    cos, sin = jnp.cos(ang), jnp.sin(ang)
    lane = jnp.arange(HEAD_DIM) % (2 * ROPE_FREQS)
    lo = lane < ROPE_FREQS
    return cos, jnp.where(lo, -sin, 0.0), jnp.where(lo, 0.0, sin)


def _vec(*rows):
    d = rows[0].shape[-1]
    out = jnp.zeros((SUBLANES, d), F32)
    for i, r in enumerate(rows):
        out = out.at[i].set(r.reshape(d))
    return out


def _tile(n, pref):
    t = pref
    while n % t:
        t //= 2
    return t


def kernel(x, c, ctx, c_ctx, ada_w, ada_b, norm_attn_g, norm_ffn_g, w_in, b_gate, q_norm_g, k_norm_g,
           w_attn_proj, w_four_proj, w_out, ffn_w_gate, ffn_w_up, ffn_w_down, router_w, moe_w_gate,
           moe_w_up, moe_w_down, final_norm_g):
    B, S, D = x.shape
    assert B == 1 and c.shape[0] == 1 and ctx.shape[0] == 1
    depth = ada_w.shape[0]
    q_w = w_attn_proj.shape[1]
    four_w = w_four_proj.shape[1]
    kv_w = (w_in.shape[2] - q_w - four_w - 2 * D) // 2
    n_kv = kv_w // HEAD_DIM

    cond = jnp.zeros((SUBLANES, D), F32).at[0].set(c[0]).at[1].set(c_ctx)
    mods = ada_modulation(cond, ada_w, ada_b)
    rope = _rope_tables(S)
    xl = x[0]
    xc = ctx[0]
    C = xc.shape[0]

    def mod(l, r, k):
        return mods[l, r, k * D:(k + 1) * D]

    def project(l, xs_, r, rope_, tm):
        vec = _vec(norm_attn_g[l], mod(l, r, 0), mod(l, r, 1))
        qkg = _vec(q_norm_g[l], k_norm_g[l])
        return in_projection(xs_, vec, w_in[l].astype(BF16), b_gate[l].reshape(1, -1), qkg, rope_,
                             q_w=q_w, kv_w=kv_w, four_w=four_w, tm=tm)

    def finish_mixer(l, xs_, r, proj, extra, tm):
        qt, k, vt, f, gates = proj
        n = xs_.shape[0]
        tq, tk = _tile(n, 512), _tile(n, 512)
        if extra is None:
            attn = attention(qt, k, vt, None, n_kv=n_kv, tq=tq, tk=tk)
        else:
            bound = (SCORE_BOUND_SLACK * HEAD_DIM * ATTN_SCALE * LOG2E
                     * jnp.max(jnp.abs(q_norm_g[l])) * jnp.max(jnp.abs(k_norm_g[l]))).reshape(1)
            attn = lax.cond(
                bound[0] < MAX_SCORE_BOUND,
                lambda: attention_bounded(qt, k, vt, extra, bound, n_kv=n_kv, tq=tq, tk=tk),
                lambda: attention(qt, k, vt, extra, n_kv=n_kv, tq=tq, tk=tk))
        four = fourier_mix(f)
        return merge(xs_, attn, four, gates, _vec(mod(l, r, 2)), w_attn_proj[l].astype(BF16),
                     w_four_proj[l].astype(BF16), w_out[l].astype(BF16), tm=tm)

    def channel_mixer(l, xs_, r, fin, tm):
        i = l // 2
        vec = _vec(norm_ffn_g[l], mod(l, r, 3), mod(l, r, 4), mod(l, r, 5), final_norm_g)
        if l % 2 == 0:
            y = dense_ffn(xs_, vec, ffn_w_gate[i].astype(BF16), ffn_w_up[i].astype(BF16),
                          ffn_w_down[i].astype(BF16), tm=tm, tf=512)
            if fin:
                y = final_norm(y, _vec(final_norm_g), tm=tm)
            return y
        return moe_layer(xs_, vec, router_w[i], moe_w_gate[i].astype(BF16), moe_w_up[i].astype(BF16),
                         moe_w_down[i].astype(BF16), final_norm=fin,
                         tm_route=_tile(xs_.shape[0], 512), tm_rows=_tile(xs_.shape[0], 256),
                         tm_group=512)

    for l in range(depth):
        last = l == depth - 1
        tmc = _tile(C, 256)
        proj_c = project(l, xc, 1, None, tmc)
        if not last:
            xc_next = finish_mixer(l, xc, 1, proj_c, None, tmc)
            xc_next = channel_mixer(l, xc_next, 1, False, tmc)
        proj = project(l, xl, 0, rope, _tile(S, 256))
        xl = finish_mixer(l, xl, 0, proj, (proj_c[1], proj_c[2]), _tile(S, 256))
        xl = channel_mixer(l, xl, 0, last, _tile(S, 512))
        if not last:
            xc = xc_next
    return xl[None]
```

```python
import functools
import math

import numpy as np
import jax
import jax.numpy as jnp
from jax import lax
from jax.experimental import pallas as pl
from jax.experimental.pallas import tpu as pltpu

F32 = jnp.float32
BF16 = jnp.bfloat16

HEAD_DIM = 128
GRID_W = 64
ROPE_THETA = 10000.0
ROPE_FREQS = HEAD_DIM // 4
ATTN_SCALE = HEAD_DIM ** -0.5
N_FOURIER_GROUPS = 4
TOP_K = 2
EPS = 1e-6

LANES = 128
SUBLANES = 8
VMEM_LIMIT_BYTES = 58 * 1024 * 1024
NEG_INF = -1e30
LOG2E = math.log2(math.e)
BF16_SUBLANES = 16
VT_ROWS = HEAD_DIM + BF16_SUBLANES
SCORE_BOUND_SLACK = 1.02
ROW_DMA_UNROLL = 8
MAX_SCORE_BOUND = 60.0


def _cparams(*sem):
    return pltpu.CompilerParams(dimension_semantics=sem, vmem_limit_bytes=VMEM_LIMIT_BYTES)


def _norm_mod(x, g, shift, scale):
    ms = jnp.mean(x * x, axis=-1, keepdims=True)
    return (x * lax.rsqrt(ms + EPS) * g) * (1.0 + scale) + shift


def _silu(x):
    return x * jax.nn.sigmoid(x)


def _ada_kernel(cond_ref, w_ref, b_ref, o_ref):
    s = _silu(cond_ref[...]).astype(BF16)
    w = w_ref[0].astype(BF16)
    o_ref[0] = jnp.dot(s, w, preferred_element_type=F32) + b_ref[0]


def ada_modulation(cond, ada_w, ada_b, tn=1024):
    L, D, N = ada_w.shape
    return pl.pallas_call(
        _ada_kernel,
        grid=(L, N // tn),
        in_specs=[
            pl.BlockSpec((SUBLANES, D), lambda l, j: (0, 0)),
            pl.BlockSpec((1, D, tn), lambda l, j: (l, 0, j)),
            pl.BlockSpec((1, 1, tn), lambda l, j: (l, 0, j)),
        ],
        out_specs=pl.BlockSpec((1, SUBLANES, tn), lambda l, j: (l, 0, j)),
        out_shape=jax.ShapeDtypeStruct((L, SUBLANES, N), F32),
        compiler_params=_cparams("parallel", "parallel"),
        name="ada_modulation",
    )(cond, ada_w, ada_b.reshape(L, 1, N))


def _inproj_kernel(x_ref, vec_ref, w_ref, bg_ref, qkg_ref, cos_ref, sa_ref, sb_ref,
                   qt_ref, k_ref, vt_ref, f_ref, gates_ref, hs_ref, *, q_w, kv_w, four_w, tn, use_rope):
    hs_ref[...] = _norm_mod(x_ref[...], vec_ref[0:1], vec_ref[1:2], vec_ref[2:3]).astype(BF16)
    n_tiles = w_ref.shape[1] // tn
    gate0 = q_w + 2 * kv_w + four_w

    def matmul(j):
        return jnp.dot(hs_ref[...], w_ref[:, j * tn:(j + 1) * tn], preferred_element_type=F32)

    def head_norm(t, g, scale):
        ms = jnp.mean(t * t, axis=-1, keepdims=True)
        y = t * lax.rsqrt(ms + EPS) * g
        if use_rope:
            y = (y * cos_ref[...] + pltpu.roll(y, HEAD_DIM - ROPE_FREQS, 1) * sa_ref[...]
                 + pltpu.roll(y, ROPE_FREQS, 1) * sb_ref[...])
        return y * scale if scale != 1.0 else y

    def epilogue(j, acc):
        for c in range(0, tn, HEAD_DIM):
            col = j * tn + c
            t = acc[:, c:c + HEAD_DIM]
            if col < q_w:
                qt_ref[col:col + HEAD_DIM, :] = head_norm(t, qkg_ref[0:1], ATTN_SCALE * LOG2E).T.astype(BF16)
            elif col < q_w + kv_w:
                k_ref[:, col - q_w:col - q_w + HEAD_DIM] = head_norm(t, qkg_ref[1:2], 1.0).astype(BF16)
            elif col < q_w + 2 * kv_w:
                r0 = (col - q_w - kv_w) // HEAD_DIM * VT_ROWS
                vt_ref[r0:r0 + HEAD_DIM, :] = t.T.astype(BF16)
                vt_ref[r0 + HEAD_DIM:r0 + VT_ROWS, :] = jnp.ones((VT_ROWS - HEAD_DIM, vt_ref.shape[1]), BF16)
            elif col < gate0:
                f_ref[:, col - gate0 + four_w:col - gate0 + four_w + HEAD_DIM] = t
            else:
                g0 = col - gate0
                gates_ref[:, g0:g0 + HEAD_DIM] = jax.nn.sigmoid(t + bg_ref[:, g0:g0 + HEAD_DIM]).astype(BF16)

    acc = matmul(0)
    for j in range(n_tiles):
        nxt = matmul(j + 1) if j + 1 < n_tiles else None
        epilogue(j, acc)
        acc = nxt


def in_projection(x, vec, w_in, b_gate, qkg, rope, *, q_w, kv_w, four_w, tm, tn=1024):
    R, D = x.shape
    n_cols = w_in.shape[1]
    n_kv = kv_w // HEAD_DIM
    assert n_cols == q_w + 2 * kv_w + four_w + 2 * D and n_cols % tn == 0 and R % tm == 0 and tm % LANES == 0
    use_rope = rope is not None
    if rope is None:
        rope = (jnp.zeros((SUBLANES, HEAD_DIM), F32),) * 3
        rope_spec = pl.BlockSpec((SUBLANES, HEAD_DIM), lambda i: (0, 0))
    else:
        rope_spec = pl.BlockSpec((tm, HEAD_DIM), lambda i: (i, 0))
    kern = functools.partial(_inproj_kernel, q_w=q_w, kv_w=kv_w, four_w=four_w, tn=tn, use_rope=use_rope)
    return pl.pallas_call(
        kern,
        grid=(R // tm,),
        in_specs=[
            pl.BlockSpec((tm, D), lambda i: (i, 0)),
            pl.BlockSpec((SUBLANES, D), lambda i: (0, 0)),
            pl.BlockSpec((D, n_cols), lambda i: (0, 0), pipeline_mode=pl.Buffered(1)),
            pl.BlockSpec((1, 2 * D), lambda i: (0, 0)),
            pl.BlockSpec((SUBLANES, HEAD_DIM), lambda i: (0, 0)),
            rope_spec, rope_spec, rope_spec,
        ],
        out_specs=[
            pl.BlockSpec((q_w, tm), lambda i: (0, i)),
            pl.BlockSpec((tm, kv_w), lambda i: (i, 0)),
            pl.BlockSpec((n_kv * VT_ROWS, tm), lambda i: (0, i)),
            pl.BlockSpec((tm, four_w), lambda i: (i, 0)),
            pl.BlockSpec((tm, 2 * D), lambda i: (i, 0)),
        ],
        out_shape=[
            jax.ShapeDtypeStruct((q_w, R), BF16),
            jax.ShapeDtypeStruct((R, kv_w), BF16),
            jax.ShapeDtypeStruct((n_kv * VT_ROWS, R), BF16),
            jax.ShapeDtypeStruct((R, four_w), F32),
            jax.ShapeDtypeStruct((R, 2 * D), BF16),
        ],
        scratch_shapes=[pltpu.VMEM((tm, D), BF16)],
        compiler_params=_cparams("parallel"),
        name="in_projection",
    )(x, vec, w_in, b_gate, qkg, *rope)


def _attn_kernel(*refs, group, tk, n_chunks, has_extra):
    if has_extra:
        qt_ref, k_ref, vt_ref, kx_ref, vxt_ref, o_ref, m_ref, acc_ref, sa_ref, sb_ref, sx_ref = refs
    else:
        qt_ref, k_ref, vt_ref, o_ref, m_ref, acc_ref, sa_ref, sb_ref = refs
    m_ref[...] = jnp.full(m_ref.shape, NEG_INF, F32)
    acc_ref[...] = jnp.zeros(acc_ref.shape, F32)

    def scores(k, s_ref):
        for g in range(group):
            s_ref[g] = jnp.dot(k, qt_ref[g * HEAD_DIM:(g + 1) * HEAD_DIM, :], preferred_element_type=F32)

    def softmax_pv(vt, s_ref):
        for g in range(group):
            s = s_ref[g]
            m_prev = m_ref[g]
            m_new = jnp.maximum(m_prev, jnp.max(s, axis=0, keepdims=True))
            alpha = jnp.exp2(m_prev - m_new)
            p = jnp.exp2((s - m_new).astype(BF16))
            acc_ref[g] = alpha * acc_ref[g] + jnp.dot(vt, p, preferred_element_type=F32)
            m_ref[g] = m_new

    def k_chunk(c):
        return k_ref[pl.ds(pl.multiple_of(c * tk, tk), tk), :]

    def vt_chunk(c):
        return vt_ref[:, pl.ds(pl.multiple_of(c * tk, tk), tk)]

    n_pairs = (n_chunks - 1) // 2
    scores(k_chunk(0), sa_ref)

    def pair(i, carry):
        c = 2 * i
        scores(k_chunk(c + 1), sb_ref)
        softmax_pv(vt_chunk(c), sa_ref)
        scores(k_chunk(c + 2), sa_ref)
        softmax_pv(vt_chunk(c + 1), sb_ref)
        return carry

    lax.fori_loop(0, n_pairs, pair, 0)
    c = 2 * n_pairs
    if n_chunks - c == 2:
        scores(k_chunk(c + 1), sb_ref)
    softmax_pv(vt_chunk(c), sa_ref)
    if has_extra:
        scores(kx_ref[...], sx_ref)
    if n_chunks - c == 2:
        softmax_pv(vt_chunk(c + 1), sb_ref)
    if has_extra:
        softmax_pv(vxt_ref[...], sx_ref)
    for g in range(group):
        o = acc_ref[g, :HEAD_DIM, :] / acc_ref[g, HEAD_DIM:HEAD_DIM + 1, :]
        o_ref[:, g * HEAD_DIM:(g + 1) * HEAD_DIM] = o.T.astype(BF16)


def attention(qt, k, vt, extra, *, n_kv, tq, tk):
    q_w, R = qt.shape
    group = q_w // HEAD_DIM // n_kv
    gw = group * HEAD_DIM
    assert R % tq == 0 and R % tk == 0
    in_specs = [
        pl.BlockSpec((gw, tq), lambda h, i: (h, i)),
        pl.BlockSpec((R, HEAD_DIM), lambda h, i: (0, h)),
        pl.BlockSpec((VT_ROWS, R), lambda h, i: (h, 0)),
    ]
    args = [qt, k, vt]
    scratch = [
        pltpu.VMEM((group, 1, tq), F32),
        pltpu.VMEM((group, VT_ROWS, tq), F32),
        pltpu.VMEM((group, tk, tq), F32),
        pltpu.VMEM((group, tk, tq), F32),
    ]
    if extra is not None:
        rx = extra[0].shape[0]
        in_specs += [
            pl.BlockSpec((rx, HEAD_DIM), lambda h, i: (0, h)),
            pl.BlockSpec((VT_ROWS, rx), lambda h, i: (h, 0)),
        ]
        args += list(extra)
        scratch.append(pltpu.VMEM((group, rx, tq), F32))
    kern = functools.partial(_attn_kernel, group=group, tk=tk, n_chunks=R // tk, has_extra=extra is not None)
    return pl.pallas_call(
        kern,
        grid=(n_kv, R // tq),
        in_specs=in_specs,
        out_specs=pl.BlockSpec((tq, gw), lambda h, i: (i, h)),
        out_shape=jax.ShapeDtypeStruct((R, q_w), BF16),
        scratch_shapes=scratch,
        compiler_params=_cparams("parallel", "parallel"),
        name="attention",
    )(*args)


def _attn_bounded_kernel(bound_ref, qt_ref, k_ref, vt_ref, kx_ref, vxt_ref, o_ref, acc_ref, sa_ref, sb_ref, sx_ref,
                         *, group, tk, n_chunks):
    acc_ref[...] = jnp.zeros(acc_ref.shape, F32)
    bound = bound_ref[0]

    def scores(k, s_ref, g):
        s_ref[g] = jnp.dot(k, qt_ref[g * HEAD_DIM:(g + 1) * HEAD_DIM, :], preferred_element_type=F32)

    def exp_pv(vt, s_ref, g):
        p = jnp.exp2((s_ref[g] - bound).astype(BF16))
        acc_ref[g] += jnp.dot(vt, p, preferred_element_type=F32)

    def k_chunk(c):
        return k_ref[pl.ds(pl.multiple_of(c * tk, tk), tk), :]

    def vt_chunk(c):
        return vt_ref[:, pl.ds(pl.multiple_of(c * tk, tk), tk)]

    def step(k_next, s_next, vt_cur, s_cur):
        for g in range(group):
            if k_next is not None:
                scores(k_next, s_next, g)
            exp_pv(vt_cur, s_cur, g)

    n_pairs = (n_chunks - 1) // 2
    for g in range(group):
        scores(k_chunk(0), sa_ref, g)

    def pair(i, carry):
        c = 2 * i
        step(k_chunk(c + 1), sb_ref, vt_chunk(c), sa_ref)
        step(k_chunk(c + 2), sa_ref, vt_chunk(c + 1), sb_ref)
        return carry

    lax.fori_loop(0, n_pairs, pair, 0)
    c = 2 * n_pairs
    if n_chunks - c == 2:
        step(k_chunk(c + 1), sb_ref, vt_chunk(c), sa_ref)
        step(kx_ref[...], sx_ref, vt_chunk(c + 1), sb_ref)
    else:
        step(kx_ref[...], sx_ref, vt_chunk(c), sa_ref)
    step(None, None, vxt_ref[...], sx_ref)
    for g in range(group):
        o = acc_ref[g, :HEAD_DIM, :] / acc_ref[g, HEAD_DIM:HEAD_DIM + 1, :]
        o_ref[:, g * HEAD_DIM:(g + 1) * HEAD_DIM] = o.T.astype(BF16)


def attention_bounded(qt, k, vt, extra, bound, *, n_kv, tq, tk):
    q_w, R = qt.shape
    group = q_w // HEAD_DIM // n_kv
    gw = group * HEAD_DIM
    rx = extra[0].shape[0]
    assert R % tq == 0 and R % tk == 0
    kern = functools.partial(_attn_bounded_kernel, group=group, tk=tk, n_chunks=R // tk)
    return pl.pallas_call(
        kern,
        grid=(n_kv, R // tq),
        in_specs=[
            pl.BlockSpec(memory_space=pltpu.SMEM),
            pl.BlockSpec((gw, tq), lambda h, i: (h, i)),
            pl.BlockSpec((R, HEAD_DIM), lambda h, i: (0, h)),
            pl.BlockSpec((VT_ROWS, R), lambda h, i: (h, 0)),
            pl.BlockSpec((rx, HEAD_DIM), lambda h, i: (0, h)),
            pl.BlockSpec((VT_ROWS, rx), lambda h, i: (h, 0)),
        ],
        out_specs=pl.BlockSpec((tq, gw), lambda h, i: (i, h)),
        out_shape=jax.ShapeDtypeStruct((R, q_w), BF16),
        scratch_shapes=[pltpu.VMEM((group, VT_ROWS, tq), F32), pltpu.VMEM((group, tk, tq), F32),
                        pltpu.VMEM((group, tk, tq), F32), pltpu.VMEM((group, rx, tq), F32)],
        compiler_params=_cparams("parallel", "parallel"),
        name="attention_bounded",
    )(bound, qt, k, vt, *extra)


def _dft_mats(n, scale, repeat=1):
    k = np.arange(n)
    ang = 2.0 * np.pi * ((k[:, None] * k[None, :]) % n) / n
    eye = np.eye(repeat)
    return tuple(jnp.asarray(np.kron(m * scale, eye), BF16) for m in (np.cos(ang), np.sin(ang)))


def _four_small_kernel(u_ref, cn_ref, sn_ref, cc_ref, sc_ref, o_ref, *, gc):
    u = u_ref[...].astype(BF16)
    for g in range(u.shape[1] // gc):
        ug = u[:, g * gc:(g + 1) * gc]
        a = jnp.dot(ug, cc_ref[...], preferred_element_type=F32).astype(BF16)
        b = jnp.dot(ug, sc_ref[...], preferred_element_type=F32).astype(BF16)
        o = (jnp.dot(cn_ref[...], a, preferred_element_type=F32)
             - jnp.dot(sn_ref[...], b, preferred_element_type=F32))
        o_ref[:, g * gc:(g + 1) * gc] = o.astype(BF16)


def fourier_mix_small(u):
    n, c = u.shape
    gc = c // N_FOURIER_GROUPS
    cn, sn = _dft_mats(n, n ** -0.5)
    cc, sc = _dft_mats(gc, gc ** -0.5)
    full = lambda a: pl.BlockSpec(a.shape, lambda: (0,) * a.ndim)
    return pl.pallas_call(
        functools.partial(_four_small_kernel, gc=gc),
        in_specs=[full(u), full(cn), full(sn), full(cc), full(sc)],
        out_specs=pl.BlockSpec((n, c), lambda: (0, 0)),
        out_shape=jax.ShapeDtypeStruct((n, c), BF16),
        compiler_params=pltpu.CompilerParams(vmem_limit_bytes=VMEM_LIMIT_BYTES),
        name="fourier_small",
    )(u, cn, sn, cc, sc)


def _four_stage1_kernel(u_ref, ca_ref, sa_ref, cw_ref, sw_ref, yr_ref, yi_ref, *, tb):
    na, _, c = u_ref.shape
    u = u_ref[...].reshape(na * tb, c).astype(BF16)
    yr = jnp.dot(ca_ref[...], u, preferred_element_type=F32)
    yi = -jnp.dot(sa_ref[...], u, preferred_element_type=F32)
    cw = cw_ref[...].reshape(na * tb, LANES)
    sw = sw_ref[...].reshape(na * tb, LANES)
    for l in range(c // LANES):
        sl = slice(l * LANES, (l + 1) * LANES)
        r, i = yr[:, sl], yi[:, sl]
        yr_ref[:, :, sl] = (r * cw + i * sw).reshape(na, tb, LANES)
        yi_ref[:, :, sl] = (i * cw - r * sw).reshape(na, tb, LANES)


def _four_stage2_kernel(yr_ref, yi_ref, cb_ref, sb_ref, cc_ref, sc_ref, o_ref, *, nb, c, gc, tk1):
    for t in range(tk1):
        yr = yr_ref[t * nb:(t + 1) * nb, :].astype(BF16)
        yi = yi_ref[t * nb:(t + 1) * nb, :].astype(BF16)
        pr = (jnp.dot(cb_ref[...], yr, preferred_element_type=F32)
              + jnp.dot(sb_ref[...], yi, preferred_element_type=F32)).astype(BF16)
        pi = (jnp.dot(cb_ref[...], yi, preferred_element_type=F32)
              - jnp.dot(sb_ref[...], yr, preferred_element_type=F32)).astype(BF16)
        for g in range(c // gc):
            sl = slice(g * gc, (g + 1) * gc)
            o = (jnp.dot(pr[:, sl], cc_ref[...], preferred_element_type=F32)
                 + jnp.dot(pi[:, sl], sc_ref[...], preferred_element_type=F32))
            o_ref[:, t, g * gc:(g + 1) * gc] = o


def fourier_mix_large(u, *, na=128, tb=SUBLANES, tk1=SUBLANES):
    n, c = u.shape
    nb = n // na
    assert na * nb == n and nb % tb == 0 and na % tk1 == 0 and nb % SUBLANES == 0
    gc = c // N_FOURIER_GROUPS
    ca_k, sa_k = _dft_mats(na, na ** -0.5, repeat=tb)
    cb, sb = _dft_mats(nb, nb ** -0.5)
    cc, sc = _dft_mats(gc, gc ** -0.5)
    k1 = lax.broadcasted_iota(jnp.int32, (na, nb, LANES), 0)
    b = lax.broadcasted_iota(jnp.int32, (na, nb, LANES), 1)
    ang = ((k1 * b) % n).astype(F32) * (2.0 * math.pi / n)
    cw, sw = jnp.cos(ang), jnp.sin(ang)
    const = lambda a: pl.BlockSpec(a.shape, lambda i: (0,) * a.ndim)
    yr, yi = pl.pallas_call(
        functools.partial(_four_stage1_kernel, tb=tb),
        grid=(nb // tb,),
        in_specs=[
            pl.BlockSpec((na, tb, c), lambda i: (0, i, 0)),
            const(ca_k), const(sa_k),
            pl.BlockSpec((na, tb, LANES), lambda i: (0, i, 0)),
            pl.BlockSpec((na, tb, LANES), lambda i: (0, i, 0)),
        ],
        out_specs=[pl.BlockSpec((na, tb, c), lambda i: (0, i, 0))] * 2,
        out_shape=[jax.ShapeDtypeStruct((na, nb, c), F32)] * 2,
        compiler_params=_cparams("parallel"),
        name="fourier_stage1",
    )(u.reshape(na, nb, c), ca_k, sa_k, cw, sw)
    out = pl.pallas_call(
        functools.partial(_four_stage2_kernel, nb=nb, c=c, gc=gc, tk1=tk1),
        grid=(na // tk1,),
        in_specs=[
            pl.BlockSpec((tk1 * nb, c), lambda i: (i, 0)),
            pl.BlockSpec((tk1 * nb, c), lambda i: (i, 0)),
            const(cb), const(sb), const(cc), const(sc),
        ],
        out_specs=pl.BlockSpec((nb, tk1, c), lambda i: (0, i, 0)),
        out_shape=jax.ShapeDtypeStruct((nb, na, c), F32),
        compiler_params=_cparams("parallel"),
        name="fourier_stage2",
    )(yr.reshape(na * nb, c), yi.reshape(na * nb, c), cb, sb, cc, sc)
    return out.reshape(n, c)


def fourier_mix(u):
    n = u.shape[0]
    if n % (128 * SUBLANES) == 0 and n >= 128 * 16:
        return fourier_mix_large(u)
    return fourier_mix_small(u)


def _merge_kernel(x_ref, attn_ref, four_ref, gates_ref, vec_ref, wa_ref, wf_ref, wo_ref, o_ref):
    d = x_ref.shape[1]
    a = jnp.dot(attn_ref[...], wa_ref[...], preferred_element_type=F32)
    f = jnp.dot(four_ref[...].astype(BF16), wf_ref[...], preferred_element_type=F32)
    y = gates_ref[:, :d].astype(F32) * a + gates_ref[:, d:].astype(F32) * f
    y = jnp.dot(y.astype(BF16), wo_ref[...], preferred_element_type=F32)
    o_ref[...] = x_ref[...] + vec_ref[0:1] * y


def merge(x, attn, four, gates, vec, wa, wf, wo, *, tm):
    R, D = x.shape
    row = lambda w: pl.BlockSpec((tm, w), lambda i: (i, 0))
    const = lambda a: pl.BlockSpec(a.shape, lambda i: (0, 0), pipeline_mode=pl.Buffered(1))
    return pl.pallas_call(
        _merge_kernel,
        grid=(R // tm,),
        in_specs=[row(D), row(attn.shape[1]), row(four.shape[1]), row(gates.shape[1]),
                  pl.BlockSpec((SUBLANES, D), lambda i: (0, 0)), const(wa), const(wf), const(wo)],
        out_specs=row(D),
        out_shape=jax.ShapeDtypeStruct((R, D), F32),
        compiler_params=_cparams("parallel"),
        name="merge",
    )(x, attn, four, gates, vec, wa, wf, wo)


def _ffn_kernel(x_ref, vec_ref, wg_ref, wu_ref, wd_ref, o_ref, hs_ref, acc_ref):
    j = pl.program_id(1)

    @pl.when(j == 0)
    def _():
        hs_ref[...] = _norm_mod(x_ref[...], vec_ref[0:1], vec_ref[1:2], vec_ref[2:3]).astype(BF16)
        acc_ref[...] = jnp.zeros(acc_ref.shape, F32)

    h = hs_ref[...]
    g = jnp.dot(h, wg_ref[...], preferred_element_type=F32)
    u = jnp.dot(h, wu_ref[...], preferred_element_type=F32)
    acc_ref[...] += jnp.dot((_silu(g) * u).astype(BF16), wd_ref[...], preferred_element_type=F32)

    @pl.when(j == pl.num_programs(1) - 1)
    def _():
        o_ref[...] = x_ref[...] + vec_ref[3:4] * acc_ref[...]


def dense_ffn(x, vec, wg, wu, wd, *, tm, tf):
    R, D = x.shape
    F = wg.shape[1]
    assert R % tm == 0 and F % tf == 0
    return pl.pallas_call(
        _ffn_kernel,
        grid=(R // tm, F // tf),
        in_specs=[
            pl.BlockSpec((tm, D), lambda i, j: (i, 0)),
            pl.BlockSpec((SUBLANES, D), lambda i, j: (0, 0)),
            pl.BlockSpec((D, tf), lambda i, j: (0, j)),
            pl.BlockSpec((D, tf), lambda i, j: (0, j)),
            pl.BlockSpec((tf, D), lambda i, j: (j, 0)),
        ],
        out_specs=pl.BlockSpec((tm, D), lambda i, j: (i, 0)),
        out_shape=jax.ShapeDtypeStruct((R, D), F32),
        scratch_shapes=[pltpu.VMEM((tm, D), BF16), pltpu.VMEM((tm, D), F32)],
        compiler_params=_cparams("parallel", "arbitrary"),
        name="dense_ffn",
    )(x, vec, wg, wu, wd)


def _router_kernel(x_ref, vec_ref, rw_ref, h_ref, meta_ref, cnt_ref, run_ref, *, n_exp):
    i = pl.program_id(0)

    @pl.when(i == 0)
    def _():
        run_ref[...] = jnp.zeros(run_ref.shape, F32)

    h = _norm_mod(x_ref[...], vec_ref[0:1], vec_ref[1:2], vec_ref[2:3])
    h_ref[...] = h
    tm = h.shape[0]
    logits = [jnp.sum(h * rw_ref[e:e + 1], axis=-1, keepdims=True) for e in range(n_exp)]
    m1 = logits[0]
    for e in range(1, n_exp):
        m1 = jnp.maximum(m1, logits[e])
    i1 = jnp.full((tm, 1), n_exp, jnp.int32)
    for e in reversed(range(n_exp)):
        i1 = jnp.where(logits[e] == m1, e, i1)
    m2 = jnp.full((tm, 1), NEG_INF, F32)
    for e in range(n_exp):
        m2 = jnp.maximum(m2, jnp.where(i1 == e, NEG_INF, logits[e]))
    i2 = jnp.full((tm, 1), n_exp, jnp.int32)
    for e in reversed(range(n_exp)):
        i2 = jnp.where((logits[e] == m2) & (i1 != e), e, i2)
    e2 = jnp.exp(m2 - m1)
    w1 = 1.0 / (1.0 + e2)
    w2 = e2 / (1.0 + e2)

    lane = lax.broadcasted_iota(jnp.int32, (tm, LANES), 1)
    hit1 = lane == i1
    hit2 = lane == i2
    onehot = (hit1 | hit2).astype(F32)
    r = lax.broadcasted_iota(jnp.int32, (tm, tm), 0)
    cidx = lax.broadcasted_iota(jnp.int32, (tm, tm), 1)
    ltri = (cidx < r).astype(BF16)
    before = jnp.dot(ltri, onehot.astype(BF16), preferred_element_type=F32) + run_ref[0:1]
    rank1 = jnp.sum(jnp.where(hit1, before, 0.0), axis=-1, keepdims=True)
    rank2 = jnp.sum(jnp.where(hit2, before, 0.0), axis=-1, keepdims=True)
    run_ref[0:1] = run_ref[0:1] + jnp.sum(onehot, axis=0, keepdims=True)

    meta = jnp.where(lane == 0, i1.astype(F32), 0.0)
    meta = jnp.where(lane == 1, i2.astype(F32), meta)
    meta = jnp.where(lane == 2, w1, meta)
    meta = jnp.where(lane == 3, w2, meta)
    meta = jnp.where(lane == 4, rank1, meta)
    meta = jnp.where(lane == 5, rank2, meta)
    meta_ref[...] = meta
    cnt_ref[...] = run_ref[...]


def moe_router(x, vec, router_w, *, tm):
    R, D = x.shape
    n_exp = router_w.shape[1]
    rw = jnp.zeros((SUBLANES * ((n_exp + SUBLANES - 1) // SUBLANES), D), F32).at[:n_exp].set(router_w.T)
    return pl.pallas_call(
        functools.partial(_router_kernel, n_exp=n_exp),
        grid=(R // tm,),
        in_specs=[
            pl.BlockSpec((tm, D), lambda i: (i, 0)),
            pl.BlockSpec((SUBLANES, D), lambda i: (0, 0)),
            pl.BlockSpec(rw.shape, lambda i: (0, 0)),
        ],
        out_specs=[
            pl.BlockSpec((tm, D), lambda i: (i, 0)),
            pl.BlockSpec((tm, LANES), lambda i: (i, 0)),
            pl.BlockSpec((SUBLANES, LANES), lambda i: (0, 0)),
        ],
        out_shape=[
            jax.ShapeDtypeStruct((R, D), F32),
            jax.ShapeDtypeStruct((R, LANES), F32),
            jax.ShapeDtypeStruct((SUBLANES, LANES), F32),
        ],
        scratch_shapes=[pltpu.VMEM((SUBLANES, LANES), F32)],
        compiler_params=_cparams("arbitrary"),
        name="moe_router",
    )(x, vec, rw)


def _row_copy(src_ref, s, dst_ref, d, sem):
    return pltpu.make_async_copy(src_ref.at[pl.ds(s, 1)], dst_ref.at[pl.ds(d, 1)], sem)


def _dispatch_kernel(dest_ref, h_ref, xs_in_ref, xs_ref, sem):
    del xs_in_ref
    tm = h_ref.shape[0]
    base = pl.program_id(0) * tm

    def start(t, c):
        for s in range(TOP_K):
            _row_copy(h_ref, t, xs_ref, dest_ref[TOP_K * (base + t) + s], sem).start()
        return c

    def wait(t, c):
        for s in range(TOP_K):
            _row_copy(h_ref, t, xs_ref, dest_ref[TOP_K * (base + t) + s], sem).wait()
        return c

    lax.fori_loop(0, tm, start, 0, unroll=ROW_DMA_UNROLL)
    lax.fori_loop(0, tm, wait, 0, unroll=ROW_DMA_UNROLL)


def moe_dispatch(h, dest, n_rows, *, tm):
    R, D = h.shape
    return pl.pallas_call(
        _dispatch_kernel,
        grid_spec=pltpu.PrefetchScalarGridSpec(
            num_scalar_prefetch=1,
            grid=(R // tm,),
            in_specs=[pl.BlockSpec((tm, D), lambda i, dest: (i, 0)),
                      pl.BlockSpec(memory_space=pl.ANY)],
            out_specs=pl.BlockSpec(memory_space=pl.ANY),
            scratch_shapes=[pltpu.SemaphoreType.DMA(())],
        ),
        out_shape=jax.ShapeDtypeStruct((n_rows, D), F32),
        input_output_aliases={2: 0},
        compiler_params=_cparams("arbitrary"),
        name="moe_dispatch",
    )(dest, h, jnp.zeros((n_rows, D), F32))


def _gffn_up_kernel(te_ref, nv_ref, xs_ref, wg_ref, wu_ref, a_ref, *, tc):
    valid = pl.program_id(0) < nv_ref[0]

    @pl.when(valid)
    def _():
        h = xs_ref[...].astype(BF16)
        n_cols = a_ref.shape[1]
        for c0 in range(0, n_cols, tc):
            c1 = min(c0 + tc, n_cols)
            g = jnp.dot(h, wg_ref[0, :, c0:c1], preferred_element_type=F32)
            u = jnp.dot(h, wu_ref[0, :, c0:c1], preferred_element_type=F32)
            a_ref[:, c0:c1] = (_silu(g) * u).astype(BF16)

    @pl.when(jnp.logical_not(valid))
    def _():
        a_ref[...] = jnp.zeros(a_ref.shape, BF16)


def _gffn_down_kernel(te_ref, nv_ref, a_ref, wd_ref, ys_ref):
    valid = pl.program_id(0) < nv_ref[0]

    @pl.when(valid)
    def _():
        ys_ref[...] = jnp.dot(a_ref[...], wd_ref[0], preferred_element_type=F32)

    @pl.when(jnp.logical_not(valid))
    def _():
        ys_ref[...] = jnp.zeros(ys_ref.shape, F32)


def moe_grouped_ffn(xs, tile_expert, n_valid, wg, wu, wd, *, tm, tc=512):
    n_rows, D = xs.shape
    F = wg.shape[2]
    assert n_rows % tm == 0 and F % LANES == 0
    expert_w = lambda shape: pl.BlockSpec(shape, lambda t, te, nv: (te[t], 0, 0), pipeline_mode=pl.Buffered(1))
    a = pl.pallas_call(
        functools.partial(_gffn_up_kernel, tc=tc),
        grid_spec=pltpu.PrefetchScalarGridSpec(
            num_scalar_prefetch=2,
            grid=(n_rows // tm,),
            in_specs=[
                pl.BlockSpec((tm, D), lambda t, te, nv: (t, 0)),
                expert_w((1, D, F)),
                expert_w((1, D, F)),
            ],
            out_specs=pl.BlockSpec((tm, F), lambda t, te, nv: (t, 0)),
        ),
        out_shape=jax.ShapeDtypeStruct((n_rows, F), BF16),
        compiler_params=_cparams("parallel"),
        name="moe_ffn_up",
    )(tile_expert, n_valid, xs, wg, wu)
    return pl.pallas_call(
        _gffn_down_kernel,
        grid_spec=pltpu.PrefetchScalarGridSpec(
            num_scalar_prefetch=2,
            grid=(n_rows // tm,),
            in_specs=[
                pl.BlockSpec((tm, F), lambda t, te, nv: (t, 0)),
                pl.BlockSpec((1, F, D), lambda t, te, nv: (te[t], 0, 0)),
            ],
            out_specs=pl.BlockSpec((tm, D), lambda t, te, nv: (t, 0)),
        ),
        out_shape=jax.ShapeDtypeStruct((n_rows, D), F32),
        compiler_params=_cparams("parallel"),
        name="moe_ffn_down",
    )(tile_expert, n_valid, a, wd)


def _combine_kernel(dest_ref, x_ref, meta_ref, vec_ref, ys_ref, o_ref, b0_ref, b1_ref, sem, *, final_norm):
    tm = x_ref.shape[0]
    base = pl.program_id(0) * tm
    bufs = (b0_ref, b1_ref)

    def start(t, c):
        for s in range(TOP_K):
            _row_copy(ys_ref, dest_ref[TOP_K * (base + t) + s], bufs[s], t, sem).start()
        return c

    def wait(t, c):
        for s in range(TOP_K):
            _row_copy(ys_ref, dest_ref[TOP_K * (base + t) + s], bufs[s], t, sem).wait()
        return c

    lax.fori_loop(0, tm, start, 0, unroll=ROW_DMA_UNROLL)
    lax.fori_loop(0, tm, wait, 0, unroll=ROW_DMA_UNROLL)
    y = meta_ref[:, 2:3] * b0_ref[...] + meta_ref[:, 3:4] * b1_ref[...]
    out = x_ref[...] + vec_ref[3:4] * y
    if final_norm:
        ms = jnp.mean(out * out, axis=-1, keepdims=True)
        out = out * lax.rsqrt(ms + EPS) * vec_ref[4:5]
    o_ref[...] = out


def moe_combine(x, meta, vec, ys, dest, *, tm, final_norm):
    R, D = x.shape
    return pl.pallas_call(
        functools.partial(_combine_kernel, final_norm=final_norm),
        grid_spec=pltpu.PrefetchScalarGridSpec(
            num_scalar_prefetch=1,
            grid=(R // tm,),
            in_specs=[
                pl.BlockSpec((tm, D), lambda i, dest: (i, 0)),
                pl.BlockSpec((tm, LANES), lambda i, dest: (i, 0)),
                pl.BlockSpec((SUBLANES, D), lambda i, dest: (0, 0)),
                pl.BlockSpec(memory_space=pl.ANY),
            ],
            out_specs=pl.BlockSpec((tm, D), lambda i, dest: (i, 0)),
            scratch_shapes=[pltpu.VMEM((tm, D), F32), pltpu.VMEM((tm, D), F32),
                            pltpu.SemaphoreType.DMA(())],
        ),
        out_shape=jax.ShapeDtypeStruct((R, D), F32),
        compiler_params=_cparams("arbitrary"),
        name="moe_combine",
    )(dest, x, meta, vec, ys)


def moe_layer(x, vec, router_w, wg, wu, wd, *, final_norm, tm_route, tm_rows, tm_group):
    R, D = x.shape
    n_exp = router_w.shape[1]
    h, meta, counts = moe_router(x, vec, router_w, tm=tm_route)
    counts = counts[0, :n_exp].astype(jnp.int32)
    padded = ((counts + tm_group - 1) // tm_group) * tm_group
    ends = jnp.cumsum(padded)
    offsets = ends - padded
    eid = meta[:, 0:TOP_K].astype(jnp.int32)
    rank = meta[:, 4:4 + TOP_K].astype(jnp.int32)
    dest = (offsets[eid] + rank).reshape(-1)
    n_tiles = (TOP_K * R) // tm_group + n_exp
    tile_start = jnp.arange(n_tiles, dtype=jnp.int32) * tm_group
    n_valid = ends[-1] // tm_group
    tile_expert = jnp.sum(tile_start[:, None] >= ends[None, :], axis=1).astype(jnp.int32)
    last_expert = jnp.take(tile_expert, jnp.maximum(n_valid - 1, 0))
    tile_expert = jnp.where(jnp.arange(n_tiles) < n_valid, tile_expert, last_expert)
    xs = moe_dispatch(h, dest, n_tiles * tm_group, tm=tm_rows)
    ys = moe_grouped_ffn(xs, tile_expert, n_valid.reshape(1).astype(jnp.int32), wg, wu, wd,
                         tm=tm_group)
    return moe_combine(x, meta, vec, ys, dest, tm=tm_rows, final_norm=final_norm)


def _final_norm_kernel(x_ref, g_ref, o_ref):
    x = x_ref[...]
    ms = jnp.mean(x * x, axis=-1, keepdims=True)
    o_ref[...] = x * lax.rsqrt(ms + EPS) * g_ref[0:1]


def final_norm(x, vec, *, tm):
    R, D = x.shape
    return pl.pallas_call(
        _final_norm_kernel,
        grid=(R // tm,),
        in_specs=[pl.BlockSpec((tm, D), lambda i: (i, 0)), pl.BlockSpec((SUBLANES, D), lambda i: (0, 0))],
        out_specs=pl.BlockSpec((tm, D), lambda i: (i, 0)),
        out_shape=jax.ShapeDtypeStruct((R, D), F32),
        compiler_params=_cparams("parallel"),
        name="final_norm",
    )(x, vec)


def _rope_tables(n_tokens):
    rows = n_tokens // GRID_W
    row = jnp.repeat(jnp.arange(rows, dtype=F32), GRID_W)
    col = jnp.tile(jnp.arange(GRID_W, dtype=F32), rows)
    inv = ROPE_THETA ** (-jnp.arange(ROPE_FREQS, dtype=F32) / ROPE_FREQS)
    ar = row[:, None] * inv
    ac = col[:, None] * inv
    ang = jnp.concatenate([ar, ar, ac, ac], axis=-1)
    cos, sin = jnp.cos(ang), jnp.sin(ang)
    lane = jnp.arange(HEAD_DIM) % (2 * ROPE_FREQS)
    lo = lane < ROPE_FREQS
    return cos, jnp.where(lo, -sin, 0.0), jnp.where(lo, 0.0, sin)


def _vec(*rows):
    d = rows[0].shape[-1]
    out = jnp.zeros((SUBLANES, d), F32)
    for i, r in enumerate(rows):
        out = out.at[i].set(r.reshape(d))
    return out


def _tile(n, pref):
    t = pref
    while n % t:
        t //= 2
    return t


def kernel(x, c, ctx, c_ctx, ada_w, ada_b, norm_attn_g, norm_ffn_g, w_in, b_gate, q_norm_g, k_norm_g,
           w_attn_proj, w_four_proj, w_out, ffn_w_gate, ffn_w_up, ffn_w_down, router_w, moe_w_gate,
           moe_w_up, moe_w_down, final_norm_g):
    B, S, D = x.shape
    assert B == 1 and c.shape[0] == 1 and ctx.shape[0] == 1
    depth = ada_w.shape[0]
    q_w = w_attn_proj.shape[1]
    four_w = w_four_proj.shape[1]
    kv_w = (w_in.shape[2] - q_w - four_w - 2 * D) // 2
    n_kv = kv_w // HEAD_DIM

    cond = jnp.zeros((SUBLANES, D), F32).at[0].set(c[0]).at[1].set(c_ctx)
    mods = ada_modulation(cond, ada_w, ada_b)
    rope = _rope_tables(S)
    xl = x[0]
    xc = ctx[0]
    C = xc.shape[0]

    def mod(l, r, k):
        return mods[l, r, k * D:(k + 1) * D]

    def project(l, xs_, r, rope_, tm):
        vec = _vec(norm_attn_g[l], mod(l, r, 0), mod(l, r, 1))
        qkg = _vec(q_norm_g[l], k_norm_g[l])
        return in_projection(xs_, vec, w_in[l].astype(BF16), b_gate[l].reshape(1, -1), qkg, rope_,
                             q_w=q_w, kv_w=kv_w, four_w=four_w, tm=tm)

    def finish_mixer(l, xs_, r, proj, extra, tm):
        qt, k, vt, f, gates = proj
        n = xs_.shape[0]
        tq, tk = _tile(n, 512), _tile(n, 512)
        if extra is None:
            attn = attention(qt, k, vt, None, n_kv=n_kv, tq=tq, tk=tk)
        else:
            bound = (SCORE_BOUND_SLACK * HEAD_DIM * ATTN_SCALE * LOG2E
                     * jnp.max(jnp.abs(q_norm_g[l])) * jnp.max(jnp.abs(k_norm_g[l]))).reshape(1)
            attn = lax.cond(
                bound[0] < MAX_SCORE_BOUND,
                lambda: attention_bounded(qt, k, vt, extra, bound, n_kv=n_kv, tq=tq, tk=tk),
                lambda: attention(qt, k, vt, extra, n_kv=n_kv, tq=tq, tk=tk))
        four = fourier_mix(f)
        return merge(xs_, attn, four, gates, _vec(mod(l, r, 2)), w_attn_proj[l].astype(BF16),
                     w_four_proj[l].astype(BF16), w_out[l].astype(BF16), tm=tm)

    def channel_mixer(l, xs_, r, fin, tm):
        i = l // 2
        vec = _vec(norm_ffn_g[l], mod(l, r, 3), mod(l, r, 4), mod(l, r, 5), final_norm_g)
        if l % 2 == 0:
            y = dense_ffn(xs_, vec, ffn_w_gate[i].astype(BF16), ffn_w_up[i].astype(BF16),
                          ffn_w_down[i].astype(BF16), tm=tm, tf=512)
            if fin:
                y = final_norm(y, _vec(final_norm_g), tm=tm)
            return y
        return moe_layer(xs_, vec, router_w[i], moe_w_gate[i].astype(BF16), moe_w_up[i].astype(BF16),
                         moe_w_down[i].astype(BF16), final_norm=fin,
                         tm_route=_tile(xs_.shape[0], 512), tm_rows=_tile(xs_.shape[0], 256),
                         tm_group=512)

    for l in range(depth):
        last = l == depth - 1
        tmc = _tile(C, 256)
        proj_c = project(l, xc, 1, None, tmc)
        if not last:
            xc_next = finish_mixer(l, xc, 1, proj_c, None, tmc)
            xc_next = channel_mixer(l, xc_next, 1, False, tmc)
        proj = project(l, xl, 0, rope, _tile(S, 256))
        xl = finish_mixer(l, xl, 0, proj, (proj_c[1], proj_c[2]), _tile(S, 256))
        xl = channel_mixer(l, xl, 0, last, _tile(S, 512))
        if not last:
            xc = xc_next
    return xl[None]
```

```python
import functools
import math

import numpy as np
import jax
import jax.numpy as jnp
from jax import lax
from jax.experimental import pallas as pl
from jax.experimental.pallas import tpu as pltpu

F32 = jnp.float32
BF16 = jnp.bfloat16

HEAD_DIM = 128
GRID_W = 64
ROPE_THETA = 10000.0
ROPE_FREQS = HEAD_DIM // 4
ATTN_SCALE = HEAD_DIM ** -0.5
N_FOURIER_GROUPS = 4
TOP_K = 2
EPS = 1e-6

LANES = 128
SUBLANES = 8
VMEM_LIMIT_BYTES = 58 * 1024 * 1024
NEG_INF = -1e30
LOG2E = math.log2(math.e)
BF16_SUBLANES = 16
VT_ROWS = HEAD_DIM + BF16_SUBLANES
SCORE_BOUND_SLACK = 1.02
ROW_DMA_UNROLL = 8
MAX_SCORE_BOUND = 60.0


def _cparams(*sem):
    return pltpu.CompilerParams(dimension_semantics=sem, vmem_limit_bytes=VMEM_LIMIT_BYTES)


def _norm_mod(x, g, shift, scale):
    ms = jnp.mean(x * x, axis=-1, keepdims=True)
    return (x * lax.rsqrt(ms + EPS) * g) * (1.0 + scale) + shift


def _silu(x):
    return x * jax.nn.sigmoid(x)


def _ada_kernel(cond_ref, w_ref, b_ref, o_ref):
    s = _silu(cond_ref[...]).astype(BF16)
    w = w_ref[0].astype(BF16)
    o_ref[0] = jnp.dot(s, w, preferred_element_type=F32) + b_ref[0]


def ada_modulation(cond, ada_w, ada_b, tn=1024):
    L, D, N = ada_w.shape
    return pl.pallas_call(
        _ada_kernel,
        grid=(L, N // tn),
        in_specs=[
            pl.BlockSpec((SUBLANES, D), lambda l, j: (0, 0)),
            pl.BlockSpec((1, D, tn), lambda l, j: (l, 0, j)),
            pl.BlockSpec((1, 1, tn), lambda l, j: (l, 0, j)),
        ],
        out_specs=pl.BlockSpec((1, SUBLANES, tn), lambda l, j: (l, 0, j)),
        out_shape=jax.ShapeDtypeStruct((L, SUBLANES, N), F32),
        compiler_params=_cparams("parallel", "parallel"),
        name="ada_modulation",
    )(cond, ada_w, ada_b.reshape(L, 1, N))


def _inproj_kernel(x_ref, vec_ref, w_ref, bg_ref, qkg_ref, cos_ref, sa_ref, sb_ref,
                   qt_ref, k_ref, vt_ref, f_ref, gates_ref, hs_ref, *, q_w, kv_w, four_w, tn, use_rope):
    hs_ref[...] = _norm_mod(x_ref[...], vec_ref[0:1], vec_ref[1:2], vec_ref[2:3]).astype(BF16)
    n_tiles = w_ref.shape[1] // tn
    gate0 = q_w + 2 * kv_w + four_w

    def matmul(j):
        return jnp.dot(hs_ref[...], w_ref[:, j * tn:(j + 1) * tn], preferred_element_type=F32)

    def head_norm(t, g, scale):
        ms = jnp.mean(t * t, axis=-1, keepdims=True)
        y = t * lax.rsqrt(ms + EPS) * g
        if use_rope:
            y = (y * cos_ref[...] + pltpu.roll(y, HEAD_DIM - ROPE_FREQS, 1) * sa_ref[...]
                 + pltpu.roll(y, ROPE_FREQS, 1) * sb_ref[...])
        return y * scale if scale != 1.0 else y

    def epilogue(j, acc):
        for c in range(0, tn, HEAD_DIM):
            col = j * tn + c
            t = acc[:, c:c + HEAD_DIM]
            if col < q_w:
                qt_ref[col:col + HEAD_DIM, :] = head_norm(t, qkg_ref[0:1], ATTN_SCALE * LOG2E).T.astype(BF16)
            elif col < q_w + kv_w:
                k_ref[:, col - q_w:col - q_w + HEAD_DIM] = head_norm(t, qkg_ref[1:2], 1.0).astype(BF16)
            elif col < q_w + 2 * kv_w:
                r0 = (col - q_w - kv_w) // HEAD_DIM * VT_ROWS
                vt_ref[r0:r0 + HEAD_DIM, :] = t.T.astype(BF16)
                vt_ref[r0 + HEAD_DIM:r0 + VT_ROWS, :] = jnp.ones((VT_ROWS - HEAD_DIM, vt_ref.shape[1]), BF16)
            elif col < gate0:
                f_ref[:, col - gate0 + four_w:col - gate0 + four_w + HEAD_DIM] = t
            else:
                g0 = col - gate0
                gates_ref[:, g0:g0 + HEAD_DIM] = jax.nn.sigmoid(t + bg_ref[:, g0:g0 + HEAD_DIM]).astype(BF16)

    acc = matmul(0)
    for j in range(n_tiles):
        nxt = matmul(j + 1) if j + 1 < n_tiles else None
        epilogue(j, acc)
        acc = nxt


def in_projection(x, vec, w_in, layer, b_gate, qkg, rope, *, q_w, kv_w, four_w, tm, tn=1024):
    R, D = x.shape
    n_cols = w_in.shape[2]
    n_kv = kv_w // HEAD_DIM
    assert n_cols == q_w + 2 * kv_w + four_w + 2 * D and n_cols % tn == 0 and R % tm == 0 and tm % LANES == 0
    use_rope = rope is not None
    if rope is None:
        rope = (jnp.zeros((SUBLANES, HEAD_DIM), F32),) * 3
        rope_spec = pl.BlockSpec((SUBLANES, HEAD_DIM), lambda i: (0, 0))
    else:
        rope_spec = pl.BlockSpec((tm, HEAD_DIM), lambda i: (i, 0))
    kern = functools.partial(_inproj_kernel, q_w=q_w, kv_w=kv_w, four_w=four_w, tn=tn, use_rope=use_rope)
    return pl.pallas_call(
        kern,
        grid=(R // tm,),
        in_specs=[
            pl.BlockSpec((tm, D), lambda i: (i, 0)),
            pl.BlockSpec((SUBLANES, D), lambda i: (0, 0)),
            pl.BlockSpec((None, D, n_cols), lambda i: (layer, 0, 0), pipeline_mode=pl.Buffered(1)),
            pl.BlockSpec((1, 2 * D), lambda i: (0, 0)),
            pl.BlockSpec((SUBLANES, HEAD_DIM), lambda i: (0, 0)),
            rope_spec, rope_spec, rope_spec,
        ],
        out_specs=[
            pl.BlockSpec((q_w, tm), lambda i: (0, i)),
            pl.BlockSpec((tm, kv_w), lambda i: (i, 0)),
            pl.BlockSpec((n_kv * VT_ROWS, tm), lambda i: (0, i)),
            pl.BlockSpec((tm, four_w), lambda i: (i, 0)),
            pl.BlockSpec((tm, 2 * D), lambda i: (i, 0)),
        ],
        out_shape=[
            jax.ShapeDtypeStruct((q_w, R), BF16),
            jax.ShapeDtypeStruct((R, kv_w), BF16),
            jax.ShapeDtypeStruct((n_kv * VT_ROWS, R), BF16),
            jax.ShapeDtypeStruct((R, four_w), F32),
            jax.ShapeDtypeStruct((R, 2 * D), BF16),
        ],
        scratch_shapes=[pltpu.VMEM((tm, D), BF16)],
        compiler_params=_cparams("parallel"),
        name="in_projection",
    )(x, vec, w_in, b_gate, qkg, *rope)


def _attn_kernel(*refs, group, tk, n_chunks, has_extra):
    if has_extra:
        qt_ref, k_ref, vt_ref, kx_ref, vxt_ref, o_ref, m_ref, acc_ref, sa_ref, sb_ref, sx_ref = refs
    else:
        qt_ref, k_ref, vt_ref, o_ref, m_ref, acc_ref, sa_ref, sb_ref = refs
    m_ref[...] = jnp.full(m_ref.shape, NEG_INF, F32)
    acc_ref[...] = jnp.zeros(acc_ref.shape, F32)

    def scores(k, s_ref):
        for g in range(group):
            s_ref[g] = jnp.dot(k, qt_ref[g * HEAD_DIM:(g + 1) * HEAD_DIM, :], preferred_element_type=F32)

    def softmax_pv(vt, s_ref):
        for g in range(group):
            s = s_ref[g]
            m_prev = m_ref[g]
            m_new = jnp.maximum(m_prev, jnp.max(s, axis=0, keepdims=True))
            alpha = jnp.exp2(m_prev - m_new)
            p = jnp.exp2((s - m_new).astype(BF16))
            acc_ref[g] = alpha * acc_ref[g] + jnp.dot(vt, p, preferred_element_type=F32)
            m_ref[g] = m_new

    def k_chunk(c):
        return k_ref[pl.ds(pl.multiple_of(c * tk, tk), tk), :]

    def vt_chunk(c):
        return vt_ref[:, pl.ds(pl.multiple_of(c * tk, tk), tk)]

    n_pairs = (n_chunks - 1) // 2
    scores(k_chunk(0), sa_ref)

    def pair(i, carry):
        c = 2 * i
        scores(k_chunk(c + 1), sb_ref)
        softmax_pv(vt_chunk(c), sa_ref)
        scores(k_chunk(c + 2), sa_ref)
        softmax_pv(vt_chunk(c + 1), sb_ref)
        return carry

    lax.fori_loop(0, n_pairs, pair, 0)
    c = 2 * n_pairs
    if n_chunks - c == 2:
        scores(k_chunk(c + 1), sb_ref)
    softmax_pv(vt_chunk(c), sa_ref)
    if has_extra:
        scores(kx_ref[...], sx_ref)
    if n_chunks - c == 2:
        softmax_pv(vt_chunk(c + 1), sb_ref)
    if has_extra:
        softmax_pv(vxt_ref[...], sx_ref)
    for g in range(group):
        o = acc_ref[g, :HEAD_DIM, :] / acc_ref[g, HEAD_DIM:HEAD_DIM + 1, :]
        o_ref[:, g * HEAD_DIM:(g + 1) * HEAD_DIM] = o.T.astype(BF16)


def attention(qt, k, vt, extra, *, n_kv, tq, tk):
    q_w, R = qt.shape
    group = q_w // HEAD_DIM // n_kv
    gw = group * HEAD_DIM
    assert R % tq == 0 and R % tk == 0
    in_specs = [
        pl.BlockSpec((gw, tq), lambda h, i: (h, i)),
        pl.BlockSpec((R, HEAD_DIM), lambda h, i: (0, h)),
        pl.BlockSpec((VT_ROWS, R), lambda h, i: (h, 0)),
    ]
    args = [qt, k, vt]
    scratch = [
        pltpu.VMEM((group, 1, tq), F32),
        pltpu.VMEM((group, VT_ROWS, tq), F32),
        pltpu.VMEM((group, tk, tq), F32),
        pltpu.VMEM((group, tk, tq), F32),
    ]
    if extra is not None:
        rx = extra[0].shape[0]
        in_specs += [
            pl.BlockSpec((rx, HEAD_DIM), lambda h, i: (0, h)),
            pl.BlockSpec((VT_ROWS, rx), lambda h, i: (h, 0)),
        ]
        args += list(extra)
        scratch.append(pltpu.VMEM((group, rx, tq), F32))
    kern = functools.partial(_attn_kernel, group=group, tk=tk, n_chunks=R // tk, has_extra=extra is not None)
    return pl.pallas_call(
        kern,
        grid=(n_kv, R // tq),
        in_specs=in_specs,
        out_specs=pl.BlockSpec((tq, gw), lambda h, i: (i, h)),
        out_shape=jax.ShapeDtypeStruct((R, q_w), BF16),
        scratch_shapes=scratch,
        compiler_params=_cparams("parallel", "parallel"),
        name="attention",
    )(*args)


def _attn_bounded_kernel(bound_ref, qt_ref, k_ref, vt_ref, kx_ref, vxt_ref, o_ref, acc_ref, sa_ref, sb_ref, sx_ref,
                         *, group, tk, n_chunks):
    acc_ref[...] = jnp.zeros(acc_ref.shape, F32)
    bound = bound_ref[0]

    def scores(k, s_ref, g):
        s_ref[g] = jnp.dot(k, qt_ref[g * HEAD_DIM:(g + 1) * HEAD_DIM, :], preferred_element_type=F32)

    def exp_pv(vt, s_ref, g):
        p = jnp.exp2(s_ref[g] - bound)
        acc_ref[g, HEAD_DIM:HEAD_DIM + 1, :] += jnp.sum(p, axis=0, keepdims=True)
        acc_ref[g, :HEAD_DIM, :] += jnp.dot(vt[:HEAD_DIM], p.astype(BF16), preferred_element_type=F32)

    def k_chunk(c):
        return k_ref[pl.ds(pl.multiple_of(c * tk, tk), tk), :]

    def vt_chunk(c):
        return vt_ref[:, pl.ds(pl.multiple_of(c * tk, tk), tk)]

    def step(k_next, s_next, vt_cur, s_cur):
        for g in range(group):
            if k_next is not None:
                scores(k_next, s_next, g)
            exp_pv(vt_cur, s_cur, g)

    n_pairs = (n_chunks - 1) // 2
    for g in range(group):
        scores(k_chunk(0), sa_ref, g)

    def pair(i, carry):
        c = 2 * i
        step(k_chunk(c + 1), sb_ref, vt_chunk(c), sa_ref)
        step(k_chunk(c + 2), sa_ref, vt_chunk(c + 1), sb_ref)
        return carry

    lax.fori_loop(0, n_pairs, pair, 0)
    c = 2 * n_pairs
    if n_chunks - c == 2:
        step(k_chunk(c + 1), sb_ref, vt_chunk(c), sa_ref)
        step(kx_ref[...], sx_ref, vt_chunk(c + 1), sb_ref)
    else:
        step(kx_ref[...], sx_ref, vt_chunk(c), sa_ref)
    step(None, None, vxt_ref[...], sx_ref)
    for g in range(group):
        o = acc_ref[g, :HEAD_DIM, :] / acc_ref[g, HEAD_DIM:HEAD_DIM + 1, :]
        o_ref[:, g * HEAD_DIM:(g + 1) * HEAD_DIM] = o.T.astype(BF16)


def attention_bounded(qt, k, vt, extra, bound, *, n_kv, tq, tk):
    q_w, R = qt.shape
    group = q_w // HEAD_DIM // n_kv
    gw = group * HEAD_DIM
    rx = extra[0].shape[0]
    assert R % tq == 0 and R % tk == 0
    kern = functools.partial(_attn_bounded_kernel, group=group, tk=tk, n_chunks=R // tk)
    return pl.pallas_call(
        kern,
        grid=(n_kv, R // tq),
        in_specs=[
            pl.BlockSpec(memory_space=pltpu.SMEM),
            pl.BlockSpec((gw, tq), lambda h, i: (h, i)),
            pl.BlockSpec((R, HEAD_DIM), lambda h, i: (0, h)),
            pl.BlockSpec((VT_ROWS, R), lambda h, i: (h, 0)),
            pl.BlockSpec((rx, HEAD_DIM), lambda h, i: (0, h)),
            pl.BlockSpec((VT_ROWS, rx), lambda h, i: (h, 0)),
        ],
        out_specs=pl.BlockSpec((tq, gw), lambda h, i: (i, h)),
        out_shape=jax.ShapeDtypeStruct((R, q_w), BF16),
        scratch_shapes=[pltpu.VMEM((group, VT_ROWS, tq), F32), pltpu.VMEM((group, tk, tq), F32),
                        pltpu.VMEM((group, tk, tq), F32), pltpu.VMEM((group, rx, tq), F32)],
        compiler_params=_cparams("parallel", "parallel"),
        name="attention_bounded",
    )(bound, qt, k, vt, *extra)


def _dft_mats(n, scale, repeat=1):
    k = np.arange(n)
    ang = 2.0 * np.pi * ((k[:, None] * k[None, :]) % n) / n
    eye = np.eye(repeat)
    return tuple(jnp.asarray(np.kron(m * scale, eye), BF16) for m in (np.cos(ang), np.sin(ang)))


def _four_small_kernel(u_ref, cn_ref, sn_ref, cc_ref, sc_ref, o_ref, *, gc):
    u = u_ref[...].astype(BF16)
    for g in range(u.shape[1] // gc):
        ug = u[:, g * gc:(g + 1) * gc]
        a = jnp.dot(ug, cc_ref[...], preferred_element_type=F32).astype(BF16)
        b = jnp.dot(ug, sc_ref[...], preferred_element_type=F32).astype(BF16)
        o = (jnp.dot(cn_ref[...], a, preferred_element_type=F32)
             - jnp.dot(sn_ref[...], b, preferred_element_type=F32))
        o_ref[:, g * gc:(g + 1) * gc] = o.astype(BF16)


def fourier_mix_small(u):
    n, c = u.shape
    gc = c // N_FOURIER_GROUPS
    cn, sn = _dft_mats(n, n ** -0.5)
    cc, sc = _dft_mats(gc, gc ** -0.5)
    full = lambda a: pl.BlockSpec(a.shape, lambda: (0,) * a.ndim)
    return pl.pallas_call(
        functools.partial(_four_small_kernel, gc=gc),
        in_specs=[full(u), full(cn), full(sn), full(cc), full(sc)],
        out_specs=pl.BlockSpec((n, c), lambda: (0, 0)),
        out_shape=jax.ShapeDtypeStruct((n, c), BF16),
        compiler_params=pltpu.CompilerParams(vmem_limit_bytes=VMEM_LIMIT_BYTES),
        name="fourier_small",
    )(u, cn, sn, cc, sc)


def _four_stage1_kernel(u_ref, ca_ref, sa_ref, cw_ref, sw_ref, yr_ref, yi_ref, *, tb):
    na, _, c = u_ref.shape
    u = u_ref[...].reshape(na * tb, c).astype(BF16)
    yr = jnp.dot(ca_ref[...], u, preferred_element_type=F32)
    yi = -jnp.dot(sa_ref[...], u, preferred_element_type=F32)
    cw = cw_ref[...].reshape(na * tb, LANES)
    sw = sw_ref[...].reshape(na * tb, LANES)
    for l in range(c // LANES):
        sl = slice(l * LANES, (l + 1) * LANES)
        r, i = yr[:, sl], yi[:, sl]
        yr_ref[:, :, sl] = (r * cw + i * sw).reshape(na, tb, LANES)
        yi_ref[:, :, sl] = (i * cw - r * sw).reshape(na, tb, LANES)


def _four_stage2_kernel(yr_ref, yi_ref, cb_ref, sb_ref, cc_ref, sc_ref, o_ref, *, nb, c, gc, tk1):
    for t in range(tk1):
        yr = yr_ref[t * nb:(t + 1) * nb, :].astype(BF16)
        yi = yi_ref[t * nb:(t + 1) * nb, :].astype(BF16)
        pr = (jnp.dot(cb_ref[...], yr, preferred_element_type=F32)
              + jnp.dot(sb_ref[...], yi, preferred_element_type=F32)).astype(BF16)
        pi = (jnp.dot(cb_ref[...], yi, preferred_element_type=F32)
              - jnp.dot(sb_ref[...], yr, preferred_element_type=F32)).astype(BF16)
        for g in range(c // gc):
            sl = slice(g * gc, (g + 1) * gc)
            o = (jnp.dot(pr[:, sl], cc_ref[...], preferred_element_type=F32)
                 + jnp.dot(pi[:, sl], sc_ref[...], preferred_element_type=F32))
            o_ref[:, t, g * gc:(g + 1) * gc] = o


def fourier_mix_large(u, *, na=128, tb=SUBLANES, tk1=SUBLANES):
    n, c = u.shape
    nb = n // na
    assert na * nb == n and nb % tb == 0 and na % tk1 == 0 and nb % SUBLANES == 0
    gc = c // N_FOURIER_GROUPS
    ca_k, sa_k = _dft_mats(na, na ** -0.5, repeat=tb)
    cb, sb = _dft_mats(nb, nb ** -0.5)
    cc, sc = _dft_mats(gc, gc ** -0.5)
    k1 = lax.broadcasted_iota(jnp.int32, (na, nb, LANES), 0)
    b = lax.broadcasted_iota(jnp.int32, (na, nb, LANES), 1)
    ang = ((k1 * b) % n).astype(F32) * (2.0 * math.pi / n)
    cw, sw = jnp.cos(ang), jnp.sin(ang)
    const = lambda a: pl.BlockSpec(a.shape, lambda i: (0,) * a.ndim)
    yr, yi = pl.pallas_call(
        functools.partial(_four_stage1_kernel, tb=tb),
        grid=(nb // tb,),
        in_specs=[
            pl.BlockSpec((na, tb, c), lambda i: (0, i, 0)),
            const(ca_k), const(sa_k),
            pl.BlockSpec((na, tb, LANES), lambda i: (0, i, 0)),
            pl.BlockSpec((na, tb, LANES), lambda i: (0, i, 0)),
        ],
        out_specs=[pl.BlockSpec((na, tb, c), lambda i: (0, i, 0))] * 2,
        out_shape=[jax.ShapeDtypeStruct((na, nb, c), F32)] * 2,
        compiler_params=_cparams("parallel"),
        name="fourier_stage1",
    )(u.reshape(na, nb, c), ca_k, sa_k, cw, sw)
    out = pl.pallas_call(
        functools.partial(_four_stage2_kernel, nb=nb, c=c, gc=gc, tk1=tk1),
        grid=(na // tk1,),
        in_specs=[
            pl.BlockSpec((tk1 * nb, c), lambda i: (i, 0)),
            pl.BlockSpec((tk1 * nb, c), lambda i: (i, 0)),
            const(cb), const(sb), const(cc), const(sc),
        ],
        out_specs=pl.BlockSpec((nb, tk1, c), lambda i: (0, i, 0)),
        out_shape=jax.ShapeDtypeStruct((nb, na, c), F32),
        compiler_params=_cparams("parallel"),
        name="fourier_stage2",
    )(yr.reshape(na * nb, c), yi.reshape(na * nb, c), cb, sb, cc, sc)
    return out.reshape(n, c)


def fourier_mix(u):
    n = u.shape[0]
    if n % (128 * SUBLANES) == 0 and n >= 128 * 16:
        return fourier_mix_large(u)
    return fourier_mix_small(u)


def _merge_kernel(x_ref, attn_ref, four_ref, gates_ref, vec_ref, wa_ref, wf_ref, wo_ref, o_ref):
    d = x_ref.shape[1]
    a = jnp.dot(attn_ref[...], wa_ref[...], preferred_element_type=F32)
    f = jnp.dot(four_ref[...].astype(BF16), wf_ref[...], preferred_element_type=F32)
    y = gates_ref[:, :d].astype(F32) * a + gates_ref[:, d:].astype(F32) * f
    y = jnp.dot(y.astype(BF16), wo_ref[...], preferred_element_type=F32)
    o_ref[...] = x_ref[...] + vec_ref[0:1] * y


def merge(x, attn, four, gates, vec, wa, wf, wo, layer, *, tm):
    R, D = x.shape
    row = lambda w: pl.BlockSpec((tm, w), lambda i: (i, 0))
    const = lambda a: pl.BlockSpec((None,) + a.shape[1:], lambda i: (layer, 0, 0), pipeline_mode=pl.Buffered(1))
    return pl.pallas_call(
        _merge_kernel,
        grid=(R // tm,),
        in_specs=[row(D), row(attn.shape[1]), row(four.shape[1]), row(gates.shape[1]),
                  pl.BlockSpec((SUBLANES, D), lambda i: (0, 0)), const(wa), const(wf), const(wo)],
        out_specs=row(D),
        out_shape=jax.ShapeDtypeStruct((R, D), F32),
        compiler_params=_cparams("parallel"),
        name="merge",
    )(x, attn, four, gates, vec, wa, wf, wo)


def _ffn_kernel(x_ref, vec_ref, wg_ref, wu_ref, wd_ref, o_ref, hs_ref, acc_ref):
    j = pl.program_id(1)

    @pl.when(j == 0)
    def _():
        hs_ref[...] = _norm_mod(x_ref[...], vec_ref[0:1], vec_ref[1:2], vec_ref[2:3]).astype(BF16)
        acc_ref[...] = jnp.zeros(acc_ref.shape, F32)

    h = hs_ref[...]
    g = jnp.dot(h, wg_ref[...], preferred_element_type=F32)
    u = jnp.dot(h, wu_ref[...], preferred_element_type=F32)
    acc_ref[...] += jnp.dot((_silu(g) * u).astype(BF16), wd_ref[...], preferred_element_type=F32)

    @pl.when(j == pl.num_programs(1) - 1)
    def _():
        o_ref[...] = x_ref[...] + vec_ref[3:4] * acc_ref[...]


def dense_ffn(x, vec, wg, wu, wd, *, tm, tf):
    R, D = x.shape
    F = wg.shape[1]
    assert R % tm == 0 and F % tf == 0
    return pl.pallas_call(
        _ffn_kernel,
        grid=(R // tm, F // tf),
        in_specs=[
            pl.BlockSpec((tm, D), lambda i, j: (i, 0)),
            pl.BlockSpec((SUBLANES, D), lambda i, j: (0, 0)),
            pl.BlockSpec((D, tf), lambda i, j: (0, j)),
            pl.BlockSpec((D, tf), lambda i, j: (0, j)),
            pl.BlockSpec((tf, D), lambda i, j: (j, 0)),
        ],
        out_specs=pl.BlockSpec((tm, D), lambda i, j: (i, 0)),
        out_shape=jax.ShapeDtypeStruct((R, D), F32),
        scratch_shapes=[pltpu.VMEM((tm, D), BF16), pltpu.VMEM((tm, D), F32)],
        compiler_params=_cparams("parallel", "arbitrary"),
        name="dense_ffn",
    )(x, vec, wg, wu, wd)


def _router_kernel(x_ref, vec_ref, rw_ref, h_ref, meta_ref, cnt_ref, run_ref, *, n_exp):
    i = pl.program_id(0)

    @pl.when(i == 0)
    def _():
        run_ref[...] = jnp.zeros(run_ref.shape, F32)

    h = _norm_mod(x_ref[...], vec_ref[0:1], vec_ref[1:2], vec_ref[2:3])
    h_ref[...] = h
    tm = h.shape[0]
    logits = [jnp.sum(h * rw_ref[e:e + 1], axis=-1, keepdims=True) for e in range(n_exp)]
    m1 = logits[0]
    for e in range(1, n_exp):
        m1 = jnp.maximum(m1, logits[e])
    i1 = jnp.full((tm, 1), n_exp, jnp.int32)
    for e in reversed(range(n_exp)):
        i1 = jnp.where(logits[e] == m1, e, i1)
    m2 = jnp.full((tm, 1), NEG_INF, F32)
    for e in range(n_exp):
        m2 = jnp.maximum(m2, jnp.where(i1 == e, NEG_INF, logits[e]))
    i2 = jnp.full((tm, 1), n_exp, jnp.int32)
    for e in reversed(range(n_exp)):
        i2 = jnp.where((logits[e] == m2) & (i1 != e), e, i2)
    e2 = jnp.exp(m2 - m1)
    w1 = 1.0 / (1.0 + e2)
    w2 = e2 / (1.0 + e2)

    lane = lax.broadcasted_iota(jnp.int32, (tm, LANES), 1)
    hit1 = lane == i1
    hit2 = lane == i2
    onehot = (hit1 | hit2).astype(F32)
    r = lax.broadcasted_iota(jnp.int32, (tm, tm), 0)
    cidx = lax.broadcasted_iota(jnp.int32, (tm, tm), 1)
    ltri = (cidx < r).astype(BF16)
    before = jnp.dot(ltri, onehot.astype(BF16), preferred_element_type=F32) + run_ref[0:1]
    rank1 = jnp.sum(jnp.where(hit1, before, 0.0), axis=-1, keepdims=True)
    rank2 = jnp.sum(jnp.where(hit2, before, 0.0), axis=-1, keepdims=True)
    run_ref[0:1] = run_ref[0:1] + jnp.sum(onehot, axis=0, keepdims=True)

    meta = jnp.where(lane == 0, i1.astype(F32), 0.0)
    meta = jnp.where(lane == 1, i2.astype(F32), meta)
    meta = jnp.where(lane == 2, w1, meta)
    meta = jnp.where(lane == 3, w2, meta)
    meta = jnp.where(lane == 4, rank1, meta)
    meta = jnp.where(lane == 5, rank2, meta)
    meta_ref[...] = meta
    cnt_ref[...] = run_ref[...]


def moe_router(x, vec, router_w, *, tm):
    R, D = x.shape
    n_exp = router_w.shape[1]
    rw = jnp.zeros((SUBLANES * ((n_exp + SUBLANES - 1) // SUBLANES), D), F32).at[:n_exp].set(router_w.T)
    return pl.pallas_call(
        functools.partial(_router_kernel, n_exp=n_exp),
        grid=(R // tm,),
        in_specs=[
            pl.BlockSpec((tm, D), lambda i: (i, 0)),
            pl.BlockSpec((SUBLANES, D), lambda i: (0, 0)),
            pl.BlockSpec(rw.shape, lambda i: (0, 0)),
        ],
        out_specs=[
            pl.BlockSpec((tm, D), lambda i: (i, 0)),
            pl.BlockSpec((tm, LANES), lambda i: (i, 0)),
            pl.BlockSpec((SUBLANES, LANES), lambda i: (0, 0)),
        ],
        out_shape=[
            jax.ShapeDtypeStruct((R, D), F32),
            jax.ShapeDtypeStruct((R, LANES), F32),
            jax.ShapeDtypeStruct((SUBLANES, LANES), F32),
        ],
        scratch_shapes=[pltpu.VMEM((SUBLANES, LANES), F32)],
        compiler_params=_cparams("arbitrary"),
        name="moe_router",
    )(x, vec, rw)


def _row_copy(src_ref, s, dst_ref, d, sem):
    return pltpu.make_async_copy(src_ref.at[pl.ds(s, 1)], dst_ref.at[pl.ds(d, 1)], sem)


def _dispatch_kernel(dest_ref, h_ref, xs_in_ref, xs_ref, sem):
    del xs_in_ref
    tm = h_ref.shape[0]
    base = pl.program_id(0) * tm

    def start(t, c):
        for s in range(TOP_K):
            _row_copy(h_ref, t, xs_ref, dest_ref[TOP_K * (base + t) + s], sem).start()
        return c

    def wait(t, c):
        for s in range(TOP_K):
            _row_copy(h_ref, t, xs_ref, dest_ref[TOP_K * (base + t) + s], sem).wait()
        return c

    lax.fori_loop(0, tm, start, 0, unroll=ROW_DMA_UNROLL)
    lax.fori_loop(0, tm, wait, 0, unroll=ROW_DMA_UNROLL)


def moe_dispatch(h, dest, n_rows, *, tm):
    R, D = h.shape
    return pl.pallas_call(
        _dispatch_kernel,
        grid_spec=pltpu.PrefetchScalarGridSpec(
            num_scalar_prefetch=1,
            grid=(R // tm,),
            in_specs=[pl.BlockSpec((tm, D), lambda i, dest: (i, 0)),
                      pl.BlockSpec(memory_space=pl.ANY)],
            out_specs=pl.BlockSpec(memory_space=pl.ANY),
            scratch_shapes=[pltpu.SemaphoreType.DMA(())],
        ),
        out_shape=jax.ShapeDtypeStruct((n_rows, D), F32),
        input_output_aliases={2: 0},
        compiler_params=_cparams("arbitrary"),
        name="moe_dispatch",
    )(dest, h, jnp.zeros((n_rows, D), F32))


def _gffn_up_kernel(te_ref, nv_ref, xs_ref, wg_ref, wu_ref, a_ref, *, tc):
    valid = pl.program_id(0) < nv_ref[0]

    @pl.when(valid)
    def _():
        h = xs_ref[...].astype(BF16)
        n_cols = a_ref.shape[1]
        for c0 in range(0, n_cols, tc):
            c1 = min(c0 + tc, n_cols)
            g = jnp.dot(h, wg_ref[0, :, c0:c1], preferred_element_type=F32)
            u = jnp.dot(h, wu_ref[0, :, c0:c1], preferred_element_type=F32)
            a_ref[:, c0:c1] = (_silu(g) * u).astype(BF16)

    @pl.when(jnp.logical_not(valid))
    def _():
        a_ref[...] = jnp.zeros(a_ref.shape, BF16)


def _gffn_down_kernel(te_ref, nv_ref, a_ref, wd_ref, ys_ref):
    valid = pl.program_id(0) < nv_ref[0]

    @pl.when(valid)
    def _():
        ys_ref[...] = jnp.dot(a_ref[...], wd_ref[0], preferred_element_type=F32)

    @pl.when(jnp.logical_not(valid))
    def _():
        ys_ref[...] = jnp.zeros(ys_ref.shape, F32)


def moe_grouped_ffn(xs, tile_expert, n_valid, wg, wu, wd, *, tm, tc=512):
    n_rows, D = xs.shape
    F = wg.shape[2]
    assert n_rows % tm == 0 and F % LANES == 0
    expert_w = lambda shape: pl.BlockSpec(shape, lambda t, te, nv: (te[t], 0, 0), pipeline_mode=pl.Buffered(1))
    a = pl.pallas_call(
        functools.partial(_gffn_up_kernel, tc=tc),
        grid_spec=pltpu.PrefetchScalarGridSpec(
            num_scalar_prefetch=2,
            grid=(n_rows // tm,),
            in_specs=[
                pl.BlockSpec((tm, D), lambda t, te, nv: (t, 0)),
                expert_w((1, D, F)),
                expert_w((1, D, F)),
            ],
            out_specs=pl.BlockSpec((tm, F), lambda t, te, nv: (t, 0)),
        ),
        out_shape=jax.ShapeDtypeStruct((n_rows, F), BF16),
        compiler_params=_cparams("parallel"),
        name="moe_ffn_up",
    )(tile_expert, n_valid, xs, wg, wu)
    return pl.pallas_call(
        _gffn_down_kernel,
        grid_spec=pltpu.PrefetchScalarGridSpec(
            num_scalar_prefetch=2,
            grid=(n_rows // tm,),
            in_specs=[
                pl.BlockSpec((tm, F), lambda t, te, nv: (t, 0)),
                pl.BlockSpec((1, F, D), lambda t, te, nv: (te[t], 0, 0)),
            ],
            out_specs=pl.BlockSpec((tm, D), lambda t, te, nv: (t, 0)),
        ),
        out_shape=jax.ShapeDtypeStruct((n_rows, D), F32),
        compiler_params=_cparams("parallel"),
        name="moe_ffn_down",
    )(tile_expert, n_valid, a, wd)


def _combine_kernel(dest_ref, x_ref, meta_ref, vec_ref, ys_ref, o_ref, b0_ref, b1_ref, sem, *, final_norm):
    tm = x_ref.shape[0]
    base = pl.program_id(0) * tm
    bufs = (b0_ref, b1_ref)

    def start(t, c):
        for s in range(TOP_K):
            _row_copy(ys_ref, dest_ref[TOP_K * (base + t) + s], bufs[s], t, sem).start()
        return c

    def wait(t, c):
        for s in range(TOP_K):
            _row_copy(ys_ref, dest_ref[TOP_K * (base + t) + s], bufs[s], t, sem).wait()
        return c

    lax.fori_loop(0, tm, start, 0, unroll=ROW_DMA_UNROLL)
    lax.fori_loop(0, tm, wait, 0, unroll=ROW_DMA_UNROLL)
    y = meta_ref[:, 2:3] * b0_ref[...] + meta_ref[:, 3:4] * b1_ref[...]
    out = x_ref[...] + vec_ref[3:4] * y
    if final_norm:
        ms = jnp.mean(out * out, axis=-1, keepdims=True)
        out = out * lax.rsqrt(ms + EPS) * vec_ref[4:5]
    o_ref[...] = out


def moe_combine(x, meta, vec, ys, dest, *, tm, final_norm):
    R, D = x.shape
    return pl.pallas_call(
        functools.partial(_combine_kernel, final_norm=final_norm),
        grid_spec=pltpu.PrefetchScalarGridSpec(
            num_scalar_prefetch=1,
            grid=(R // tm,),
            in_specs=[
                pl.BlockSpec((tm, D), lambda i, dest: (i, 0)),
                pl.BlockSpec((tm, LANES), lambda i, dest: (i, 0)),
                pl.BlockSpec((SUBLANES, D), lambda i, dest: (0, 0)),
                pl.BlockSpec(memory_space=pl.ANY),
            ],
            out_specs=pl.BlockSpec((tm, D), lambda i, dest: (i, 0)),
            scratch_shapes=[pltpu.VMEM((tm, D), F32), pltpu.VMEM((tm, D), F32),
                            pltpu.SemaphoreType.DMA(())],
        ),
        out_shape=jax.ShapeDtypeStruct((R, D), F32),
        compiler_params=_cparams("arbitrary"),
        name="moe_combine",
    )(dest, x, meta, vec, ys)


def moe_layer(x, vec, router_w, wg, wu, wd, *, final_norm, tm_route, tm_rows, tm_group):
    R, D = x.shape
    n_exp = router_w.shape[1]
    h, meta, counts = moe_router(x, vec, router_w, tm=tm_route)
    counts = counts[0, :n_exp].astype(jnp.int32)
    padded = ((counts + tm_group - 1) // tm_group) * tm_group
    ends = jnp.cumsum(padded)
    offsets = ends - padded
    eid = meta[:, 0:TOP_K].astype(jnp.int32)
    rank = meta[:, 4:4 + TOP_K].astype(jnp.int32)
    dest = (offsets[eid] + rank).reshape(-1)
    n_tiles = (TOP_K * R) // tm_group + n_exp
    tile_start = jnp.arange(n_tiles, dtype=jnp.int32) * tm_group
    n_valid = ends[-1] // tm_group
    tile_expert = jnp.sum(tile_start[:, None] >= ends[None, :], axis=1).astype(jnp.int32)
    last_expert = jnp.take(tile_expert, jnp.maximum(n_valid - 1, 0))
    tile_expert = jnp.where(jnp.arange(n_tiles) < n_valid, tile_expert, last_expert)
    xs = moe_dispatch(h, dest, n_tiles * tm_group, tm=tm_rows)
    ys = moe_grouped_ffn(xs, tile_expert, n_valid.reshape(1).astype(jnp.int32), wg, wu, wd,
                         tm=tm_group)
    return moe_combine(x, meta, vec, ys, dest, tm=tm_rows, final_norm=final_norm)


def _final_norm_kernel(x_ref, g_ref, o_ref):
    x = x_ref[...]
    ms = jnp.mean(x * x, axis=-1, keepdims=True)
    o_ref[...] = x * lax.rsqrt(ms + EPS) * g_ref[0:1]


def final_norm(x, vec, *, tm):
    R, D = x.shape
    return pl.pallas_call(
        _final_norm_kernel,
        grid=(R // tm,),
        in_specs=[pl.BlockSpec((tm, D), lambda i: (i, 0)), pl.BlockSpec((SUBLANES, D), lambda i: (0, 0))],
        out_specs=pl.BlockSpec((tm, D), lambda i: (i, 0)),
        out_shape=jax.ShapeDtypeStruct((R, D), F32),
        compiler_params=_cparams("parallel"),
        name="final_norm",
    )(x, vec)


def _rope_tables(n_tokens):
    rows = n_tokens // GRID_W
    row = jnp.repeat(jnp.arange(rows, dtype=F32), GRID_W)
    col = jnp.tile(jnp.arange(GRID_W, dtype=F32), rows)
    inv = ROPE_THETA ** (-jnp.arange(ROPE_FREQS, dtype=F32) / ROPE_FREQS)
    ar = row[:, None] * inv
    ac = col[:, None] * inv
    ang = jnp.concatenate([ar, ar, ac, ac], axis=-1)
    cos, sin = jnp.cos(ang), jnp.sin(ang)
    lane = jnp.arange(HEAD_DIM) % (2 * ROPE_FREQS)
    lo = lane < ROPE_FREQS
    return cos, jnp.where(lo, -sin, 0.0), jnp.where(lo, 0.0, sin)


def _vec(*rows):
    d = rows[0].shape[-1]
    out = jnp.zeros((SUBLANES, d), F32)
    for i, r in enumerate(rows):
        out = out.at[i].set(r.reshape(d))
    return out


def _tile(n, pref):
    t = pref
    while n % t:
        t //= 2
    return t


def kernel(x, c, ctx, c_ctx, ada_w, ada_b, norm_attn_g, norm_ffn_g, w_in, b_gate, q_norm_g, k_norm_g,
           w_attn_proj, w_four_proj, w_out, ffn_w_gate, ffn_w_up, ffn_w_down, router_w, moe_w_gate,
           moe_w_up, moe_w_down, final_norm_g):
    B, S, D = x.shape
    assert B == 1 and c.shape[0] == 1 and ctx.shape[0] == 1
    depth = ada_w.shape[0]
    q_w = w_attn_proj.shape[1]
    four_w = w_four_proj.shape[1]
    kv_w = (w_in.shape[2] - q_w - four_w - 2 * D) // 2
    n_kv = kv_w // HEAD_DIM

    cond = jnp.zeros((SUBLANES, D), F32).at[0].set(c[0]).at[1].set(c_ctx)
    mods = ada_modulation(cond, ada_w, ada_b)
    rope = _rope_tables(S)
    w_in_bf, wa_bf, wf_bf, wo_bf = (w.astype(BF16) for w in (w_in, w_attn_proj, w_four_proj, w_out))
    xl = x[0]
    xc = ctx[0]
    C = xc.shape[0]

    def mod(l, r, k):
        return mods[l, r, k * D:(k + 1) * D]

    def project(l, xs_, r, rope_, tm):
        vec = _vec(norm_attn_g[l], mod(l, r, 0), mod(l, r, 1))
        qkg = _vec(q_norm_g[l], k_norm_g[l])
        return in_projection(xs_, vec, w_in_bf, l, b_gate[l].reshape(1, -1), qkg, rope_,
                             q_w=q_w, kv_w=kv_w, four_w=four_w, tm=tm)

    def finish_mixer(l, xs_, r, proj, extra, tm):
        qt, k, vt, f, gates = proj
        n = xs_.shape[0]
        tq, tk = _tile(n, 512), _tile(n, 512)
        if extra is None:
            attn = attention(qt, k, vt, None, n_kv=n_kv, tq=tq, tk=tk)
        else:
            bound = (SCORE_BOUND_SLACK * HEAD_DIM * ATTN_SCALE * LOG2E
                     * jnp.max(jnp.abs(q_norm_g[l])) * jnp.max(jnp.abs(k_norm_g[l]))).reshape(1)
            attn = lax.cond(
                bound[0] < MAX_SCORE_BOUND,
                lambda: attention_bounded(qt, k, vt, extra, bound, n_kv=n_kv, tq=tq, tk=tk),
                lambda: attention(qt, k, vt, extra, n_kv=n_kv, tq=tq, tk=tk))
        four = fourier_mix(f)
        return merge(xs_, attn, four, gates, _vec(mod(l, r, 2)), wa_bf, wf_bf, wo_bf, l, tm=tm)

    def channel_mixer(l, xs_, r, fin, tm):
        i = l // 2
        vec = _vec(norm_ffn_g[l], mod(l, r, 3), mod(l, r, 4), mod(l, r, 5), final_norm_g)
        if l % 2 == 0:
            y = dense_ffn(xs_, vec, ffn_w_gate[i].astype(BF16), ffn_w_up[i].astype(BF16),
                          ffn_w_down[i].astype(BF16), tm=tm, tf=512)
            if fin:
                y = final_norm(y, _vec(final_norm_g), tm=tm)
            return y
        return moe_layer(xs_, vec, router_w[i], moe_w_gate[i].astype(BF16), moe_w_up[i].astype(BF16),
                         moe_w_down[i].astype(BF16), final_norm=fin,
                         tm_route=_tile(xs_.shape[0], 512), tm_rows=_tile(xs_.shape[0], 256),
                         tm_group=512)

    for l in range(depth):
        last = l == depth - 1
        tmc = _tile(C, 256)
        proj_c = project(l, xc, 1, None, tmc)
        if not last:
            xc_next = finish_mixer(l, xc, 1, proj_c, None, tmc)
            xc_next = channel_mixer(l, xc_next, 1, False, tmc)
        proj = project(l, xl, 0, rope, _tile(S, 256))
        xl = finish_mixer(l, xl, 0, proj, (proj_c[1], proj_c[2]), _tile(S, 256))
        xl = channel_mixer(l, xl, 0, last, _tile(S, 512))
        if not last:
            xc = xc_next
    return xl[None]
```

```python
import functools
import math

import numpy as np
import jax
import jax.numpy as jnp
from jax import lax
from jax.experimental import pallas as pl
from jax.experimental.pallas import tpu as pltpu

F32 = jnp.float32
BF16 = jnp.bfloat16

HEAD_DIM = 128
GRID_W = 64
ROPE_THETA = 10000.0
ROPE_FREQS = HEAD_DIM // 4
ATTN_SCALE = HEAD_DIM ** -0.5
N_FOURIER_GROUPS = 4
TOP_K = 2
EPS = 1e-6

LANES = 128
SUBLANES = 8
VMEM_LIMIT_BYTES = 58 * 1024 * 1024
NEG_INF = -1e30
LOG2E = math.log2(math.e)
BF16_SUBLANES = 16
VT_ROWS = HEAD_DIM + BF16_SUBLANES
SCORE_BOUND_SLACK = 1.02
ROW_DMA_UNROLL = 8
MAX_SCORE_BOUND = 60.0


def _cparams(*sem):
    return pltpu.CompilerParams(dimension_semantics=sem, vmem_limit_bytes=VMEM_LIMIT_BYTES)


def _norm_mod(x, g, shift, scale):
    ms = jnp.mean(x * x, axis=-1, keepdims=True)
    return (x * lax.rsqrt(ms + EPS) * g) * (1.0 + scale) + shift


def _silu(x):
    return x * jax.nn.sigmoid(x)


def _ada_kernel(cond_ref, w_ref, b_ref, o_ref):
    s = _silu(cond_ref[...]).astype(BF16)
    w = w_ref[0].astype(BF16)
    o_ref[0] = jnp.dot(s, w, preferred_element_type=F32) + b_ref[0]


def ada_modulation(cond, ada_w, ada_b, tn=1024):
    L, D, N = ada_w.shape
    return pl.pallas_call(
        _ada_kernel,
        grid=(L, N // tn),
        in_specs=[
            pl.BlockSpec((SUBLANES, D), lambda l, j: (0, 0)),
            pl.BlockSpec((1, D, tn), lambda l, j: (l, 0, j)),
            pl.BlockSpec((1, 1, tn), lambda l, j: (l, 0, j)),
        ],
        out_specs=pl.BlockSpec((1, SUBLANES, tn), lambda l, j: (l, 0, j)),
        out_shape=jax.ShapeDtypeStruct((L, SUBLANES, N), F32),
        compiler_params=_cparams("parallel", "parallel"),
        name="ada_modulation",
    )(cond, ada_w, ada_b.reshape(L, 1, N))


def _inproj_kernel(x_ref, vec_ref, w_ref, bg_ref, qkg_ref, cos_ref, sa_ref, sb_ref,
                   qt_ref, k_ref, vt_ref, f_ref, gates_ref, hs_ref, *, q_w, kv_w, four_w, tn, use_rope):
    hs_ref[...] = _norm_mod(x_ref[...], vec_ref[0:1], vec_ref[1:2], vec_ref[2:3]).astype(BF16)
    n_tiles = w_ref.shape[1] // tn
    gate0 = q_w + 2 * kv_w + four_w

    def matmul(j):
        return jnp.dot(hs_ref[...], w_ref[:, j * tn:(j + 1) * tn], preferred_element_type=F32)

    def head_norm(t, g, scale):
        ms = jnp.mean(t * t, axis=-1, keepdims=True)
        y = t * lax.rsqrt(ms + EPS) * g
        if use_rope:
            y = (y * cos_ref[...] + pltpu.roll(y, HEAD_DIM - ROPE_FREQS, 1) * sa_ref[...]
                 + pltpu.roll(y, ROPE_FREQS, 1) * sb_ref[...])
        return y * scale if scale != 1.0 else y

    def epilogue(j, acc):
        for c in range(0, tn, HEAD_DIM):
            col = j * tn + c
            t = acc[:, c:c + HEAD_DIM]
            if col < q_w:
                qt_ref[col:col + HEAD_DIM, :] = head_norm(t, qkg_ref[0:1], ATTN_SCALE * LOG2E).T.astype(BF16)
            elif col < q_w + kv_w:
                k_ref[:, col - q_w:col - q_w + HEAD_DIM] = head_norm(t, qkg_ref[1:2], 1.0).astype(BF16)
            elif col < q_w + 2 * kv_w:
                r0 = (col - q_w - kv_w) // HEAD_DIM * VT_ROWS
                vt_ref[r0:r0 + HEAD_DIM, :] = t.T.astype(BF16)
                vt_ref[r0 + HEAD_DIM:r0 + VT_ROWS, :] = jnp.ones((VT_ROWS - HEAD_DIM, vt_ref.shape[1]), BF16)
            elif col < gate0:
                f_ref[:, col - gate0 + four_w:col - gate0 + four_w + HEAD_DIM] = t
            else:
                g0 = col - gate0
                gates_ref[:, g0:g0 + HEAD_DIM] = jax.nn.sigmoid(t + bg_ref[:, g0:g0 + HEAD_DIM]).astype(BF16)

    acc = matmul(0)
    for j in range(n_tiles):
        nxt = matmul(j + 1) if j + 1 < n_tiles else None
        epilogue(j, acc)
        acc = nxt


def in_projection(x, vec, w_in, layer, b_gate, qkg, rope, *, q_w, kv_w, four_w, tm, tn=1024):
    R, D = x.shape
    n_cols = w_in.shape[2]
    n_kv = kv_w // HEAD_DIM
    assert n_cols == q_w + 2 * kv_w + four_w + 2 * D and n_cols % tn == 0 and R % tm == 0 and tm % LANES == 0
    use_rope = rope is not None
    if rope is None:
        rope = (jnp.zeros((SUBLANES, HEAD_DIM), F32),) * 3
        rope_spec = pl.BlockSpec((SUBLANES, HEAD_DIM), lambda i: (0, 0))
    else:
        rope_spec = pl.BlockSpec((tm, HEAD_DIM), lambda i: (i, 0))
    kern = functools.partial(_inproj_kernel, q_w=q_w, kv_w=kv_w, four_w=four_w, tn=tn, use_rope=use_rope)
    return pl.pallas_call(
        kern,
        grid=(R // tm,),
        in_specs=[
            pl.BlockSpec((tm, D), lambda i: (i, 0)),
            pl.BlockSpec((SUBLANES, D), lambda i: (0, 0)),
            pl.BlockSpec((None, D, n_cols), lambda i: (layer, 0, 0), pipeline_mode=pl.Buffered(1)),
            pl.BlockSpec((1, 2 * D), lambda i: (0, 0)),
            pl.BlockSpec((SUBLANES, HEAD_DIM), lambda i: (0, 0)),
            rope_spec, rope_spec, rope_spec,
        ],
        out_specs=[
            pl.BlockSpec((q_w, tm), lambda i: (0, i)),
            pl.BlockSpec((tm, kv_w), lambda i: (i, 0)),
            pl.BlockSpec((n_kv * VT_ROWS, tm), lambda i: (0, i)),
            pl.BlockSpec((tm, four_w), lambda i: (i, 0)),
            pl.BlockSpec((tm, 2 * D), lambda i: (i, 0)),
        ],
        out_shape=[
            jax.ShapeDtypeStruct((q_w, R), BF16),
            jax.ShapeDtypeStruct((R, kv_w), BF16),
            jax.ShapeDtypeStruct((n_kv * VT_ROWS, R), BF16),
            jax.ShapeDtypeStruct((R, four_w), F32),
            jax.ShapeDtypeStruct((R, 2 * D), BF16),
        ],
        scratch_shapes=[pltpu.VMEM((tm, D), BF16)],
        compiler_params=_cparams("parallel"),
        name="in_projection",
    )(x, vec, w_in, b_gate, qkg, *rope)


def _attn_kernel(*refs, group, tk, n_chunks, has_extra):
    if has_extra:
        qt_ref, k_ref, vt_ref, kx_ref, vxt_ref, o_ref, m_ref, acc_ref, sa_ref, sb_ref, sx_ref = refs
    else:
        qt_ref, k_ref, vt_ref, o_ref, m_ref, acc_ref, sa_ref, sb_ref = refs
    m_ref[...] = jnp.full(m_ref.shape, NEG_INF, F32)
    acc_ref[...] = jnp.zeros(acc_ref.shape, F32)

    def scores(k, s_ref):
        for g in range(group):
            s_ref[g] = jnp.dot(k, qt_ref[g * HEAD_DIM:(g + 1) * HEAD_DIM, :], preferred_element_type=F32)

    def softmax_pv(vt, s_ref):
        for g in range(group):
            s = s_ref[g]
            m_prev = m_ref[g]
            m_new = jnp.maximum(m_prev, jnp.max(s, axis=0, keepdims=True))
            alpha = jnp.exp2(m_prev - m_new)
            p = jnp.exp2(s - m_new).astype(BF16)
            acc_ref[g] = alpha * acc_ref[g] + jnp.dot(vt, p, preferred_element_type=F32)
            m_ref[g] = m_new

    def k_chunk(c):
        return k_ref[pl.ds(pl.multiple_of(c * tk, tk), tk), :]

    def vt_chunk(c):
        return vt_ref[:, pl.ds(pl.multiple_of(c * tk, tk), tk)]

    n_pairs = (n_chunks - 1) // 2
    scores(k_chunk(0), sa_ref)

    def pair(i, carry):
        c = 2 * i
        scores(k_chunk(c + 1), sb_ref)
        softmax_pv(vt_chunk(c), sa_ref)
        scores(k_chunk(c + 2), sa_ref)
        softmax_pv(vt_chunk(c + 1), sb_ref)
        return carry

    lax.fori_loop(0, n_pairs, pair, 0)
    c = 2 * n_pairs
    if n_chunks - c == 2:
        scores(k_chunk(c + 1), sb_ref)
    softmax_pv(vt_chunk(c), sa_ref)
    if has_extra:
        scores(kx_ref[...], sx_ref)
    if n_chunks - c == 2:
        softmax_pv(vt_chunk(c + 1), sb_ref)
    if has_extra:
        softmax_pv(vxt_ref[...], sx_ref)
    for g in range(group):
        o = acc_ref[g, :HEAD_DIM, :] / acc_ref[g, HEAD_DIM:HEAD_DIM + 1, :]
        o_ref[:, g * HEAD_DIM:(g + 1) * HEAD_DIM] = o.T.astype(BF16)


def attention(qt, k, vt, extra, *, n_kv, tq, tk):
    q_w, R = qt.shape
    group = q_w // HEAD_DIM // n_kv
    gw = group * HEAD_DIM
    assert R % tq == 0 and R % tk == 0
    in_specs = [
        pl.BlockSpec((gw, tq), lambda h, i: (h, i)),
        pl.BlockSpec((R, HEAD_DIM), lambda h, i: (0, h)),
        pl.BlockSpec((VT_ROWS, R), lambda h, i: (h, 0)),
    ]
    args = [qt, k, vt]
    scratch = [
        pltpu.VMEM((group, 1, tq), F32),
        pltpu.VMEM((group, VT_ROWS, tq), F32),
        pltpu.VMEM((group, tk, tq), F32),
        pltpu.VMEM((group, tk, tq), F32),
    ]
    if extra is not None:
        rx = extra[0].shape[0]
        in_specs += [
            pl.BlockSpec((rx, HEAD_DIM), lambda h, i: (0, h)),
            pl.BlockSpec((VT_ROWS, rx), lambda h, i: (h, 0)),
        ]
        args += list(extra)
        scratch.append(pltpu.VMEM((group, rx, tq), F32))
    kern = functools.partial(_attn_kernel, group=group, tk=tk, n_chunks=R // tk, has_extra=extra is not None)
    return pl.pallas_call(
        kern,
        grid=(n_kv, R // tq),
        in_specs=in_specs,
        out_specs=pl.BlockSpec((tq, gw), lambda h, i: (i, h)),
        out_shape=jax.ShapeDtypeStruct((R, q_w), BF16),
        scratch_shapes=scratch,
        compiler_params=_cparams("parallel", "parallel"),
        name="attention",
    )(*args)


def _attn_bounded_kernel(bound_ref, qt_ref, k_ref, vt_ref, kx_ref, vxt_ref, o_ref, acc_ref, sa_ref, sb_ref, sx_ref,
                         *, group, tk, n_chunks):
    acc_ref[...] = jnp.zeros(acc_ref.shape, F32)
    bound = bound_ref[0]

    def scores(k, s_ref, g):
        s_ref[g] = jnp.dot(k, qt_ref[g * HEAD_DIM:(g + 1) * HEAD_DIM, :], preferred_element_type=F32)

    def exp_pv(vt, s_ref, g):
        p = jnp.exp2(s_ref[g] - bound).astype(BF16)
        acc_ref[g] += jnp.dot(vt, p, preferred_element_type=F32)

    def k_chunk(c):
        return k_ref[pl.ds(pl.multiple_of(c * tk, tk), tk), :]

    def vt_chunk(c):
        return vt_ref[:, pl.ds(pl.multiple_of(c * tk, tk), tk)]

    def step(k_next, s_next, vt_cur, s_cur):
        for g in range(group):
            if k_next is not None:
                scores(k_next, s_next, g)
            exp_pv(vt_cur, s_cur, g)

    n_pairs = (n_chunks - 1) // 2
    for g in range(group):
        scores(k_chunk(0), sa_ref, g)

    def pair(i, carry):
        c = 2 * i
        step(k_chunk(c + 1), sb_ref, vt_chunk(c), sa_ref)
        step(k_chunk(c + 2), sa_ref, vt_chunk(c + 1), sb_ref)
        return carry

    lax.fori_loop(0, n_pairs, pair, 0)
    c = 2 * n_pairs
    if n_chunks - c == 2:
        step(k_chunk(c + 1), sb_ref, vt_chunk(c), sa_ref)
        step(kx_ref[...], sx_ref, vt_chunk(c + 1), sb_ref)
    else:
        step(kx_ref[...], sx_ref, vt_chunk(c), sa_ref)
    step(None, None, vxt_ref[...], sx_ref)
    for g in range(group):
        o = acc_ref[g, :HEAD_DIM, :] / acc_ref[g, HEAD_DIM:HEAD_DIM + 1, :]
        o_ref[:, g * HEAD_DIM:(g + 1) * HEAD_DIM] = o.T.astype(BF16)


def attention_bounded(qt, k, vt, extra, bound, *, n_kv, tq, tk):
    q_w, R = qt.shape
    group = q_w // HEAD_DIM // n_kv
    gw = group * HEAD_DIM
    rx = extra[0].shape[0]
    assert R % tq == 0 and R % tk == 0
    kern = functools.partial(_attn_bounded_kernel, group=group, tk=tk, n_chunks=R // tk)
    return pl.pallas_call(
        kern,
        grid=(n_kv, R // tq),
        in_specs=[
            pl.BlockSpec(memory_space=pltpu.SMEM),
            pl.BlockSpec((gw, tq), lambda h, i: (h, i)),
            pl.BlockSpec((R, HEAD_DIM), lambda h, i: (0, h)),
            pl.BlockSpec((VT_ROWS, R), lambda h, i: (h, 0)),
            pl.BlockSpec((rx, HEAD_DIM), lambda h, i: (0, h)),
            pl.BlockSpec((VT_ROWS, rx), lambda h, i: (h, 0)),
        ],
        out_specs=pl.BlockSpec((tq, gw), lambda h, i: (i, h)),
        out_shape=jax.ShapeDtypeStruct((R, q_w), BF16),
        scratch_shapes=[pltpu.VMEM((group, VT_ROWS, tq), F32), pltpu.VMEM((group, tk, tq), F32),
                        pltpu.VMEM((group, tk, tq), F32), pltpu.VMEM((group, rx, tq), F32)],
        compiler_params=_cparams("parallel", "parallel"),
        name="attention_bounded",
    )(bound, qt, k, vt, *extra)


def _dft_mats(n, scale, repeat=1):
    k = np.arange(n)
    ang = 2.0 * np.pi * ((k[:, None] * k[None, :]) % n) / n
    eye = np.eye(repeat)
    return tuple(jnp.asarray(np.kron(m * scale, eye), BF16) for m in (np.cos(ang), np.sin(ang)))


def _four_small_kernel(u_ref, cn_ref, sn_ref, cc_ref, sc_ref, o_ref, *, gc):
    u = u_ref[...].astype(BF16)
    for g in range(u.shape[1] // gc):
        ug = u[:, g * gc:(g + 1) * gc]
        a = jnp.dot(ug, cc_ref[...], preferred_element_type=F32).astype(BF16)
        b = jnp.dot(ug, sc_ref[...], preferred_element_type=F32).astype(BF16)
        o = (jnp.dot(cn_ref[...], a, preferred_element_type=F32)
             - jnp.dot(sn_ref[...], b, preferred_element_type=F32))
        o_ref[:, g * gc:(g + 1) * gc] = o.astype(BF16)


def fourier_mix_small(u):
    n, c = u.shape
    gc = c // N_FOURIER_GROUPS
    cn, sn = _dft_mats(n, n ** -0.5)
    cc, sc = _dft_mats(gc, gc ** -0.5)
    full = lambda a: pl.BlockSpec(a.shape, lambda: (0,) * a.ndim)
    return pl.pallas_call(
        functools.partial(_four_small_kernel, gc=gc),
        in_specs=[full(u), full(cn), full(sn), full(cc), full(sc)],
        out_specs=pl.BlockSpec((n, c), lambda: (0, 0)),
        out_shape=jax.ShapeDtypeStruct((n, c), BF16),
        compiler_params=pltpu.CompilerParams(vmem_limit_bytes=VMEM_LIMIT_BYTES),
        name="fourier_small",
    )(u, cn, sn, cc, sc)


def _four_stage1_kernel(u_ref, ca_ref, sa_ref, cw_ref, sw_ref, yr_ref, yi_ref, *, tb):
    na, _, c = u_ref.shape
    u = u_ref[...].reshape(na * tb, c).astype(BF16)
    yr = jnp.dot(ca_ref[...], u, preferred_element_type=F32)
    yi = -jnp.dot(sa_ref[...], u, preferred_element_type=F32)
    cw = cw_ref[...].reshape(na * tb, LANES)
    sw = sw_ref[...].reshape(na * tb, LANES)
    for l in range(c // LANES):
        sl = slice(l * LANES, (l + 1) * LANES)
        r, i = yr[:, sl], yi[:, sl]
        yr_ref[:, :, sl] = (r * cw + i * sw).reshape(na, tb, LANES)
        yi_ref[:, :, sl] = (i * cw - r * sw).reshape(na, tb, LANES)


def _four_stage2_kernel(yr_ref, yi_ref, cb_ref, sb_ref, cc_ref, sc_ref, o_ref, *, nb, c, gc, tk1):
    for t in range(tk1):
        yr = yr_ref[t * nb:(t + 1) * nb, :].astype(BF16)
        yi = yi_ref[t * nb:(t + 1) * nb, :].astype(BF16)
        pr = (jnp.dot(cb_ref[...], yr, preferred_element_type=F32)
              + jnp.dot(sb_ref[...], yi, preferred_element_type=F32)).astype(BF16)
        pi = (jnp.dot(cb_ref[...], yi, preferred_element_type=F32)
              - jnp.dot(sb_ref[...], yr, preferred_element_type=F32)).astype(BF16)
        for g in range(c // gc):
            sl = slice(g * gc, (g + 1) * gc)
            o = (jnp.dot(pr[:, sl], cc_ref[...], preferred_element_type=F32)
                 + jnp.dot(pi[:, sl], sc_ref[...], preferred_element_type=F32))
            o_ref[:, t, g * gc:(g + 1) * gc] = o


def fourier_mix_large(u, *, na=128, tb=SUBLANES, tk1=SUBLANES):
    n, c = u.shape
    nb = n // na
    assert na * nb == n and nb % tb == 0 and na % tk1 == 0 and nb % SUBLANES == 0
    gc = c // N_FOURIER_GROUPS
    ca_k, sa_k = _dft_mats(na, na ** -0.5, repeat=tb)
    cb, sb = _dft_mats(nb, nb ** -0.5)
    cc, sc = _dft_mats(gc, gc ** -0.5)
    k1 = lax.broadcasted_iota(jnp.int32, (na, nb, LANES), 0)
    b = lax.broadcasted_iota(jnp.int32, (na, nb, LANES), 1)
    ang = ((k1 * b) % n).astype(F32) * (2.0 * math.pi / n)
    cw, sw = jnp.cos(ang), jnp.sin(ang)
    const = lambda a: pl.BlockSpec(a.shape, lambda i: (0,) * a.ndim)
    yr, yi = pl.pallas_call(
        functools.partial(_four_stage1_kernel, tb=tb),
        grid=(nb // tb,),
        in_specs=[
            pl.BlockSpec((na, tb, c), lambda i: (0, i, 0)),
            const(ca_k), const(sa_k),
            pl.BlockSpec((na, tb, LANES), lambda i: (0, i, 0)),
            pl.BlockSpec((na, tb, LANES), lambda i: (0, i, 0)),
        ],
        out_specs=[pl.BlockSpec((na, tb, c), lambda i: (0, i, 0))] * 2,
        out_shape=[jax.ShapeDtypeStruct((na, nb, c), F32)] * 2,
        compiler_params=_cparams("parallel"),
        name="fourier_stage1",
    )(u.reshape(na, nb, c), ca_k, sa_k, cw, sw)
    out = pl.pallas_call(
        functools.partial(_four_stage2_kernel, nb=nb, c=c, gc=gc, tk1=tk1),
        grid=(na // tk1,),
        in_specs=[
            pl.BlockSpec((tk1 * nb, c), lambda i: (i, 0)),
            pl.BlockSpec((tk1 * nb, c), lambda i: (i, 0)),
            const(cb), const(sb), const(cc), const(sc),
        ],
        out_specs=pl.BlockSpec((nb, tk1, c), lambda i: (0, i, 0)),
        out_shape=jax.ShapeDtypeStruct((nb, na, c), F32),
        compiler_params=_cparams("parallel"),
        name="fourier_stage2",
    )(yr.reshape(na * nb, c), yi.reshape(na * nb, c), cb, sb, cc, sc)
    return out.reshape(n, c)


def fourier_mix(u):
    n = u.shape[0]
    if n % (128 * SUBLANES) == 0 and n >= 128 * 16:
        return fourier_mix_large(u)
    return fourier_mix_small(u)


def _merge_kernel(x_ref, attn_ref, four_ref, gates_ref, vec_ref, wa_ref, wf_ref, wo_ref, o_ref):
    d = x_ref.shape[1]
    a = jnp.dot(attn_ref[...], wa_ref[...], preferred_element_type=F32)
    f = jnp.dot(four_ref[...].astype(BF16), wf_ref[...], preferred_element_type=F32)
    y = gates_ref[:, :d].astype(F32) * a + gates_ref[:, d:].astype(F32) * f
    y = jnp.dot(y.astype(BF16), wo_ref[...], preferred_element_type=F32)
    o_ref[...] = x_ref[...] + vec_ref[0:1] * y


def merge(x, attn, four, gates, vec, wa, wf, wo, layer, *, tm):
    R, D = x.shape
    row = lambda w: pl.BlockSpec((tm, w), lambda i: (i, 0))
    const = lambda a: pl.BlockSpec((None,) + a.shape[1:], lambda i: (layer, 0, 0), pipeline_mode=pl.Buffered(1))
    return pl.pallas_call(
        _merge_kernel,
        grid=(R // tm,),
        in_specs=[row(D), row(attn.shape[1]), row(four.shape[1]), row(gates.shape[1]),
                  pl.BlockSpec((SUBLANES, D), lambda i: (0, 0)), const(wa), const(wf), const(wo)],
        out_specs=row(D),
        out_shape=jax.ShapeDtypeStruct((R, D), F32),
        compiler_params=_cparams("parallel"),
        name="merge",
    )(x, attn, four, gates, vec, wa, wf, wo)


def _ffn_kernel(x_ref, vec_ref, wg_ref, wu_ref, wd_ref, o_ref, hs_ref, acc_ref):
    j = pl.program_id(1)

    @pl.when(j == 0)
    def _():
        hs_ref[...] = _norm_mod(x_ref[...], vec_ref[0:1], vec_ref[1:2], vec_ref[2:3]).astype(BF16)
        acc_ref[...] = jnp.zeros(acc_ref.shape, F32)

    h = hs_ref[...]
    g = jnp.dot(h, wg_ref[...], preferred_element_type=F32)
    u = jnp.dot(h, wu_ref[...], preferred_element_type=F32)
    acc_ref[...] += jnp.dot((_silu(g) * u).astype(BF16), wd_ref[...], preferred_element_type=F32)

    @pl.when(j == pl.num_programs(1) - 1)
    def _():
        o_ref[...] = x_ref[...] + vec_ref[3:4] * acc_ref[...]


def dense_ffn(x, vec, wg, wu, wd, *, tm, tf):
    R, D = x.shape
    F = wg.shape[1]
    assert R % tm == 0 and F % tf == 0
    return pl.pallas_call(
        _ffn_kernel,
        grid=(R // tm, F // tf),
        in_specs=[
            pl.BlockSpec((tm, D), lambda i, j: (i, 0)),
            pl.BlockSpec((SUBLANES, D), lambda i, j: (0, 0)),
            pl.BlockSpec((D, tf), lambda i, j: (0, j)),
            pl.BlockSpec((D, tf), lambda i, j: (0, j)),
            pl.BlockSpec((tf, D), lambda i, j: (j, 0)),
        ],
        out_specs=pl.BlockSpec((tm, D), lambda i, j: (i, 0)),
        out_shape=jax.ShapeDtypeStruct((R, D), F32),
        scratch_shapes=[pltpu.VMEM((tm, D), BF16), pltpu.VMEM((tm, D), F32)],
        compiler_params=_cparams("parallel", "arbitrary"),
        name="dense_ffn",
    )(x, vec, wg, wu, wd)


def _router_kernel(x_ref, vec_ref, rw_ref, h_ref, meta_ref, cnt_ref, run_ref, *, n_exp):
    i = pl.program_id(0)

    @pl.when(i == 0)
    def _():
        run_ref[...] = jnp.zeros(run_ref.shape, F32)

    h = _norm_mod(x_ref[...], vec_ref[0:1], vec_ref[1:2], vec_ref[2:3])
    h_ref[...] = h
    tm = h.shape[0]
    logits = [jnp.sum(h * rw_ref[e:e + 1], axis=-1, keepdims=True) for e in range(n_exp)]
    m1 = logits[0]
    for e in range(1, n_exp):
        m1 = jnp.maximum(m1, logits[e])
    i1 = jnp.full((tm, 1), n_exp, jnp.int32)
    for e in reversed(range(n_exp)):
        i1 = jnp.where(logits[e] == m1, e, i1)
    m2 = jnp.full((tm, 1), NEG_INF, F32)
    for e in range(n_exp):
        m2 = jnp.maximum(m2, jnp.where(i1 == e, NEG_INF, logits[e]))
    i2 = jnp.full((tm, 1), n_exp, jnp.int32)
    for e in reversed(range(n_exp)):
        i2 = jnp.where((logits[e] == m2) & (i1 != e), e, i2)
    e2 = jnp.exp(m2 - m1)
    w1 = 1.0 / (1.0 + e2)
    w2 = e2 / (1.0 + e2)

    lane = lax.broadcasted_iota(jnp.int32, (tm, LANES), 1)
    hit1 = lane == i1
    hit2 = lane == i2
    onehot = (hit1 | hit2).astype(F32)
    r = lax.broadcasted_iota(jnp.int32, (tm, tm), 0)
    cidx = lax.broadcasted_iota(jnp.int32, (tm, tm), 1)
    ltri = (cidx < r).astype(BF16)
    before = jnp.dot(ltri, onehot.astype(BF16), preferred_element_type=F32) + run_ref[0:1]
    rank1 = jnp.sum(jnp.where(hit1, before, 0.0), axis=-1, keepdims=True)
    rank2 = jnp.sum(jnp.where(hit2, before, 0.0), axis=-1, keepdims=True)
    run_ref[0:1] = run_ref[0:1] + jnp.sum(onehot, axis=0, keepdims=True)

    meta = jnp.where(lane == 0, i1.astype(F32), 0.0)
    meta = jnp.where(lane == 1, i2.astype(F32), meta)
    meta = jnp.where(lane == 2, w1, meta)
    meta = jnp.where(lane == 3, w2, meta)
    meta = jnp.where(lane == 4, rank1, meta)
    meta = jnp.where(lane == 5, rank2, meta)
    meta_ref[...] = meta
    cnt_ref[...] = run_ref[...]


def moe_router(x, vec, router_w, *, tm):
    R, D = x.shape
    n_exp = router_w.shape[1]
    rw = jnp.zeros((SUBLANES * ((n_exp + SUBLANES - 1) // SUBLANES), D), F32).at[:n_exp].set(router_w.T)
    return pl.pallas_call(
        functools.partial(_router_kernel, n_exp=n_exp),
        grid=(R // tm,),
        in_specs=[
            pl.BlockSpec((tm, D), lambda i: (i, 0)),
            pl.BlockSpec((SUBLANES, D), lambda i: (0, 0)),
            pl.BlockSpec(rw.shape, lambda i: (0, 0)),
        ],
        out_specs=[
            pl.BlockSpec((tm, D), lambda i: (i, 0)),
            pl.BlockSpec((tm, LANES), lambda i: (i, 0)),
            pl.BlockSpec((SUBLANES, LANES), lambda i: (0, 0)),
        ],
        out_shape=[
            jax.ShapeDtypeStruct((R, D), F32),
            jax.ShapeDtypeStruct((R, LANES), F32),
            jax.ShapeDtypeStruct((SUBLANES, LANES), F32),
        ],
        scratch_shapes=[pltpu.VMEM((SUBLANES, LANES), F32)],
        compiler_params=_cparams("arbitrary"),
        name="moe_router",
    )(x, vec, rw)


def _row_copy(src_ref, s, dst_ref, d, sem):
    return pltpu.make_async_copy(src_ref.at[pl.ds(s, 1)], dst_ref.at[pl.ds(d, 1)], sem)


def _dispatch_kernel(dest_ref, h_ref, xs_in_ref, xs_ref, sem):
    del xs_in_ref
    tm = h_ref.shape[0]
    base = pl.program_id(0) * tm

    def start(t, c):
        for s in range(TOP_K):
            _row_copy(h_ref, t, xs_ref, dest_ref[TOP_K * (base + t) + s], sem).start()
        return c

    def wait(t, c):
        for s in range(TOP_K):
            _row_copy(h_ref, t, xs_ref, dest_ref[TOP_K * (base + t) + s], sem).wait()
        return c

    lax.fori_loop(0, tm, start, 0, unroll=ROW_DMA_UNROLL)
    lax.fori_loop(0, tm, wait, 0, unroll=ROW_DMA_UNROLL)


def moe_dispatch(h, dest, n_rows, *, tm):
    R, D = h.shape
    return pl.pallas_call(
        _dispatch_kernel,
        grid_spec=pltpu.PrefetchScalarGridSpec(
            num_scalar_prefetch=1,
            grid=(R // tm,),
            in_specs=[pl.BlockSpec((tm, D), lambda i, dest: (i, 0)),
                      pl.BlockSpec(memory_space=pl.ANY)],
            out_specs=pl.BlockSpec(memory_space=pl.ANY),
            scratch_shapes=[pltpu.SemaphoreType.DMA(())],
        ),
        out_shape=jax.ShapeDtypeStruct((n_rows, D), F32),
        input_output_aliases={2: 0},
        compiler_params=_cparams("arbitrary"),
        name="moe_dispatch",
    )(dest, h, jnp.zeros((n_rows, D), F32))


def _gffn_up_kernel(te_ref, nv_ref, xs_ref, wg_ref, wu_ref, a_ref, *, tc):
    valid = pl.program_id(0) < nv_ref[0]

    @pl.when(valid)
    def _():
        h = xs_ref[...].astype(BF16)
        n_cols = a_ref.shape[1]
        for c0 in range(0, n_cols, tc):
            c1 = min(c0 + tc, n_cols)
            g = jnp.dot(h, wg_ref[0, :, c0:c1], preferred_element_type=F32)
            u = jnp.dot(h, wu_ref[0, :, c0:c1], preferred_element_type=F32)
            a_ref[:, c0:c1] = (_silu(g) * u).astype(BF16)

    @pl.when(jnp.logical_not(valid))
    def _():
        a_ref[...] = jnp.zeros(a_ref.shape, BF16)


def _gffn_down_kernel(te_ref, nv_ref, a_ref, wd_ref, ys_ref):
    valid = pl.program_id(0) < nv_ref[0]

    @pl.when(valid)
    def _():
        ys_ref[...] = jnp.dot(a_ref[...], wd_ref[0], preferred_element_type=F32)

    @pl.when(jnp.logical_not(valid))
    def _():
        ys_ref[...] = jnp.zeros(ys_ref.shape, F32)


def moe_grouped_ffn(xs, tile_expert, n_valid, wg, wu, wd, *, tm, tc=512):
    n_rows, D = xs.shape
    F = wg.shape[2]
    assert n_rows % tm == 0 and F % LANES == 0
    expert_w = lambda shape: pl.BlockSpec(shape, lambda t, te, nv: (te[t], 0, 0), pipeline_mode=pl.Buffered(1))
    a = pl.pallas_call(
        functools.partial(_gffn_up_kernel, tc=tc),
        grid_spec=pltpu.PrefetchScalarGridSpec(
            num_scalar_prefetch=2,
            grid=(n_rows // tm,),
            in_specs=[
                pl.BlockSpec((tm, D), lambda t, te, nv: (t, 0)),
                expert_w((1, D, F)),
                expert_w((1, D, F)),
            ],
            out_specs=pl.BlockSpec((tm, F), lambda t, te, nv: (t, 0)),
        ),
        out_shape=jax.ShapeDtypeStruct((n_rows, F), BF16),
        compiler_params=_cparams("parallel"),
        name="moe_ffn_up",
    )(tile_expert, n_valid, xs, wg, wu)
    return pl.pallas_call(
        _gffn_down_kernel,
        grid_spec=pltpu.PrefetchScalarGridSpec(
            num_scalar_prefetch=2,
            grid=(n_rows // tm,),
            in_specs=[
                pl.BlockSpec((tm, F), lambda t, te, nv: (t, 0)),
                pl.BlockSpec((1, F, D), lambda t, te, nv: (te[t], 0, 0)),
            ],
            out_specs=pl.BlockSpec((tm, D), lambda t, te, nv: (t, 0)),
        ),
        out_shape=jax.ShapeDtypeStruct((n_rows, D), F32),
        compiler_params=_cparams("parallel"),
        name="moe_ffn_down",
    )(tile_expert, n_valid, a, wd)


def _combine_kernel(dest_ref, x_ref, meta_ref, vec_ref, ys_ref, o_ref, b0_ref, b1_ref, sem, *, final_norm):
    tm = x_ref.shape[0]
    base = pl.program_id(0) * tm
    bufs = (b0_ref, b1_ref)

    def start(t, c):
        for s in range(TOP_K):
            _row_copy(ys_ref, dest_ref[TOP_K * (base + t) + s], bufs[s], t, sem).start()
        return c

    def wait(t, c):
        for s in range(TOP_K):
            _row_copy(ys_ref, dest_ref[TOP_K * (base + t) + s], bufs[s], t, sem).wait()
        return c

    lax.fori_loop(0, tm, start, 0, unroll=ROW_DMA_UNROLL)
    lax.fori_loop(0, tm, wait, 0, unroll=ROW_DMA_UNROLL)
    y = meta_ref[:, 2:3] * b0_ref[...] + meta_ref[:, 3:4] * b1_ref[...]
    out = x_ref[...] + vec_ref[3:4] * y
    if final_norm:
        ms = jnp.mean(out * out, axis=-1, keepdims=True)
        out = out * lax.rsqrt(ms + EPS) * vec_ref[4:5]
    o_ref[...] = out


def moe_combine(x, meta, vec, ys, dest, *, tm, final_norm):
    R, D = x.shape
    return pl.pallas_call(
        functools.partial(_combine_kernel, final_norm=final_norm),
        grid_spec=pltpu.PrefetchScalarGridSpec(
            num_scalar_prefetch=1,
            grid=(R // tm,),
            in_specs=[
                pl.BlockSpec((tm, D), lambda i, dest: (i, 0)),
                pl.BlockSpec((tm, LANES), lambda i, dest: (i, 0)),
                pl.BlockSpec((SUBLANES, D), lambda i, dest: (0, 0)),
                pl.BlockSpec(memory_space=pl.ANY),
            ],
            out_specs=pl.BlockSpec((tm, D), lambda i, dest: (i, 0)),
            scratch_shapes=[pltpu.VMEM((tm, D), F32), pltpu.VMEM((tm, D), F32),
                            pltpu.SemaphoreType.DMA(())],
        ),
        out_shape=jax.ShapeDtypeStruct((R, D), F32),
        compiler_params=_cparams("arbitrary"),
        name="moe_combine",
    )(dest, x, meta, vec, ys)


def moe_layer(x, vec, router_w, wg, wu, wd, *, final_norm, tm_route, tm_rows, tm_group):
    R, D = x.shape
    n_exp = router_w.shape[1]
    h, meta, counts = moe_router(x, vec, router_w, tm=tm_route)
    counts = counts[0, :n_exp].astype(jnp.int32)
    padded = ((counts + tm_group - 1) // tm_group) * tm_group
    ends = jnp.cumsum(padded)
    offsets = ends - padded
    eid = meta[:, 0:TOP_K].astype(jnp.int32)
    rank = meta[:, 4:4 + TOP_K].astype(jnp.int32)
    dest = (offsets[eid] + rank).reshape(-1)
    n_tiles = (TOP_K * R) // tm_group + n_exp
    tile_start = jnp.arange(n_tiles, dtype=jnp.int32) * tm_group
    n_valid = ends[-1] // tm_group
    tile_expert = jnp.sum(tile_start[:, None] >= ends[None, :], axis=1).astype(jnp.int32)
    last_expert = jnp.take(tile_expert, jnp.maximum(n_valid - 1, 0))
    tile_expert = jnp.where(jnp.arange(n_tiles) < n_valid, tile_expert, last_expert)
    xs = moe_dispatch(h, dest, n_tiles * tm_group, tm=tm_rows)
    ys = moe_grouped_ffn(xs, tile_expert, n_valid.reshape(1).astype(jnp.int32), wg, wu, wd,
                         tm=tm_group)
    return moe_combine(x, meta, vec, ys, dest, tm=tm_rows, final_norm=final_norm)


def _final_norm_kernel(x_ref, g_ref, o_ref):
    x = x_ref[...]
    ms = jnp.mean(x * x, axis=-1, keepdims=True)
    o_ref[...] = x * lax.rsqrt(ms + EPS) * g_ref[0:1]


def final_norm(x, vec, *, tm):
    R, D = x.shape
    return pl.pallas_call(
        _final_norm_kernel,
        grid=(R // tm,),
        in_specs=[pl.BlockSpec((tm, D), lambda i: (i, 0)), pl.BlockSpec((SUBLANES, D), lambda i: (0, 0))],
        out_specs=pl.BlockSpec((tm, D), lambda i: (i, 0)),
        out_shape=jax.ShapeDtypeStruct((R, D), F32),
        compiler_params=_cparams("parallel"),
        name="final_norm",
    )(x, vec)


def _rope_tables(n_tokens):
    rows = n_tokens // GRID_W
    row = jnp.repeat(jnp.arange(rows, dtype=F32), GRID_W)
    col = jnp.tile(jnp.arange(GRID_W, dtype=F32), rows)
    inv = ROPE_THETA ** (-jnp.arange(ROPE_FREQS, dtype=F32) / ROPE_FREQS)
    ar = row[:, None] * inv
    ac = col[:, None] * inv
    ang = jnp.concatenate([ar, ar, ac, ac], axis=-1)
    cos, sin = jnp.cos(ang), jnp.sin(ang)
    lane = jnp.arange(HEAD_DIM) % (2 * ROPE_FREQS)
    lo = lane < ROPE_FREQS
    return cos, jnp.where(lo, -sin, 0.0), jnp.where(lo, 0.0, sin)


def _vec(*rows):
    d = rows[0].shape[-1]
    out = jnp.zeros((SUBLANES, d), F32)
    for i, r in enumerate(rows):
        out = out.at[i].set(r.reshape(d))
    return out


def _tile(n, pref):
    t = pref
    while n % t:
        t //= 2
    return t


def kernel(x, c, ctx, c_ctx, ada_w, ada_b, norm_attn_g, norm_ffn_g, w_in, b_gate, q_norm_g, k_norm_g,
           w_attn_proj, w_four_proj, w_out, ffn_w_gate, ffn_w_up, ffn_w_down, router_w, moe_w_gate,
           moe_w_up, moe_w_down, final_norm_g):
    B, S, D = x.shape
    assert B == 1 and c.shape[0] == 1 and ctx.shape[0] == 1
    depth = ada_w.shape[0]
    q_w = w_attn_proj.shape[1]
    four_w = w_four_proj.shape[1]
    kv_w = (w_in.shape[2] - q_w - four_w - 2 * D) // 2
    n_kv = kv_w // HEAD_DIM

    cond = jnp.zeros((SUBLANES, D), F32).at[0].set(c[0]).at[1].set(c_ctx)
    mods = ada_modulation(cond, ada_w, ada_b)
    rope = _rope_tables(S)
    w_in_bf, wa_bf, wf_bf, wo_bf = (w.astype(BF16) for w in (w_in, w_attn_proj, w_four_proj, w_out))
    xl = x[0]
    xc = ctx[0]
    C = xc.shape[0]

    def mod(l, r, k):
        return mods[l, r, k * D:(k + 1) * D]

    def project(l, xs_, r, rope_, tm):
        vec = _vec(norm_attn_g[l], mod(l, r, 0), mod(l, r, 1))
        qkg = _vec(q_norm_g[l], k_norm_g[l])
        return in_projection(xs_, vec, w_in_bf, l, b_gate[l].reshape(1, -1), qkg, rope_,
                             q_w=q_w, kv_w=kv_w, four_w=four_w, tm=tm)

    def finish_mixer(l, xs_, r, proj, extra, tm):
        qt, k, vt, f, gates = proj
        n = xs_.shape[0]
        tq, tk = _tile(n, 512), _tile(n, 512)
        if extra is None:
            attn = attention(qt, k, vt, None, n_kv=n_kv, tq=tq, tk=tk)
        else:
            bound = (SCORE_BOUND_SLACK * HEAD_DIM * ATTN_SCALE * LOG2E
                     * jnp.max(jnp.abs(q_norm_g[l])) * jnp.max(jnp.abs(k_norm_g[l]))).reshape(1)
            attn = lax.cond(
                bound[0] < MAX_SCORE_BOUND,
                lambda: attention_bounded(qt, k, vt, extra, bound, n_kv=n_kv, tq=tq, tk=tk),
                lambda: attention(qt, k, vt, extra, n_kv=n_kv, tq=tq, tk=tk))
        four = fourier_mix(f)
        return merge(xs_, attn, four, gates, _vec(mod(l, r, 2)), wa_bf, wf_bf, wo_bf, l, tm=tm)

    def channel_mixer(l, xs_, r, fin, tm):
        i = l // 2
        vec = _vec(norm_ffn_g[l], mod(l, r, 3), mod(l, r, 4), mod(l, r, 5), final_norm_g)
        if l % 2 == 0:
            y = dense_ffn(xs_, vec, ffn_w_gate[i].astype(BF16), ffn_w_up[i].astype(BF16),
                          ffn_w_down[i].astype(BF16), tm=tm, tf=512)
            if fin:
                y = final_norm(y, _vec(final_norm_g), tm=tm)
            return y
        return moe_layer(xs_, vec, router_w[i], moe_w_gate[i].astype(BF16), moe_w_up[i].astype(BF16),
                         moe_w_down[i].astype(BF16), final_norm=fin,
                         tm_route=_tile(xs_.shape[0], 512), tm_rows=_tile(xs_.shape[0], 256),
                         tm_group=512)

    for l in range(depth):
        last = l == depth - 1
        tmc = _tile(C, 256)
        proj_c = project(l, xc, 1, None, tmc)
        if not last:
            xc_next = finish_mixer(l, xc, 1, proj_c, None, tmc)
            xc_next = channel_mixer(l, xc_next, 1, False, tmc)
        proj = project(l, xl, 0, rope, _tile(S, 256))
        xl = finish_mixer(l, xl, 0, proj, (proj_c[1], proj_c[2]), _tile(S, 256))
        xl = channel_mixer(l, xl, 0, last, _tile(S, 512))
        if not last:
            xc = xc_next
    return xl[None]
```

```python
import functools
import math

import numpy as np
import jax
import jax.numpy as jnp
from jax import lax
from jax.experimental import pallas as pl
from jax.experimental.pallas import tpu as pltpu

F32 = jnp.float32
BF16 = jnp.bfloat16

HEAD_DIM = 128
GRID_W = 64
ROPE_THETA = 10000.0
ROPE_FREQS = HEAD_DIM // 4
ATTN_SCALE = HEAD_DIM ** -0.5
N_FOURIER_GROUPS = 4
TOP_K = 2
EPS = 1e-6

LANES = 128
SUBLANES = 8
VMEM_LIMIT_BYTES = 58 * 1024 * 1024
NEG_INF = -1e30
LOG2E = math.log2(math.e)
BF16_SUBLANES = 16
VT_ROWS = HEAD_DIM + BF16_SUBLANES
SCORE_BOUND_SLACK = 1.02
ROW_DMA_UNROLL = 8
MAX_SCORE_BOUND = 60.0


def _cparams(*sem):
    return pltpu.CompilerParams(dimension_semantics=sem, vmem_limit_bytes=VMEM_LIMIT_BYTES)


def _norm_mod(x, g, shift, scale):
    ms = jnp.mean(x * x, axis=-1, keepdims=True)
    return (x * lax.rsqrt(ms + EPS) * g) * (1.0 + scale) + shift


def _silu(x):
    return x * jax.nn.sigmoid(x)


def _ada_kernel(cond_ref, w_ref, b_ref, o_ref):
    s = _silu(cond_ref[...]).astype(BF16)
    w = w_ref[0].astype(BF16)
    o_ref[0] = jnp.dot(s, w, preferred_element_type=F32) + b_ref[0]


def ada_modulation(cond, ada_w, ada_b, tn=1024):
    L, D, N = ada_w.shape
    return pl.pallas_call(
        _ada_kernel,
        grid=(L, N // tn),
        in_specs=[
            pl.BlockSpec((SUBLANES, D), lambda l, j: (0, 0)),
            pl.BlockSpec((1, D, tn), lambda l, j: (l, 0, j)),
            pl.BlockSpec((1, 1, tn), lambda l, j: (l, 0, j)),
        ],
        out_specs=pl.BlockSpec((1, SUBLANES, tn), lambda l, j: (l, 0, j)),
        out_shape=jax.ShapeDtypeStruct((L, SUBLANES, N), F32),
        compiler_params=_cparams("parallel", "parallel"),
        name="ada_modulation",
    )(cond, ada_w, ada_b.reshape(L, 1, N))


def _inproj_kernel(x_ref, vec_ref, w_ref, bg_ref, qkg_ref, cos_ref, sa_ref, sb_ref,
                   qt_ref, k_ref, vt_ref, f_ref, gates_ref, hs_ref, *, q_w, kv_w, four_w, tn, use_rope):
    hs_ref[...] = _norm_mod(x_ref[...], vec_ref[0:1], vec_ref[1:2], vec_ref[2:3]).astype(BF16)
    n_tiles = w_ref.shape[1] // tn
    gate0 = q_w + 2 * kv_w + four_w

    def matmul(j):
        return jnp.dot(hs_ref[...], w_ref[:, j * tn:(j + 1) * tn], preferred_element_type=F32)

    def head_norm(t, g, scale):
        ms = jnp.mean(t * t, axis=-1, keepdims=True)
        y = t * lax.rsqrt(ms + EPS) * g
        if use_rope:
            y = (y * cos_ref[...] + pltpu.roll(y, HEAD_DIM - ROPE_FREQS, 1) * sa_ref[...]
                 + pltpu.roll(y, ROPE_FREQS, 1) * sb_ref[...])
        return y * scale if scale != 1.0 else y

    def epilogue(j, acc):
        for c in range(0, tn, HEAD_DIM):
            col = j * tn + c
            t = acc[:, c:c + HEAD_DIM]
            if col < q_w:
                qt_ref[col:col + HEAD_DIM, :] = head_norm(t, qkg_ref[0:1], ATTN_SCALE * LOG2E).T.astype(BF16)
            elif col < q_w + kv_w:
                k_ref[:, col - q_w:col - q_w + HEAD_DIM] = head_norm(t, qkg_ref[1:2], 1.0).astype(BF16)
            elif col < q_w + 2 * kv_w:
                r0 = (col - q_w - kv_w) // HEAD_DIM * VT_ROWS
                vt_ref[r0:r0 + HEAD_DIM, :] = t.T.astype(BF16)
                vt_ref[r0 + HEAD_DIM:r0 + VT_ROWS, :] = jnp.ones((VT_ROWS - HEAD_DIM, vt_ref.shape[1]), BF16)
            elif col < gate0:
                f_ref[:, col - gate0 + four_w:col - gate0 + four_w + HEAD_DIM] = t
            else:
                g0 = col - gate0
                gates_ref[:, g0:g0 + HEAD_DIM] = jax.nn.sigmoid(t + bg_ref[:, g0:g0 + HEAD_DIM]).astype(BF16)

    acc = matmul(0)
    for j in range(n_tiles):
        nxt = matmul(j + 1) if j + 1 < n_tiles else None
        epilogue(j, acc)
        acc = nxt


def in_projection(x, vec, w_in, layer, b_gate, qkg, rope, *, q_w, kv_w, four_w, tm, tn=1024):
    R, D = x.shape
    n_cols = w_in.shape[2]
    n_kv = kv_w // HEAD_DIM
    assert n_cols == q_w + 2 * kv_w + four_w + 2 * D and n_cols % tn == 0 and R % tm == 0 and tm % LANES == 0
    use_rope = rope is not None
    if rope is None:
        rope = (jnp.zeros((SUBLANES, HEAD_DIM), F32),) * 3
        rope_spec = pl.BlockSpec((SUBLANES, HEAD_DIM), lambda i: (0, 0))
    else:
        rope_spec = pl.BlockSpec((tm, HEAD_DIM), lambda i: (i, 0))
    kern = functools.partial(_inproj_kernel, q_w=q_w, kv_w=kv_w, four_w=four_w, tn=tn, use_rope=use_rope)
    return pl.pallas_call(
        kern,
        grid=(R // tm,),
        in_specs=[
            pl.BlockSpec((tm, D), lambda i: (i, 0)),
            pl.BlockSpec((SUBLANES, D), lambda i: (0, 0)),
            pl.BlockSpec((None, D, n_cols), lambda i: (layer, 0, 0), pipeline_mode=pl.Buffered(1)),
            pl.BlockSpec((1, 2 * D), lambda i: (0, 0)),
            pl.BlockSpec((SUBLANES, HEAD_DIM), lambda i: (0, 0)),
            rope_spec, rope_spec, rope_spec,
        ],
        out_specs=[
            pl.BlockSpec((q_w, tm), lambda i: (0, i)),
            pl.BlockSpec((tm, kv_w), lambda i: (i, 0)),
            pl.BlockSpec((n_kv * VT_ROWS, tm), lambda i: (0, i)),
            pl.BlockSpec((tm, four_w), lambda i: (i, 0)),
            pl.BlockSpec((tm, 2 * D), lambda i: (i, 0)),
        ],
        out_shape=[
            jax.ShapeDtypeStruct((q_w, R), BF16),
            jax.ShapeDtypeStruct((R, kv_w), BF16),
            jax.ShapeDtypeStruct((n_kv * VT_ROWS, R), BF16),
            jax.ShapeDtypeStruct((R, four_w), F32),
            jax.ShapeDtypeStruct((R, 2 * D), BF16),
        ],
        scratch_shapes=[pltpu.VMEM((tm, D), BF16)],
        compiler_params=_cparams("parallel"),
        name="in_projection",
    )(x, vec, w_in, b_gate, qkg, *rope)


def _attn_kernel(*refs, group, tk, n_chunks, has_extra):
    if has_extra:
        qt_ref, k_ref, vt_ref, kx_ref, vxt_ref, o_ref, m_ref, acc_ref, sa_ref, sb_ref, sx_ref = refs
    else:
        qt_ref, k_ref, vt_ref, o_ref, m_ref, acc_ref, sa_ref, sb_ref = refs
    m_ref[...] = jnp.full(m_ref.shape, NEG_INF, F32)
    acc_ref[...] = jnp.zeros(acc_ref.shape, F32)

    def scores(k, s_ref):
        for g in range(group):
            s_ref[g] = jnp.dot(k, qt_ref[g * HEAD_DIM:(g + 1) * HEAD_DIM, :], preferred_element_type=F32)

    def softmax_pv(vt, s_ref):
        for g in range(group):
            s = s_ref[g]
            m_prev = m_ref[g]
            m_new = jnp.maximum(m_prev, jnp.max(s, axis=0, keepdims=True))
            alpha = jnp.exp2(m_prev - m_new)
            p = jnp.exp2(s - m_new).astype(BF16)
            acc_ref[g] = alpha * acc_ref[g] + jnp.dot(vt, p, preferred_element_type=F32)
            m_ref[g] = m_new

    def k_chunk(c):
        return k_ref[pl.ds(pl.multiple_of(c * tk, tk), tk), :]

    def vt_chunk(c):
        return vt_ref[:, pl.ds(pl.multiple_of(c * tk, tk), tk)]

    n_pairs = (n_chunks - 1) // 2
    scores(k_chunk(0), sa_ref)

    def pair(i, carry):
        c = 2 * i
        scores(k_chunk(c + 1), sb_ref)
        softmax_pv(vt_chunk(c), sa_ref)
        scores(k_chunk(c + 2), sa_ref)
        softmax_pv(vt_chunk(c + 1), sb_ref)
        return carry

    lax.fori_loop(0, n_pairs, pair, 0)
    c = 2 * n_pairs
    if n_chunks - c == 2:
        scores(k_chunk(c + 1), sb_ref)
    softmax_pv(vt_chunk(c), sa_ref)
    if has_extra:
        scores(kx_ref[...], sx_ref)
    if n_chunks - c == 2:
        softmax_pv(vt_chunk(c + 1), sb_ref)
    if has_extra:
        softmax_pv(vxt_ref[...], sx_ref)
    for g in range(group):
        o = acc_ref[g, :HEAD_DIM, :] / acc_ref[g, HEAD_DIM:HEAD_DIM + 1, :]
        o_ref[:, g * HEAD_DIM:(g + 1) * HEAD_DIM] = o.T.astype(BF16)


def attention(qt, k, vt, extra, *, n_kv, tq, tk):
    q_w, R = qt.shape
    group = q_w // HEAD_DIM // n_kv
    gw = group * HEAD_DIM
    assert R % tq == 0 and R % tk == 0
    in_specs = [
        pl.BlockSpec((gw, tq), lambda h, i: (h, i)),
        pl.BlockSpec((R, HEAD_DIM), lambda h, i: (0, h)),
        pl.BlockSpec((VT_ROWS, R), lambda h, i: (h, 0)),
    ]
    args = [qt, k, vt]
    scratch = [
        pltpu.VMEM((group, 1, tq), F32),
        pltpu.VMEM((group, VT_ROWS, tq), F32),
        pltpu.VMEM((group, tk, tq), F32),
        pltpu.VMEM((group, tk, tq), F32),
    ]
    if extra is not None:
        rx = extra[0].shape[0]
        in_specs += [
            pl.BlockSpec((rx, HEAD_DIM), lambda h, i: (0, h)),
            pl.BlockSpec((VT_ROWS, rx), lambda h, i: (h, 0)),
        ]
        args += list(extra)
        scratch.append(pltpu.VMEM((group, rx, tq), F32))
    kern = functools.partial(_attn_kernel, group=group, tk=tk, n_chunks=R // tk, has_extra=extra is not None)
    return pl.pallas_call(
        kern,
        grid=(n_kv, R // tq),
        in_specs=in_specs,
        out_specs=pl.BlockSpec((tq, gw), lambda h, i: (i, h)),
        out_shape=jax.ShapeDtypeStruct((R, q_w), BF16),
        scratch_shapes=scratch,
        compiler_params=_cparams("parallel", "parallel"),
        name="attention",
    )(*args)


def _attn_bounded_kernel(bound_ref, qt_ref, k_ref, vt_ref, kx_ref, vxt_ref, o_ref, acc_ref, sa_ref, sb_ref, sx_ref,
                         *, group, tk, n_chunks):
    acc_ref[...] = jnp.zeros(acc_ref.shape, F32)
    bound = bound_ref[0]

    def scores(k, s_ref, g):
        s_ref[g] = jnp.dot(k, qt_ref[g * HEAD_DIM:(g + 1) * HEAD_DIM, :], preferred_element_type=F32)

    def exp_pv(vt, s_ref, g):
        p = jnp.exp2(s_ref[g] - bound).astype(BF16)
        acc_ref[g] += jnp.dot(vt, p, preferred_element_type=F32)

    def k_chunk(c):
        return k_ref[pl.ds(pl.multiple_of(c * tk, tk), tk), :]

    def vt_chunk(c):
        return vt_ref[:, pl.ds(pl.multiple_of(c * tk, tk), tk)]

    def step(k_next, s_next, vt_cur, s_cur):
        for g in range(group):
            if k_next is not None:
                scores(k_next, s_next, g)
            exp_pv(vt_cur, s_cur, g)

    n_pairs = (n_chunks - 1) // 2
    for g in range(group):
        scores(k_chunk(0), sa_ref, g)

    def pair(i, carry):
        c = 2 * i
        step(k_chunk(c + 1), sb_ref, vt_chunk(c), sa_ref)
        step(k_chunk(c + 2), sa_ref, vt_chunk(c + 1), sb_ref)
        return carry

    lax.fori_loop(0, n_pairs, pair, 0)
    c = 2 * n_pairs
    if n_chunks - c == 2:
        step(k_chunk(c + 1), sb_ref, vt_chunk(c), sa_ref)
        step(kx_ref[...], sx_ref, vt_chunk(c + 1), sb_ref)
    else:
        step(kx_ref[...], sx_ref, vt_chunk(c), sa_ref)
    step(None, None, vxt_ref[...], sx_ref)
    for g in range(group):
        o = acc_ref[g, :HEAD_DIM, :] / acc_ref[g, HEAD_DIM:HEAD_DIM + 1, :]
        o_ref[:, g * HEAD_DIM:(g + 1) * HEAD_DIM] = o.T.astype(BF16)


def attention_bounded(qt, k, vt, extra, bound, *, n_kv, tq, tk):
    q_w, R = qt.shape
    group = q_w // HEAD_DIM // n_kv
    gw = group * HEAD_DIM
    rx = extra[0].shape[0]
    assert R % tq == 0 and R % tk == 0
    kern = functools.partial(_attn_bounded_kernel, group=group, tk=tk, n_chunks=R // tk)
    return pl.pallas_call(
        kern,
        grid=(n_kv, R // tq),
        in_specs=[
            pl.BlockSpec(memory_space=pltpu.SMEM),
            pl.BlockSpec((gw, tq), lambda h, i: (h, i)),
            pl.BlockSpec((R, HEAD_DIM), lambda h, i: (0, h)),
            pl.BlockSpec((VT_ROWS, R), lambda h, i: (h, 0)),
            pl.BlockSpec((rx, HEAD_DIM), lambda h, i: (0, h)),
            pl.BlockSpec((VT_ROWS, rx), lambda h, i: (h, 0)),
        ],
        out_specs=pl.BlockSpec((tq, gw), lambda h, i: (i, h)),
        out_shape=jax.ShapeDtypeStruct((R, q_w), BF16),
        scratch_shapes=[pltpu.VMEM((group, VT_ROWS, tq), F32), pltpu.VMEM((group, tk, tq), F32),
                        pltpu.VMEM((group, tk, tq), F32), pltpu.VMEM((group, rx, tq), F32)],
        compiler_params=_cparams("parallel", "parallel"),
        name="attention_bounded",
    )(bound, qt, k, vt, *extra)


def _dft_mats(n, scale, repeat=1):
    k = np.arange(n)
    ang = 2.0 * np.pi * ((k[:, None] * k[None, :]) % n) / n
    eye = np.eye(repeat)
    return tuple(jnp.asarray(np.kron(m * scale, eye), BF16) for m in (np.cos(ang), np.sin(ang)))


def _four_small_kernel(u_ref, cn_ref, sn_ref, cc_ref, sc_ref, o_ref, *, gc):
    u = u_ref[...].astype(BF16)
    for g in range(u.shape[1] // gc):
        ug = u[:, g * gc:(g + 1) * gc]
        a = jnp.dot(ug, cc_ref[...], preferred_element_type=F32).astype(BF16)
        b = jnp.dot(ug, sc_ref[...], preferred_element_type=F32).astype(BF16)
        o = (jnp.dot(cn_ref[...], a, preferred_element_type=F32)
             - jnp.dot(sn_ref[...], b, preferred_element_type=F32))
        o_ref[:, g * gc:(g + 1) * gc] = o.astype(BF16)


def fourier_mix_small(u):
    n, c = u.shape
    gc = c // N_FOURIER_GROUPS
    cn, sn = _dft_mats(n, n ** -0.5)
    cc, sc = _dft_mats(gc, gc ** -0.5)
    full = lambda a: pl.BlockSpec(a.shape, lambda: (0,) * a.ndim)
    return pl.pallas_call(
        functools.partial(_four_small_kernel, gc=gc),
        in_specs=[full(u), full(cn), full(sn), full(cc), full(sc)],
        out_specs=pl.BlockSpec((n, c), lambda: (0, 0)),
        out_shape=jax.ShapeDtypeStruct((n, c), BF16),
        compiler_params=pltpu.CompilerParams(vmem_limit_bytes=VMEM_LIMIT_BYTES),
        name="fourier_small",
    )(u, cn, sn, cc, sc)


def _four_stage1_kernel(u_ref, ca_ref, sa_ref, cw_ref, sw_ref, yr_ref, yi_ref, *, tb):
    na, _, c = u_ref.shape
    u = u_ref[...].reshape(na * tb, c).astype(BF16)
    yr = jnp.dot(ca_ref[...], u, preferred_element_type=F32)
    yi = -jnp.dot(sa_ref[...], u, preferred_element_type=F32)
    cw = cw_ref[...].reshape(na * tb, LANES)
    sw = sw_ref[...].reshape(na * tb, LANES)
    for l in range(c // LANES):
        sl = slice(l * LANES, (l + 1) * LANES)
        r, i = yr[:, sl], yi[:, sl]
        yr_ref[:, :, sl] = (r * cw + i * sw).reshape(na, tb, LANES)
        yi_ref[:, :, sl] = (i * cw - r * sw).reshape(na, tb, LANES)


def _four_stage2_kernel(yr_ref, yi_ref, cb_ref, sb_ref, cc_ref, sc_ref, o_ref, *, nb, c, gc, tk1):
    for t in range(tk1):
        yr = yr_ref[t * nb:(t + 1) * nb, :].astype(BF16)
        yi = yi_ref[t * nb:(t + 1) * nb, :].astype(BF16)
        pr = (jnp.dot(cb_ref[...], yr, preferred_element_type=F32)
              + jnp.dot(sb_ref[...], yi, preferred_element_type=F32)).astype(BF16)
        pi = (jnp.dot(cb_ref[...], yi, preferred_element_type=F32)
              - jnp.dot(sb_ref[...], yr, preferred_element_type=F32)).astype(BF16)
        for g in range(c // gc):
            sl = slice(g * gc, (g + 1) * gc)
            o = (jnp.dot(pr[:, sl], cc_ref[...], preferred_element_type=F32)
                 + jnp.dot(pi[:, sl], sc_ref[...], preferred_element_type=F32))
            o_ref[:, t, g * gc:(g + 1) * gc] = o


def fourier_mix_large(u, *, na=128, tb=SUBLANES, tk1=SUBLANES):
    n, c = u.shape
    nb = n // na
    assert na * nb == n and nb % tb == 0 and na % tk1 == 0 and nb % SUBLANES == 0
    gc = c // N_FOURIER_GROUPS
    ca_k, sa_k = _dft_mats(na, na ** -0.5, repeat=tb)
    cb, sb = _dft_mats(nb, nb ** -0.5)
    cc, sc = _dft_mats(gc, gc ** -0.5)
    k1 = lax.broadcasted_iota(jnp.int32, (na, nb, LANES), 0)
    b = lax.broadcasted_iota(jnp.int32, (na, nb, LANES), 1)
    ang = ((k1 * b) % n).astype(F32) * (2.0 * math.pi / n)
    cw, sw = jnp.cos(ang), jnp.sin(ang)
    const = lambda a: pl.BlockSpec(a.shape, lambda i: (0,) * a.ndim)
    yr, yi = pl.pallas_call(
        functools.partial(_four_stage1_kernel, tb=tb),
        grid=(nb // tb,),
        in_specs=[
            pl.BlockSpec((na, tb, c), lambda i: (0, i, 0)),
            const(ca_k), const(sa_k),
            pl.BlockSpec((na, tb, LANES), lambda i: (0, i, 0)),
            pl.BlockSpec((na, tb, LANES), lambda i: (0, i, 0)),
        ],
        out_specs=[pl.BlockSpec((na, tb, c), lambda i: (0, i, 0))] * 2,
        out_shape=[jax.ShapeDtypeStruct((na, nb, c), F32)] * 2,
        compiler_params=_cparams("parallel"),
        name="fourier_stage1",
    )(u.reshape(na, nb, c), ca_k, sa_k, cw, sw)
    out = pl.pallas_call(
        functools.partial(_four_stage2_kernel, nb=nb, c=c, gc=gc, tk1=tk1),
        grid=(na // tk1,),
        in_specs=[
            pl.BlockSpec((tk1 * nb, c), lambda i: (i, 0)),
            pl.BlockSpec((tk1 * nb, c), lambda i: (i, 0)),
            const(cb), const(sb), const(cc), const(sc),
        ],
        out_specs=pl.BlockSpec((nb, tk1, c), lambda i: (0, i, 0)),
        out_shape=jax.ShapeDtypeStruct((nb, na, c), F32),
        compiler_params=_cparams("parallel"),
        name="fourier_stage2",
    )(yr.reshape(na * nb, c), yi.reshape(na * nb, c), cb, sb, cc, sc)
    return out.reshape(n, c)


def fourier_mix(u):
    n = u.shape[0]
    if n % (128 * SUBLANES) == 0 and n >= 128 * 16:
        return fourier_mix_large(u)
    return fourier_mix_small(u)


def _merge_kernel(x_ref, attn_ref, four_ref, gates_ref, vec_ref, wa_ref, wf_ref, wo_ref, o_ref):
    d = x_ref.shape[1]
    a = jnp.dot(attn_ref[...], wa_ref[...], preferred_element_type=F32)
    f = jnp.dot(four_ref[...].astype(BF16), wf_ref[...], preferred_element_type=F32)
    y = gates_ref[:, :d].astype(F32) * a + gates_ref[:, d:].astype(F32) * f
    y = jnp.dot(y.astype(BF16), wo_ref[...], preferred_element_type=F32)
    o_ref[...] = x_ref[...] + vec_ref[0:1] * y


def merge(x, attn, four, gates, vec, wa, wf, wo, layer, *, tm):
    R, D = x.shape
    row = lambda w: pl.BlockSpec((tm, w), lambda i: (i, 0))
    const = lambda a: pl.BlockSpec((None,) + a.shape[1:], lambda i: (layer, 0, 0), pipeline_mode=pl.Buffered(1))
    return pl.pallas_call(
        _merge_kernel,
        grid=(R // tm,),
        in_specs=[row(D), row(attn.shape[1]), row(four.shape[1]), row(gates.shape[1]),
                  pl.BlockSpec((SUBLANES, D), lambda i: (0, 0)), const(wa), const(wf), const(wo)],
        out_specs=row(D),
        out_shape=jax.ShapeDtypeStruct((R, D), F32),
        compiler_params=_cparams("parallel"),
        name="merge",
    )(x, attn, four, gates, vec, wa, wf, wo)


def _ffn_kernel(x_ref, vec_ref, wg_ref, wu_ref, wd_ref, o_ref, hs_ref, acc_ref):
    j = pl.program_id(1)

    @pl.when(j == 0)
    def _():
        hs_ref[...] = _norm_mod(x_ref[...], vec_ref[0:1], vec_ref[1:2], vec_ref[2:3]).astype(BF16)
        acc_ref[...] = jnp.zeros(acc_ref.shape, F32)

    h = hs_ref[...]
    g = jnp.dot(h, wg_ref[...], preferred_element_type=F32)
    u = jnp.dot(h, wu_ref[...], preferred_element_type=F32)
    acc_ref[...] += jnp.dot((_silu(g) * u).astype(BF16), wd_ref[...], preferred_element_type=F32)

    @pl.when(j == pl.num_programs(1) - 1)
    def _():
        o_ref[...] = x_ref[...] + vec_ref[3:4] * acc_ref[...]


def dense_ffn(x, vec, wg, wu, wd, *, tm, tf):
    R, D = x.shape
    F = wg.shape[1]
    assert R % tm == 0 and F % tf == 0
    return pl.pallas_call(
        _ffn_kernel,
        grid=(R // tm, F // tf),
        in_specs=[
            pl.BlockSpec((tm, D), lambda i, j: (i, 0)),
            pl.BlockSpec((SUBLANES, D), lambda i, j: (0, 0)),
            pl.BlockSpec((D, tf), lambda i, j: (0, j)),
            pl.BlockSpec((D, tf), lambda i, j: (0, j)),
            pl.BlockSpec((tf, D), lambda i, j: (j, 0)),
        ],
        out_specs=pl.BlockSpec((tm, D), lambda i, j: (i, 0)),
        out_shape=jax.ShapeDtypeStruct((R, D), F32),
        scratch_shapes=[pltpu.VMEM((tm, D), BF16), pltpu.VMEM((tm, D), F32)],
        compiler_params=_cparams("parallel", "arbitrary"),
        name="dense_ffn",
    )(x, vec, wg, wu, wd)


def _router_kernel(x_ref, vec_ref, rw_ref, h_ref, meta_ref, cnt_ref, run_ref, *, n_exp):
    i = pl.program_id(0)

    @pl.when(i == 0)
    def _():
        run_ref[...] = jnp.zeros(run_ref.shape, F32)

    h = _norm_mod(x_ref[...], vec_ref[0:1], vec_ref[1:2], vec_ref[2:3])
    h_ref[...] = h
    tm = h.shape[0]
    logits = [jnp.sum(h * rw_ref[e:e + 1], axis=-1, keepdims=True) for e in range(n_exp)]
    m1 = logits[0]
    for e in range(1, n_exp):
        m1 = jnp.maximum(m1, logits[e])
    i1 = jnp.full((tm, 1), n_exp, jnp.int32)
    for e in reversed(range(n_exp)):
        i1 = jnp.where(logits[e] == m1, e, i1)
    m2 = jnp.full((tm, 1), NEG_INF, F32)
    for e in range(n_exp):
        m2 = jnp.maximum(m2, jnp.where(i1 == e, NEG_INF, logits[e]))
    i2 = jnp.full((tm, 1), n_exp, jnp.int32)
    for e in reversed(range(n_exp)):
        i2 = jnp.where((logits[e] == m2) & (i1 != e), e, i2)
    e2 = jnp.exp(m2 - m1)
    w1 = 1.0 / (1.0 + e2)
    w2 = e2 / (1.0 + e2)

    lane = lax.broadcasted_iota(jnp.int32, (tm, LANES), 1)
    hit1 = lane == i1
    hit2 = lane == i2
    onehot = (hit1 | hit2).astype(F32)
    r = lax.broadcasted_iota(jnp.int32, (tm, tm), 0)
    cidx = lax.broadcasted_iota(jnp.int32, (tm, tm), 1)
    ltri = (cidx < r).astype(BF16)
    before = jnp.dot(ltri, onehot.astype(BF16), preferred_element_type=F32) + run_ref[0:1]
    rank1 = jnp.sum(jnp.where(hit1, before, 0.0), axis=-1, keepdims=True)
    rank2 = jnp.sum(jnp.where(hit2, before, 0.0), axis=-1, keepdims=True)
    run_ref[0:1] = run_ref[0:1] + jnp.sum(onehot, axis=0, keepdims=True)

    meta = jnp.where(lane == 0, i1.astype(F32), 0.0)
    meta = jnp.where(lane == 1, i2.astype(F32), meta)
    meta = jnp.where(lane == 2, w1, meta)
    meta = jnp.where(lane == 3, w2, meta)
    meta = jnp.where(lane == 4, rank1, meta)
    meta = jnp.where(lane == 5, rank2, meta)
    meta_ref[...] = meta
    cnt_ref[...] = run_ref[...]


def moe_router(x, vec, router_w, *, tm):
    R, D = x.shape
    n_exp = router_w.shape[1]
    rw = jnp.zeros((SUBLANES * ((n_exp + SUBLANES - 1) // SUBLANES), D), F32).at[:n_exp].set(router_w.T)
    return pl.pallas_call(
        functools.partial(_router_kernel, n_exp=n_exp),
        grid=(R // tm,),
        in_specs=[
            pl.BlockSpec((tm, D), lambda i: (i, 0)),
            pl.BlockSpec((SUBLANES, D), lambda i: (0, 0)),
            pl.BlockSpec(rw.shape, lambda i: (0, 0)),
        ],
        out_specs=[
            pl.BlockSpec((tm, D), lambda i: (i, 0)),
            pl.BlockSpec((tm, LANES), lambda i: (i, 0)),
            pl.BlockSpec((SUBLANES, LANES), lambda i: (0, 0)),
        ],
        out_shape=[
            jax.ShapeDtypeStruct((R, D), F32),
            jax.ShapeDtypeStruct((R, LANES), F32),
            jax.ShapeDtypeStruct((SUBLANES, LANES), F32),
        ],
        scratch_shapes=[pltpu.VMEM((SUBLANES, LANES), F32)],
        compiler_params=_cparams("arbitrary"),
        name="moe_router",
    )(x, vec, rw)


def _row_copy(src_ref, s, dst_ref, d, sem):
    return pltpu.make_async_copy(src_ref.at[pl.ds(s, 1)], dst_ref.at[pl.ds(d, 1)], sem)


def _dispatch_kernel(dest_ref, ends_ref, h_ref, xs_ref, zero_ref, sem, zero_sem, *, tm_group):
    tm = h_ref.shape[0]
    base = pl.program_id(0) * tm

    @pl.when(pl.program_id(0) == 0)
    def _():
        zero_ref[...] = jnp.zeros(zero_ref.shape, F32)

        def fill(e):
            prev = ends_ref[jnp.maximum(e - 1, 0)]
            start = pl.multiple_of(ends_ref[e] - tm_group, tm_group)
            nonempty = (ends_ref[e] > prev) | ((e == 0) & (ends_ref[e] > 0))
            return nonempty, pltpu.make_async_copy(zero_ref, xs_ref.at[pl.ds(start, tm_group)], zero_sem)

        for e in range(ends_ref.shape[0]):
            nonempty, copy = fill(e)
            pl.when(nonempty)(copy.start)
        for e in range(ends_ref.shape[0]):
            nonempty, copy = fill(e)
            pl.when(nonempty)(copy.wait)

        def fill_tail(t, c):
            copy = pltpu.make_async_copy(
                zero_ref, xs_ref.at[pl.ds(pl.multiple_of(t * tm_group, tm_group), tm_group)], zero_sem)
            copy.start()
            copy.wait()
            return c

        n_used = ends_ref[ends_ref.shape[0] - 1] // tm_group
        lax.fori_loop(n_used, xs_ref.shape[0] // tm_group, fill_tail, 0)

    def start(t, c):
        for s in range(TOP_K):
            _row_copy(h_ref, t, xs_ref, dest_ref[TOP_K * (base + t) + s], sem).start()
        return c

    def wait(t, c):
        for s in range(TOP_K):
            _row_copy(h_ref, t, xs_ref, dest_ref[TOP_K * (base + t) + s], sem).wait()
        return c

    lax.fori_loop(0, tm, start, 0, unroll=ROW_DMA_UNROLL)
    lax.fori_loop(0, tm, wait, 0, unroll=ROW_DMA_UNROLL)


def moe_dispatch(h, dest, ends, n_rows, *, tm, tm_group):
    R, D = h.shape
    return pl.pallas_call(
        functools.partial(_dispatch_kernel, tm_group=tm_group),
        grid_spec=pltpu.PrefetchScalarGridSpec(
            num_scalar_prefetch=2,
            grid=(R // tm,),
            in_specs=[pl.BlockSpec((tm, D), lambda i, dest, ends: (i, 0))],
            out_specs=pl.BlockSpec(memory_space=pl.ANY),
            scratch_shapes=[pltpu.VMEM((tm_group, D), F32), pltpu.SemaphoreType.DMA(()),
                            pltpu.SemaphoreType.DMA(())],
        ),
        out_shape=jax.ShapeDtypeStruct((n_rows, D), F32),
        compiler_params=_cparams("arbitrary"),
        name="moe_dispatch",
    )(dest, ends, h)


def _gffn_up_kernel(te_ref, nv_ref, xs_ref, wg_ref, wu_ref, a_ref, *, tc):
    valid = pl.program_id(0) < nv_ref[0]

    @pl.when(valid)
    def _():
        h = xs_ref[...].astype(BF16)
        n_cols = a_ref.shape[1]
        for c0 in range(0, n_cols, tc):
            c1 = min(c0 + tc, n_cols)
            g = jnp.dot(h, wg_ref[0, :, c0:c1], preferred_element_type=F32)
            u = jnp.dot(h, wu_ref[0, :, c0:c1], preferred_element_type=F32)
            a_ref[:, c0:c1] = (_silu(g) * u).astype(BF16)

    @pl.when(jnp.logical_not(valid))
    def _():
        a_ref[...] = jnp.zeros(a_ref.shape, BF16)


def _gffn_down_kernel(te_ref, nv_ref, a_ref, wd_ref, ys_ref):
    valid = pl.program_id(0) < nv_ref[0]

    @pl.when(valid)
    def _():
        ys_ref[...] = jnp.dot(a_ref[...], wd_ref[0], preferred_element_type=F32)

    @pl.when(jnp.logical_not(valid))
    def _():
        ys_ref[...] = jnp.zeros(ys_ref.shape, F32)


def moe_grouped_ffn(xs, tile_expert, n_valid, wg, wu, wd, *, tm, tc=512):
    n_rows, D = xs.shape
    F = wg.shape[2]
    assert n_rows % tm == 0 and F % LANES == 0
    expert_w = lambda shape: pl.BlockSpec(shape, lambda t, te, nv: (te[t], 0, 0), pipeline_mode=pl.Buffered(1))
    a = pl.pallas_call(
        functools.partial(_gffn_up_kernel, tc=tc),
        grid_spec=pltpu.PrefetchScalarGridSpec(
            num_scalar_prefetch=2,
            grid=(n_rows // tm,),
            in_specs=[
                pl.BlockSpec((tm, D), lambda t, te, nv: (jnp.minimum(t, jnp.maximum(nv[0] - 1, 0)), 0)),
                expert_w((1, D, F)),
                expert_w((1, D, F)),
            ],
            out_specs=pl.BlockSpec((tm, F), lambda t, te, nv: (t, 0)),
        ),
        out_shape=jax.ShapeDtypeStruct((n_rows, F), BF16),
        compiler_params=_cparams("parallel"),
        name="moe_ffn_up",
    )(tile_expert, n_valid, xs, wg, wu)
    return pl.pallas_call(
        _gffn_down_kernel,
        grid_spec=pltpu.PrefetchScalarGridSpec(
            num_scalar_prefetch=2,
            grid=(n_rows // tm,),
            in_specs=[
                pl.BlockSpec((tm, F), lambda t, te, nv: (t, 0)),
                pl.BlockSpec((1, F, D), lambda t, te, nv: (te[t], 0, 0)),
            ],
            out_specs=pl.BlockSpec((tm, D), lambda t, te, nv: (t, 0)),
        ),
        out_shape=jax.ShapeDtypeStruct((n_rows, D), F32),
        compiler_params=_cparams("parallel"),
        name="moe_ffn_down",
    )(tile_expert, n_valid, a, wd)


def _combine_kernel(dest_ref, x_ref, meta_ref, vec_ref, ys_ref, o_ref, b0_ref, b1_ref, sem, *, final_norm):
    tm = x_ref.shape[0]
    base = pl.program_id(0) * tm
    bufs = (b0_ref, b1_ref)

    def start(t, c):
        for s in range(TOP_K):
            _row_copy(ys_ref, dest_ref[TOP_K * (base + t) + s], bufs[s], t, sem).start()
        return c

    def wait(t, c):
        for s in range(TOP_K):
            _row_copy(ys_ref, dest_ref[TOP_K * (base + t) + s], bufs[s], t, sem).wait()
        return c

    lax.fori_loop(0, tm, start, 0, unroll=ROW_DMA_UNROLL)
    lax.fori_loop(0, tm, wait, 0, unroll=ROW_DMA_UNROLL)
    y = meta_ref[:, 2:3] * b0_ref[...] + meta_ref[:, 3:4] * b1_ref[...]
    out = x_ref[...] + vec_ref[3:4] * y
    if final_norm:
        ms = jnp.mean(out * out, axis=-1, keepdims=True)
        out = out * lax.rsqrt(ms + EPS) * vec_ref[4:5]
    o_ref[...] = out


def moe_combine(x, meta, vec, ys, dest, *, tm, final_norm):
    R, D = x.shape
    return pl.pallas_call(
        functools.partial(_combine_kernel, final_norm=final_norm),
        grid_spec=pltpu.PrefetchScalarGridSpec(
            num_scalar_prefetch=1,
            grid=(R // tm,),
            in_specs=[
                pl.BlockSpec((tm, D), lambda i, dest: (i, 0)),
                pl.BlockSpec((tm, LANES), lambda i, dest: (i, 0)),
                pl.BlockSpec((SUBLANES, D), lambda i, dest: (0, 0)),
                pl.BlockSpec(memory_space=pl.ANY),
            ],
            out_specs=pl.BlockSpec((tm, D), lambda i, dest: (i, 0)),
            scratch_shapes=[pltpu.VMEM((tm, D), F32), pltpu.VMEM((tm, D), F32),
                            pltpu.SemaphoreType.DMA(())],
        ),
        out_shape=jax.ShapeDtypeStruct((R, D), F32),
        compiler_params=_cparams("arbitrary"),
        name="moe_combine",
    )(dest, x, meta, vec, ys)


def moe_layer(x, vec, router_w, wg, wu, wd, *, final_norm, tm_route, tm_rows, tm_group):
    R, D = x.shape
    n_exp = router_w.shape[1]
    h, meta, counts = moe_router(x, vec, router_w, tm=tm_route)
    counts = counts[0, :n_exp].astype(jnp.int32)
    padded = ((counts + tm_group - 1) // tm_group) * tm_group
    ends = jnp.cumsum(padded)
    offsets = ends - padded
    eid = meta[:, 0:TOP_K].astype(jnp.int32)
    rank = meta[:, 4:4 + TOP_K].astype(jnp.int32)
    dest = (offsets[eid] + rank).reshape(-1)
    n_tiles = (TOP_K * R) // tm_group + n_exp
    tile_start = jnp.arange(n_tiles, dtype=jnp.int32) * tm_group
    n_valid = ends[-1] // tm_group
    tile_expert = jnp.sum(tile_start[:, None] >= ends[None, :], axis=1).astype(jnp.int32)
    last_expert = jnp.take(tile_expert, jnp.maximum(n_valid - 1, 0))
    tile_expert = jnp.where(jnp.arange(n_tiles) < n_valid, tile_expert, last_expert)
    xs = moe_dispatch(h, dest, ends.astype(jnp.int32), n_tiles * tm_group, tm=tm_rows, tm_group=tm_group)
    ys = moe_grouped_ffn(xs, tile_expert, n_valid.reshape(1).astype(jnp.int32), wg, wu, wd,
                         tm=tm_group)
    return moe_combine(x, meta, vec, ys, dest, tm=tm_rows, final_norm=final_norm)


def _final_norm_kernel(x_ref, g_ref, o_ref):
    x = x_ref[...]
    ms = jnp.mean(x * x, axis=-1, keepdims=True)
    o_ref[...] = x * lax.rsqrt(ms + EPS) * g_ref[0:1]


def final_norm(x, vec, *, tm):
    R, D = x.shape
    return pl.pallas_call(
        _final_norm_kernel,
        grid=(R // tm,),
        in_specs=[pl.BlockSpec((tm, D), lambda i: (i, 0)), pl.BlockSpec((SUBLANES, D), lambda i: (0, 0))],
        out_specs=pl.BlockSpec((tm, D), lambda i: (i, 0)),
        out_shape=jax.ShapeDtypeStruct((R, D), F32),
        compiler_params=_cparams("parallel"),
        name="final_norm",
    )(x, vec)


def _rope_tables(n_tokens):
    rows = n_tokens // GRID_W
    row = jnp.repeat(jnp.arange(rows, dtype=F32), GRID_W)
    col = jnp.tile(jnp.arange(GRID_W, dtype=F32), rows)
    inv = ROPE_THETA ** (-jnp.arange(ROPE_FREQS, dtype=F32) / ROPE_FREQS)
    ar = row[:, None] * inv
    ac = col[:, None] * inv
    ang = jnp.concatenate([ar, ar, ac, ac], axis=-1)
    cos, sin = jnp.cos(ang), jnp.sin(ang)
    lane = jnp.arange(HEAD_DIM) % (2 * ROPE_FREQS)
    lo = lane < ROPE_FREQS
    return cos, jnp.where(lo, -sin, 0.0), jnp.where(lo, 0.0, sin)


def _vec(*rows):
    d = rows[0].shape[-1]
    out = jnp.zeros((SUBLANES, d), F32)
    for i, r in enumerate(rows):
        out = out.at[i].set(r.reshape(d))
    return out


def _tile(n, pref):
    t = pref
    while n % t:
        t //= 2
    return t


def kernel(x, c, ctx, c_ctx, ada_w, ada_b, norm_attn_g, norm_ffn_g, w_in, b_gate, q_norm_g, k_norm_g,
           w_attn_proj, w_four_proj, w_out, ffn_w_gate, ffn_w_up, ffn_w_down, router_w, moe_w_gate,
           moe_w_up, moe_w_down, final_norm_g):
    B, S, D = x.shape
    assert B == 1 and c.shape[0] == 1 and ctx.shape[0] == 1
    depth = ada_w.shape[0]
    q_w = w_attn_proj.shape[1]
    four_w = w_four_proj.shape[1]
    kv_w = (w_in.shape[2] - q_w - four_w - 2 * D) // 2
    n_kv = kv_w // HEAD_DIM

    cond = jnp.zeros((SUBLANES, D), F32).at[0].set(c[0]).at[1].set(c_ctx)
    mods = ada_modulation(cond, ada_w, ada_b)
    rope = _rope_tables(S)
    w_in_bf, wa_bf, wf_bf, wo_bf = (w.astype(BF16) for w in (w_in, w_attn_proj, w_four_proj, w_out))
    xl = x[0]
    xc = ctx[0]
    C = xc.shape[0]

    def mod(l, r, k):
        return mods[l, r, k * D:(k + 1) * D]

    def project(l, xs_, r, rope_, tm):
        vec = _vec(norm_attn_g[l], mod(l, r, 0), mod(l, r, 1))
        qkg = _vec(q_norm_g[l], k_norm_g[l])
        return in_projection(xs_, vec, w_in_bf, l, b_gate[l].reshape(1, -1), qkg, rope_,
                             q_w=q_w, kv_w=kv_w, four_w=four_w, tm=tm)

    def finish_mixer(l, xs_, r, proj, extra, tm):
        qt, k, vt, f, gates = proj
        n = xs_.shape[0]
        tq, tk = _tile(n, 512), _tile(n, 512)
        if extra is None:
            attn = attention(qt, k, vt, None, n_kv=n_kv, tq=tq, tk=tk)
        else:
            bound = (SCORE_BOUND_SLACK * HEAD_DIM * ATTN_SCALE * LOG2E
                     * jnp.max(jnp.abs(q_norm_g[l])) * jnp.max(jnp.abs(k_norm_g[l]))).reshape(1)
            attn = lax.cond(
                bound[0] < MAX_SCORE_BOUND,
                lambda: attention_bounded(qt, k, vt, extra, bound, n_kv=n_kv, tq=tq, tk=tk),
                lambda: attention(qt, k, vt, extra, n_kv=n_kv, tq=tq, tk=tk))
        four = fourier_mix(f)
        return merge(xs_, attn, four, gates, _vec(mod(l, r, 2)), wa_bf, wf_bf, wo_bf, l, tm=tm)

    def channel_mixer(l, xs_, r, fin, tm):
        i = l // 2
        vec = _vec(norm_ffn_g[l], mod(l, r, 3), mod(l, r, 4), mod(l, r, 5), final_norm_g)
        if l % 2 == 0:
            y = dense_ffn(xs_, vec, ffn_w_gate[i].astype(BF16), ffn_w_up[i].astype(BF16),
                          ffn_w_down[i].astype(BF16), tm=tm, tf=512)
            if fin:
                y = final_norm(y, _vec(final_norm_g), tm=tm)
            return y
        return moe_layer(xs_, vec, router_w[i], moe_w_gate[i].astype(BF16), moe_w_up[i].astype(BF16),
                         moe_w_down[i].astype(BF16), final_norm=fin,
                         tm_route=_tile(xs_.shape[0], 512), tm_rows=_tile(xs_.shape[0], 256),
                         tm_group=512)

    for l in range(depth):
        last = l == depth - 1
        tmc = _tile(C, 256)
        proj_c = project(l, xc, 1, None, tmc)
        if not last:
            xc_next = finish_mixer(l, xc, 1, proj_c, None, tmc)
            xc_next = channel_mixer(l, xc_next, 1, False, tmc)
        proj = project(l, xl, 0, rope, _tile(S, 256))
        xl = finish_mixer(l, xl, 0, proj, (proj_c[1], proj_c[2]), _tile(S, 256))
        xl = channel_mixer(l, xl, 0, last, _tile(S, 512))
        if not last:
            xc = xc_next
    return xl[None]
```

```python
import functools
import math

import numpy as np
import jax
import jax.numpy as jnp
from jax import lax
from jax.experimental import pallas as pl
from jax.experimental.pallas import tpu as pltpu

F32 = jnp.float32
BF16 = jnp.bfloat16

HEAD_DIM = 128
GRID_W = 64
ROPE_THETA = 10000.0
ROPE_FREQS = HEAD_DIM // 4
ATTN_SCALE = HEAD_DIM ** -0.5
N_FOURIER_GROUPS = 4
TOP_K = 2
EPS = 1e-6

LANES = 128
SUBLANES = 8
VMEM_LIMIT_BYTES = 58 * 1024 * 1024
NEG_INF = -1e30
LOG2E = math.log2(math.e)
BF16_SUBLANES = 16
VT_ROWS = HEAD_DIM + BF16_SUBLANES
SCORE_BOUND_SLACK = 1.02
ROW_DMA_UNROLL = 8
MAX_SCORE_BOUND = 60.0


def _cparams(*sem):
    return pltpu.CompilerParams(dimension_semantics=sem, vmem_limit_bytes=VMEM_LIMIT_BYTES)


def _norm_mod(x, g, shift, scale):
    ms = jnp.mean(x * x, axis=-1, keepdims=True)
    return (x * lax.rsqrt(ms + EPS) * g) * (1.0 + scale) + shift


def _silu(x):
    return x * jax.nn.sigmoid(x)


def _ada_kernel(cond_ref, w_ref, b_ref, o_ref):
    s = _silu(cond_ref[...]).astype(BF16)
    w = w_ref[0].astype(BF16)
    o_ref[0] = jnp.dot(s, w, preferred_element_type=F32) + b_ref[0]


def ada_modulation(cond, ada_w, ada_b, tn=1024):
    L, D, N = ada_w.shape
    return pl.pallas_call(
        _ada_kernel,
        grid=(L, N // tn),
        in_specs=[
            pl.BlockSpec((SUBLANES, D), lambda l, j: (0, 0)),
            pl.BlockSpec((1, D, tn), lambda l, j: (l, 0, j)),
            pl.BlockSpec((1, 1, tn), lambda l, j: (l, 0, j)),
        ],
        out_specs=pl.BlockSpec((1, SUBLANES, tn), lambda l, j: (l, 0, j)),
        out_shape=jax.ShapeDtypeStruct((L, SUBLANES, N), F32),
        compiler_params=_cparams("parallel", "parallel"),
        name="ada_modulation",
    )(cond, ada_w, ada_b.reshape(L, 1, N))


def _inproj_kernel(x_ref, vec_ref, w_ref, bg_ref, qkg_ref, cos_ref, sa_ref, sb_ref,
                   qt_ref, k_ref, vt_ref, f_ref, gates_ref, hs_ref, *, q_w, kv_w, four_w, tn, use_rope):
    hs_ref[...] = _norm_mod(x_ref[...], vec_ref[0:1], vec_ref[1:2], vec_ref[2:3]).astype(BF16)
    n_tiles = w_ref.shape[1] // tn
    gate0 = q_w + 2 * kv_w + four_w

    def matmul(j):
        return jnp.dot(hs_ref[...], w_ref[:, j * tn:(j + 1) * tn], preferred_element_type=F32)

    def head_norm(t, g, scale):
        ms = jnp.mean(t * t, axis=-1, keepdims=True)
        y = t * lax.rsqrt(ms + EPS) * g
        if use_rope:
            y = (y * cos_ref[...] + pltpu.roll(y, HEAD_DIM - ROPE_FREQS, 1) * sa_ref[...]
                 + pltpu.roll(y, ROPE_FREQS, 1) * sb_ref[...])
        return y * scale if scale != 1.0 else y

    def epilogue(j, acc):
        for c in range(0, tn, HEAD_DIM):
            col = j * tn + c
            t = acc[:, c:c + HEAD_DIM]
            if col < q_w:
                qt_ref[col:col + HEAD_DIM, :] = head_norm(t, qkg_ref[0:1], ATTN_SCALE * LOG2E).T.astype(BF16)
            elif col < q_w + kv_w:
                k_ref[:, col - q_w:col - q_w + HEAD_DIM] = head_norm(t, qkg_ref[1:2], 1.0).astype(BF16)
            elif col < q_w + 2 * kv_w:
                r0 = (col - q_w - kv_w) // HEAD_DIM * VT_ROWS
                vt_ref[r0:r0 + HEAD_DIM, :] = t.T.astype(BF16)
                vt_ref[r0 + HEAD_DIM:r0 + VT_ROWS, :] = jnp.ones((VT_ROWS - HEAD_DIM, vt_ref.shape[1]), BF16)
            elif col < gate0:
                f_ref[:, col - gate0 + four_w:col - gate0 + four_w + HEAD_DIM] = t
            else:
                g0 = col - gate0
                gates_ref[:, g0:g0 + HEAD_DIM] = jax.nn.sigmoid(t + bg_ref[:, g0:g0 + HEAD_DIM]).astype(BF16)

    acc = matmul(0)
    for j in range(n_tiles):
        nxt = matmul(j + 1) if j + 1 < n_tiles else None
        epilogue(j, acc)
        acc = nxt


def in_projection(x, vec, w_in, layer, b_gate, qkg, rope, *, q_w, kv_w, four_w, tm, tn=1024):
    R, D = x.shape
    n_cols = w_in.shape[2]
    n_kv = kv_w // HEAD_DIM
    assert n_cols == q_w + 2 * kv_w + four_w + 2 * D and n_cols % tn == 0 and R % tm == 0 and tm % LANES == 0
    use_rope = rope is not None
    if rope is None:
        rope = (jnp.zeros((SUBLANES, HEAD_DIM), F32),) * 3
        rope_spec = pl.BlockSpec((SUBLANES, HEAD_DIM), lambda i: (0, 0))
    else:
        rope_spec = pl.BlockSpec((tm, HEAD_DIM), lambda i: (i, 0))
    kern = functools.partial(_inproj_kernel, q_w=q_w, kv_w=kv_w, four_w=four_w, tn=tn, use_rope=use_rope)
    return pl.pallas_call(
        kern,
        grid=(R // tm,),
        in_specs=[
            pl.BlockSpec((tm, D), lambda i: (i, 0)),
            pl.BlockSpec((SUBLANES, D), lambda i: (0, 0)),
            pl.BlockSpec((None, D, n_cols), lambda i: (layer, 0, 0), pipeline_mode=pl.Buffered(1)),
            pl.BlockSpec((1, 2 * D), lambda i: (0, 0)),
            pl.BlockSpec((SUBLANES, HEAD_DIM), lambda i: (0, 0)),
            rope_spec, rope_spec, rope_spec,
        ],
        out_specs=[
            pl.BlockSpec((q_w, tm), lambda i: (0, i)),
            pl.BlockSpec((tm, kv_w), lambda i: (i, 0)),
            pl.BlockSpec((n_kv * VT_ROWS, tm), lambda i: (0, i)),
            pl.BlockSpec((tm, four_w), lambda i: (i, 0)),
            pl.BlockSpec((tm, 2 * D), lambda i: (i, 0)),
        ],
        out_shape=[
            jax.ShapeDtypeStruct((q_w, R), BF16),
            jax.ShapeDtypeStruct((R, kv_w), BF16),
            jax.ShapeDtypeStruct((n_kv * VT_ROWS, R), BF16),
            jax.ShapeDtypeStruct((R, four_w), F32),
            jax.ShapeDtypeStruct((R, 2 * D), BF16),
        ],
        scratch_shapes=[pltpu.VMEM((tm, D), BF16)],
        compiler_params=_cparams("parallel"),
        name="in_projection",
    )(x, vec, w_in, b_gate, qkg, *rope)


def _attn_kernel(*refs, group, tk, n_chunks, has_extra):
    if has_extra:
        qt_ref, k_ref, vt_ref, kx_ref, vxt_ref, o_ref, m_ref, acc_ref, sa_ref, sb_ref, sx_ref = refs
    else:
        qt_ref, k_ref, vt_ref, o_ref, m_ref, acc_ref, sa_ref, sb_ref = refs
    m_ref[...] = jnp.full(m_ref.shape, NEG_INF, F32)
    acc_ref[...] = jnp.zeros(acc_ref.shape, F32)

    def scores(k, s_ref):
        for g in range(group):
            s_ref[g] = jnp.dot(k, qt_ref[g * HEAD_DIM:(g + 1) * HEAD_DIM, :], preferred_element_type=F32)

    def softmax_pv(vt, s_ref):
        for g in range(group):
            s = s_ref[g]
            m_prev = m_ref[g]
            m_new = jnp.maximum(m_prev, jnp.max(s, axis=0, keepdims=True))
            alpha = jnp.exp2(m_prev - m_new)
            p = jnp.exp2(s - m_new).astype(BF16)
            acc_ref[g] = alpha * acc_ref[g] + jnp.dot(vt, p, preferred_element_type=F32)
            m_ref[g] = m_new

    def k_chunk(c):
        return k_ref[pl.ds(pl.multiple_of(c * tk, tk), tk), :]

    def vt_chunk(c):
        return vt_ref[:, pl.ds(pl.multiple_of(c * tk, tk), tk)]

    n_pairs = (n_chunks - 1) // 2
    scores(k_chunk(0), sa_ref)

    def pair(i, carry):
        c = 2 * i
        scores(k_chunk(c + 1), sb_ref)
        softmax_pv(vt_chunk(c), sa_ref)
        scores(k_chunk(c + 2), sa_ref)
        softmax_pv(vt_chunk(c + 1), sb_ref)
        return carry

    lax.fori_loop(0, n_pairs, pair, 0)
    c = 2 * n_pairs
    if n_chunks - c == 2:
        scores(k_chunk(c + 1), sb_ref)
    softmax_pv(vt_chunk(c), sa_ref)
    if has_extra:
        scores(kx_ref[...], sx_ref)
    if n_chunks - c == 2:
        softmax_pv(vt_chunk(c + 1), sb_ref)
    if has_extra:
        softmax_pv(vxt_ref[...], sx_ref)
    for g in range(group):
        o = acc_ref[g, :HEAD_DIM, :] / acc_ref[g, HEAD_DIM:HEAD_DIM + 1, :]
        o_ref[:, g * HEAD_DIM:(g + 1) * HEAD_DIM] = o.T.astype(BF16)


def attention(qt, k, vt, extra, *, n_kv, tq, tk):
    q_w, R = qt.shape
    group = q_w // HEAD_DIM // n_kv
    gw = group * HEAD_DIM
    assert R % tq == 0 and R % tk == 0
    in_specs = [
        pl.BlockSpec((gw, tq), lambda h, i: (h, i)),
        pl.BlockSpec((R, HEAD_DIM), lambda h, i: (0, h)),
        pl.BlockSpec((VT_ROWS, R), lambda h, i: (h, 0)),
    ]
    args = [qt, k, vt]
    scratch = [
        pltpu.VMEM((group, 1, tq), F32),
        pltpu.VMEM((group, VT_ROWS, tq), F32),
        pltpu.VMEM((group, tk, tq), F32),
        pltpu.VMEM((group, tk, tq), F32),
    ]
    if extra is not None:
        rx = extra[0].shape[0]
        in_specs += [
            pl.BlockSpec((rx, HEAD_DIM), lambda h, i: (0, h)),
            pl.BlockSpec((VT_ROWS, rx), lambda h, i: (h, 0)),
        ]
        args += list(extra)
        scratch.append(pltpu.VMEM((group, rx, tq), F32))
    kern = functools.partial(_attn_kernel, group=group, tk=tk, n_chunks=R // tk, has_extra=extra is not None)
    return pl.pallas_call(
        kern,
        grid=(n_kv, R // tq),
        in_specs=in_specs,
        out_specs=pl.BlockSpec((tq, gw), lambda h, i: (i, h)),
        out_shape=jax.ShapeDtypeStruct((R, q_w), BF16),
        scratch_shapes=scratch,
        compiler_params=_cparams("parallel", "parallel"),
        name="attention",
    )(*args)


def _attn_bounded_kernel(bound_ref, qt_ref, k_ref, vt_ref, kx_ref, vxt_ref, o_ref, acc_ref, sa_ref, sb_ref, sx_ref,
                         *, group, tk, n_chunks):
    acc_ref[...] = jnp.zeros(acc_ref.shape, F32)
    bound = bound_ref[0]

    def scores(k, s_ref, g):
        s_ref[g] = jnp.dot(k, qt_ref[g * HEAD_DIM:(g + 1) * HEAD_DIM, :], preferred_element_type=F32)

    def exp_pv(vt, s_ref, g):
        p = jnp.exp2(s_ref[g] - bound).astype(BF16)
        acc_ref[g] += jnp.dot(vt, p, preferred_element_type=F32)

    def k_chunk(c):
        return k_ref[pl.ds(pl.multiple_of(c * tk, tk), tk), :]

    def vt_chunk(c):
        return vt_ref[:, pl.ds(pl.multiple_of(c * tk, tk), tk)]

    def step(k_next, s_next, vt_cur, s_cur):
        for g in range(group):
            if k_next is not None:
                scores(k_next, s_next, g)
            exp_pv(vt_cur, s_cur, g)

    n_pairs = (n_chunks - 1) // 2
    for g in range(group):
        scores(k_chunk(0), sa_ref, g)

    def pair(i, carry):
        c = 2 * i
        step(k_chunk(c + 1), sb_ref, vt_chunk(c), sa_ref)
        step(k_chunk(c + 2), sa_ref, vt_chunk(c + 1), sb_ref)
        return carry

    lax.fori_loop(0, n_pairs, pair, 0)
    c = 2 * n_pairs
    if n_chunks - c == 2:
        step(k_chunk(c + 1), sb_ref, vt_chunk(c), sa_ref)
        step(kx_ref[...], sx_ref, vt_chunk(c + 1), sb_ref)
    else:
        step(kx_ref[...], sx_ref, vt_chunk(c), sa_ref)
    step(None, None, vxt_ref[...], sx_ref)
    for g in range(group):
        o = acc_ref[g, :HEAD_DIM, :] / acc_ref[g, HEAD_DIM:HEAD_DIM + 1, :]
        o_ref[:, g * HEAD_DIM:(g + 1) * HEAD_DIM] = o.T.astype(BF16)


def attention_bounded(qt, k, vt, extra, bound, *, n_kv, tq, tk):
    q_w, R = qt.shape
    group = q_w // HEAD_DIM // n_kv
    gw = group * HEAD_DIM
    rx = extra[0].shape[0]
    assert R % tq == 0 and R % tk == 0
    kern = functools.partial(_attn_bounded_kernel, group=group, tk=tk, n_chunks=R // tk)
    return pl.pallas_call(
        kern,
        grid=(n_kv, R // tq),
        in_specs=[
            pl.BlockSpec(memory_space=pltpu.SMEM),
            pl.BlockSpec((gw, tq), lambda h, i: (h, i)),
            pl.BlockSpec((R, HEAD_DIM), lambda h, i: (0, h)),
            pl.BlockSpec((VT_ROWS, R), lambda h, i: (h, 0)),
            pl.BlockSpec((rx, HEAD_DIM), lambda h, i: (0, h)),
            pl.BlockSpec((VT_ROWS, rx), lambda h, i: (h, 0)),
        ],
        out_specs=pl.BlockSpec((tq, gw), lambda h, i: (i, h)),
        out_shape=jax.ShapeDtypeStruct((R, q_w), BF16),
        scratch_shapes=[pltpu.VMEM((group, VT_ROWS, tq), F32), pltpu.VMEM((group, tk, tq), F32),
                        pltpu.VMEM((group, tk, tq), F32), pltpu.VMEM((group, rx, tq), F32)],
        compiler_params=_cparams("parallel", "parallel"),
        name="attention_bounded",
    )(bound, qt, k, vt, *extra)


def _dft_mats(n, scale, repeat=1):
    k = np.arange(n)
    ang = 2.0 * np.pi * ((k[:, None] * k[None, :]) % n) / n
    eye = np.eye(repeat)
    return tuple(jnp.asarray(np.kron(m * scale, eye), BF16) for m in (np.cos(ang), np.sin(ang)))


def _four_small_kernel(u_ref, cn_ref, sn_ref, cc_ref, sc_ref, o_ref, *, gc):
    u = u_ref[...].astype(BF16)
    for g in range(u.shape[1] // gc):
        ug = u[:, g * gc:(g + 1) * gc]
        a = jnp.dot(ug, cc_ref[...], preferred_element_type=F32).astype(BF16)
        b = jnp.dot(ug, sc_ref[...], preferred_element_type=F32).astype(BF16)
        o = (jnp.dot(cn_ref[...], a, preferred_element_type=F32)
             - jnp.dot(sn_ref[...], b, preferred_element_type=F32))
        o_ref[:, g * gc:(g + 1) * gc] = o.astype(BF16)


def fourier_mix_small(u):
    n, c = u.shape
    gc = c // N_FOURIER_GROUPS
    cn, sn = _dft_mats(n, n ** -0.5)
    cc, sc = _dft_mats(gc, gc ** -0.5)
    full = lambda a: pl.BlockSpec(a.shape, lambda: (0,) * a.ndim)
    return pl.pallas_call(
        functools.partial(_four_small_kernel, gc=gc),
        in_specs=[full(u), full(cn), full(sn), full(cc), full(sc)],
        out_specs=pl.BlockSpec((n, c), lambda: (0, 0)),
        out_shape=jax.ShapeDtypeStruct((n, c), BF16),
        compiler_params=pltpu.CompilerParams(vmem_limit_bytes=VMEM_LIMIT_BYTES),
        name="fourier_small",
    )(u, cn, sn, cc, sc)


def _four_stage1_kernel(u_ref, ca_ref, sa_ref, cw_ref, sw_ref, yr_ref, yi_ref, *, tb):
    na, _, c = u_ref.shape
    u = u_ref[...].reshape(na * tb, c).astype(BF16)
    yr = jnp.dot(ca_ref[...], u, preferred_element_type=F32)
    yi = -jnp.dot(sa_ref[...], u, preferred_element_type=F32)
    cw = cw_ref[...].reshape(na * tb, LANES)
    sw = sw_ref[...].reshape(na * tb, LANES)
    for l in range(c // LANES):
        sl = slice(l * LANES, (l + 1) * LANES)
        r, i = yr[:, sl], yi[:, sl]
        yr_ref[:, :, sl] = (r * cw + i * sw).reshape(na, tb, LANES)
        yi_ref[:, :, sl] = (i * cw - r * sw).reshape(na, tb, LANES)


def _four_stage2_kernel(yr_ref, yi_ref, cb_ref, sb_ref, cc_ref, sc_ref, o_ref, *, nb, c, gc, tk1):
    for t in range(tk1):
        yr = yr_ref[t * nb:(t + 1) * nb, :].astype(BF16)
        yi = yi_ref[t * nb:(t + 1) * nb, :].astype(BF16)
        pr = (jnp.dot(cb_ref[...], yr, preferred_element_type=F32)
              + jnp.dot(sb_ref[...], yi, preferred_element_type=F32)).astype(BF16)
        pi = (jnp.dot(cb_ref[...], yi, preferred_element_type=F32)
              - jnp.dot(sb_ref[...], yr, preferred_element_type=F32)).astype(BF16)
        for g in range(c // gc):
            sl = slice(g * gc, (g + 1) * gc)
            o = (jnp.dot(pr[:, sl], cc_ref[...], preferred_element_type=F32)
                 + jnp.dot(pi[:, sl], sc_ref[...], preferred_element_type=F32))
            o_ref[:, t, g * gc:(g + 1) * gc] = o


def fourier_mix_large(u, *, na=128, tb=SUBLANES, tk1=SUBLANES):
    n, c = u.shape
    nb = n // na
    assert na * nb == n and nb % tb == 0 and na % tk1 == 0 and nb % SUBLANES == 0
    gc = c // N_FOURIER_GROUPS
    ca_k, sa_k = _dft_mats(na, na ** -0.5, repeat=tb)
    cb, sb = _dft_mats(nb, nb ** -0.5)
    cc, sc = _dft_mats(gc, gc ** -0.5)
    k1 = lax.broadcasted_iota(jnp.int32, (na, nb, LANES), 0)
    b = lax.broadcasted_iota(jnp.int32, (na, nb, LANES), 1)
    ang = ((k1 * b) % n).astype(F32) * (2.0 * math.pi / n)
    cw, sw = jnp.cos(ang), jnp.sin(ang)
    const = lambda a: pl.BlockSpec(a.shape, lambda i: (0,) * a.ndim)
    yr, yi = pl.pallas_call(
        functools.partial(_four_stage1_kernel, tb=tb),
        grid=(nb // tb,),
        in_specs=[
            pl.BlockSpec((na, tb, c), lambda i: (0, i, 0)),
            const(ca_k), const(sa_k),
            pl.BlockSpec((na, tb, LANES), lambda i: (0, i, 0)),
            pl.BlockSpec((na, tb, LANES), lambda i: (0, i, 0)),
        ],
        out_specs=[pl.BlockSpec((na, tb, c), lambda i: (0, i, 0))] * 2,
        out_shape=[jax.ShapeDtypeStruct((na, nb, c), F32)] * 2,
        compiler_params=_cparams("parallel"),
        name="fourier_stage1",
    )(u.reshape(na, nb, c), ca_k, sa_k, cw, sw)
    out = pl.pallas_call(
        functools.partial(_four_stage2_kernel, nb=nb, c=c, gc=gc, tk1=tk1),
        grid=(na // tk1,),
        in_specs=[
            pl.BlockSpec((tk1 * nb, c), lambda i: (i, 0)),
            pl.BlockSpec((tk1 * nb, c), lambda i: (i, 0)),
            const(cb), const(sb), const(cc), const(sc),
        ],
        out_specs=pl.BlockSpec((nb, tk1, c), lambda i: (0, i, 0)),
        out_shape=jax.ShapeDtypeStruct((nb, na, c), F32),
        compiler_params=_cparams("parallel"),
        name="fourier_stage2",
    )(yr.reshape(na * nb, c), yi.reshape(na * nb, c), cb, sb, cc, sc)
    return out.reshape(n, c)


def fourier_mix(u):
    n = u.shape[0]
    if n % (128 * SUBLANES) == 0 and n >= 128 * 16:
        return fourier_mix_large(u)
    return fourier_mix_small(u)


def _merge_kernel(x_ref, attn_ref, four_ref, gates_ref, vec_ref, wa_ref, wf_ref, wo_ref, o_ref):
    d = x_ref.shape[1]
    a = jnp.dot(attn_ref[...], wa_ref[...], preferred_element_type=F32)
    f = jnp.dot(four_ref[...].astype(BF16), wf_ref[...], preferred_element_type=F32)
    y = gates_ref[:, :d].astype(F32) * a + gates_ref[:, d:].astype(F32) * f
    y = jnp.dot(y.astype(BF16), wo_ref[...], preferred_element_type=F32)
    o_ref[...] = x_ref[...] + vec_ref[0:1] * y


def merge(x, attn, four, gates, vec, wa, wf, wo, layer, *, tm):
    R, D = x.shape
    row = lambda w: pl.BlockSpec((tm, w), lambda i: (i, 0))
    const = lambda a: pl.BlockSpec((None,) + a.shape[1:], lambda i: (layer, 0, 0), pipeline_mode=pl.Buffered(1))
    return pl.pallas_call(
        _merge_kernel,
        grid=(R // tm,),
        in_specs=[row(D), row(attn.shape[1]), row(four.shape[1]), row(gates.shape[1]),
                  pl.BlockSpec((SUBLANES, D), lambda i: (0, 0)), const(wa), const(wf), const(wo)],
        out_specs=row(D),
        out_shape=jax.ShapeDtypeStruct((R, D), F32),
        compiler_params=_cparams("parallel"),
        name="merge",
    )(x, attn, four, gates, vec, wa, wf, wo)


def _ffn_kernel(x_ref, vec_ref, wg_ref, wu_ref, wd_ref, o_ref, hs_ref, acc_ref):
    j = pl.program_id(1)

    @pl.when(j == 0)
    def _():
        hs_ref[...] = _norm_mod(x_ref[...], vec_ref[0:1], vec_ref[1:2], vec_ref[2:3]).astype(BF16)
        acc_ref[...] = jnp.zeros(acc_ref.shape, F32)

    h = hs_ref[...]
    g = jnp.dot(h, wg_ref[...], preferred_element_type=F32)
    u = jnp.dot(h, wu_ref[...], preferred_element_type=F32)
    acc_ref[...] += jnp.dot((_silu(g) * u).astype(BF16), wd_ref[...], preferred_element_type=F32)

    @pl.when(j == pl.num_programs(1) - 1)
    def _():
        o_ref[...] = x_ref[...] + vec_ref[3:4] * acc_ref[...]


def dense_ffn(x, vec, wg, wu, wd, *, tm, tf):
    R, D = x.shape
    F = wg.shape[1]
    assert R % tm == 0 and F % tf == 0
    return pl.pallas_call(
        _ffn_kernel,
        grid=(R // tm, F // tf),
        in_specs=[
            pl.BlockSpec((tm, D), lambda i, j: (i, 0)),
            pl.BlockSpec((SUBLANES, D), lambda i, j: (0, 0)),
            pl.BlockSpec((D, tf), lambda i, j: (0, j)),
            pl.BlockSpec((D, tf), lambda i, j: (0, j)),
            pl.BlockSpec((tf, D), lambda i, j: (j, 0)),
        ],
        out_specs=pl.BlockSpec((tm, D), lambda i, j: (i, 0)),
        out_shape=jax.ShapeDtypeStruct((R, D), F32),
        scratch_shapes=[pltpu.VMEM((tm, D), BF16), pltpu.VMEM((tm, D), F32)],
        compiler_params=_cparams("parallel", "arbitrary"),
        name="dense_ffn",
    )(x, vec, wg, wu, wd)


def _router_kernel(x_ref, vec_ref, rhi_ref, rlo_ref, h_ref, meta_ref, cnt_ref, run_ref, *, n_exp):
    i = pl.program_id(0)

    @pl.when(i == 0)
    def _():
        run_ref[...] = jnp.zeros(run_ref.shape, F32)

    h = _norm_mod(x_ref[...], vec_ref[0:1], vec_ref[1:2], vec_ref[2:3])
    h_ref[...] = h
    tm = h.shape[0]
    h_hi = h.astype(BF16)
    h_lo = (h - h_hi.astype(F32)).astype(BF16)
    logits = (jnp.dot(h_hi, rhi_ref[...], preferred_element_type=F32)
              + jnp.dot(h_lo, rhi_ref[...], preferred_element_type=F32)
              + jnp.dot(h_hi, rlo_ref[...], preferred_element_type=F32))
    lane = lax.broadcasted_iota(jnp.int32, (tm, LANES), 1)
    lg = jnp.where(lane < n_exp, logits, NEG_INF)
    m1 = jnp.max(lg, axis=-1, keepdims=True)
    i1 = jnp.min(jnp.where(lg == m1, lane, LANES), axis=-1, keepdims=True)
    lg = jnp.where(lane == i1, NEG_INF, lg)
    m2 = jnp.max(lg, axis=-1, keepdims=True)
    i2 = jnp.min(jnp.where(lg == m2, lane, LANES), axis=-1, keepdims=True)
    e2 = jnp.exp(m2 - m1)
    w1 = 1.0 / (1.0 + e2)
    w2 = e2 / (1.0 + e2)

    hit1 = lane == i1
    hit2 = lane == i2
    onehot = (hit1 | hit2).astype(F32)
    r = lax.broadcasted_iota(jnp.int32, (tm, tm), 0)
    cidx = lax.broadcasted_iota(jnp.int32, (tm, tm), 1)
    ltri = (cidx < r).astype(BF16)
    before = jnp.dot(ltri, onehot.astype(BF16), preferred_element_type=F32) + run_ref[0:1]
    rank1 = jnp.sum(jnp.where(hit1, before, 0.0), axis=-1, keepdims=True)
    rank2 = jnp.sum(jnp.where(hit2, before, 0.0), axis=-1, keepdims=True)
    run_ref[0:1] = run_ref[0:1] + jnp.sum(onehot, axis=0, keepdims=True)

    meta = jnp.where(lane == 0, i1.astype(F32), 0.0)
    meta = jnp.where(lane == 1, i2.astype(F32), meta)
    meta = jnp.where(lane == 2, w1, meta)
    meta = jnp.where(lane == 3, w2, meta)
    meta = jnp.where(lane == 4, rank1, meta)
    meta = jnp.where(lane == 5, rank2, meta)
    meta_ref[...] = meta
    cnt_ref[...] = run_ref[...]


def moe_router(x, vec, router_w, *, tm):
    R, D = x.shape
    n_exp = router_w.shape[1]
    rw = jnp.zeros((D, LANES), F32).at[:, :n_exp].set(router_w)
    rhi = rw.astype(BF16)
    rlo = (rw - rhi.astype(F32)).astype(BF16)
    return pl.pallas_call(
        functools.partial(_router_kernel, n_exp=n_exp),
        grid=(R // tm,),
        in_specs=[
            pl.BlockSpec((tm, D), lambda i: (i, 0)),
            pl.BlockSpec((SUBLANES, D), lambda i: (0, 0)),
            pl.BlockSpec(rhi.shape, lambda i: (0, 0)),
            pl.BlockSpec(rlo.shape, lambda i: (0, 0)),
        ],
        out_specs=[
            pl.BlockSpec((tm, D), lambda i: (i, 0)),
            pl.BlockSpec((tm, LANES), lambda i: (i, 0)),
            pl.BlockSpec((SUBLANES, LANES), lambda i: (0, 0)),
        ],
        out_shape=[
            jax.ShapeDtypeStruct((R, D), F32),
            jax.ShapeDtypeStruct((R, LANES), F32),
            jax.ShapeDtypeStruct((SUBLANES, LANES), F32),
        ],
        scratch_shapes=[pltpu.VMEM((SUBLANES, LANES), F32)],
        compiler_params=_cparams("arbitrary"),
        name="moe_router",
    )(x, vec, rhi, rlo)


def _row_copy(src_ref, s, dst_ref, d, sem):
    return pltpu.make_async_copy(src_ref.at[pl.ds(s, 1)], dst_ref.at[pl.ds(d, 1)], sem)


def _dispatch_kernel(dest_ref, ends_ref, h_ref, xs_ref, zero_ref, sem, zero_sem, *, tm_group):
    tm = h_ref.shape[0]
    base = pl.program_id(0) * tm

    @pl.when(pl.program_id(0) == 0)
    def _():
        zero_ref[...] = jnp.zeros(zero_ref.shape, F32)

        def fill(e):
            prev = ends_ref[jnp.maximum(e - 1, 0)]
            start = pl.multiple_of(ends_ref[e] - tm_group, tm_group)
            nonempty = (ends_ref[e] > prev) | ((e == 0) & (ends_ref[e] > 0))
            return nonempty, pltpu.make_async_copy(zero_ref, xs_ref.at[pl.ds(start, tm_group)], zero_sem)

        for e in range(ends_ref.shape[0]):
            nonempty, copy = fill(e)
            pl.when(nonempty)(copy.start)
        for e in range(ends_ref.shape[0]):
            nonempty, copy = fill(e)
            pl.when(nonempty)(copy.wait)

        def fill_tail(t, c):
            copy = pltpu.make_async_copy(
                zero_ref, xs_ref.at[pl.ds(pl.multiple_of(t * tm_group, tm_group), tm_group)], zero_sem)
            copy.start()
            copy.wait()
            return c

        n_used = ends_ref[ends_ref.shape[0] - 1] // tm_group
        lax.fori_loop(n_used, xs_ref.shape[0] // tm_group, fill_tail, 0)

    def start(t, c):
        for s in range(TOP_K):
            _row_copy(h_ref, t, xs_ref, dest_ref[TOP_K * (base + t) + s], sem).start()
        return c

    def wait(t, c):
        for s in range(TOP_K):
            _row_copy(h_ref, t, xs_ref, dest_ref[TOP_K * (base + t) + s], sem).wait()
        return c

    lax.fori_loop(0, tm, start, 0, unroll=ROW_DMA_UNROLL)
    lax.fori_loop(0, tm, wait, 0, unroll=ROW_DMA_UNROLL)


def moe_dispatch(h, dest, ends, n_rows, *, tm, tm_group):
    R, D = h.shape
    return pl.pallas_call(
        functools.partial(_dispatch_kernel, tm_group=tm_group),
        grid_spec=pltpu.PrefetchScalarGridSpec(
            num_scalar_prefetch=2,
            grid=(R // tm,),
            in_specs=[pl.BlockSpec((tm, D), lambda i, dest, ends: (i, 0))],
            out_specs=pl.BlockSpec(memory_space=pl.ANY),
            scratch_shapes=[pltpu.VMEM((tm_group, D), F32), pltpu.SemaphoreType.DMA(()),
                            pltpu.SemaphoreType.DMA(())],
        ),
        out_shape=jax.ShapeDtypeStruct((n_rows, D), F32),
        compiler_params=_cparams("arbitrary"),
        name="moe_dispatch",
    )(dest, ends, h)


def _gffn_up_kernel(te_ref, nv_ref, xs_ref, wg_ref, wu_ref, a_ref, *, tc):
    valid = pl.program_id(0) < nv_ref[0]

    @pl.when(valid)
    def _():
        h = xs_ref[...].astype(BF16)
        n_cols = a_ref.shape[1]
        for c0 in range(0, n_cols, tc):
            c1 = min(c0 + tc, n_cols)
            g = jnp.dot(h, wg_ref[0, :, c0:c1], preferred_element_type=F32)
            u = jnp.dot(h, wu_ref[0, :, c0:c1], preferred_element_type=F32)
            a_ref[:, c0:c1] = (_silu(g) * u).astype(BF16)

    @pl.when(jnp.logical_not(valid))
    def _():
        a_ref[...] = jnp.zeros(a_ref.shape, BF16)


def _gffn_down_kernel(te_ref, nv_ref, a_ref, wd_ref, ys_ref):
    valid = pl.program_id(0) < nv_ref[0]

    @pl.when(valid)
    def _():
        ys_ref[...] = jnp.dot(a_ref[...], wd_ref[0], preferred_element_type=F32)

    @pl.when(jnp.logical_not(valid))
    def _():
        ys_ref[...] = jnp.zeros(ys_ref.shape, F32)


def moe_grouped_ffn(xs, tile_expert, n_valid, wg, wu, wd, *, tm, tc=512):
    n_rows, D = xs.shape
    F = wg.shape[2]
    assert n_rows % tm == 0 and F % LANES == 0
    expert_w = lambda shape: pl.BlockSpec(shape, lambda t, te, nv: (te[t], 0, 0), pipeline_mode=pl.Buffered(1))
    a = pl.pallas_call(
        functools.partial(_gffn_up_kernel, tc=tc),
        grid_spec=pltpu.PrefetchScalarGridSpec(
            num_scalar_prefetch=2,
            grid=(n_rows // tm,),
            in_specs=[
                pl.BlockSpec((tm, D), lambda t, te, nv: (jnp.minimum(t, jnp.maximum(nv[0] - 1, 0)), 0)),
                expert_w((1, D, F)),
                expert_w((1, D, F)),
            ],
            out_specs=pl.BlockSpec((tm, F), lambda t, te, nv: (t, 0)),
        ),
        out_shape=jax.ShapeDtypeStruct((n_rows, F), BF16),
        compiler_params=_cparams("parallel"),
        name="moe_ffn_up",
    )(tile_expert, n_valid, xs, wg, wu)
    return pl.pallas_call(
        _gffn_down_kernel,
        grid_spec=pltpu.PrefetchScalarGridSpec(
            num_scalar_prefetch=2,
            grid=(n_rows // tm,),
            in_specs=[
                pl.BlockSpec((tm, F), lambda t, te, nv: (t, 0)),
                pl.BlockSpec((1, F, D), lambda t, te, nv: (te[t], 0, 0)),
            ],
            out_specs=pl.BlockSpec((tm, D), lambda t, te, nv: (t, 0)),
        ),
        out_shape=jax.ShapeDtypeStruct((n_rows, D), F32),
        compiler_params=_cparams("parallel"),
        name="moe_ffn_down",
    )(tile_expert, n_valid, a, wd)


def _combine_kernel(dest_ref, x_ref, meta_ref, vec_ref, ys_ref, o_ref, b0_ref, b1_ref, sem, *, final_norm):
    tm = x_ref.shape[0]
    base = pl.program_id(0) * tm
    bufs = (b0_ref, b1_ref)

    def start(t, c):
        for s in range(TOP_K):
            _row_copy(ys_ref, dest_ref[TOP_K * (base + t) + s], bufs[s], t, sem).start()
        return c

    def wait(t, c):
        for s in range(TOP_K):
            _row_copy(ys_ref, dest_ref[TOP_K * (base + t) + s], bufs[s], t, sem).wait()
        return c

    lax.fori_loop(0, tm, start, 0, unroll=ROW_DMA_UNROLL)
    lax.fori_loop(0, tm, wait, 0, unroll=ROW_DMA_UNROLL)
    y = meta_ref[:, 2:3] * b0_ref[...] + meta_ref[:, 3:4] * b1_ref[...]
    out = x_ref[...] + vec_ref[3:4] * y
    if final_norm:
        ms = jnp.mean(out * out, axis=-1, keepdims=True)
        out = out * lax.rsqrt(ms + EPS) * vec_ref[4:5]
    o_ref[...] = out


def moe_combine(x, meta, vec, ys, dest, *, tm, final_norm):
    R, D = x.shape
    return pl.pallas_call(
        functools.partial(_combine_kernel, final_norm=final_norm),
        grid_spec=pltpu.PrefetchScalarGridSpec(
            num_scalar_prefetch=1,
            grid=(R // tm,),
            in_specs=[
                pl.BlockSpec((tm, D), lambda i, dest: (i, 0)),
                pl.BlockSpec((tm, LANES), lambda i, dest: (i, 0)),
                pl.BlockSpec((SUBLANES, D), lambda i, dest: (0, 0)),
                pl.BlockSpec(memory_space=pl.ANY),
            ],
            out_specs=pl.BlockSpec((tm, D), lambda i, dest: (i, 0)),
            scratch_shapes=[pltpu.VMEM((tm, D), F32), pltpu.VMEM((tm, D), F32),
                            pltpu.SemaphoreType.DMA(())],
        ),
        out_shape=jax.ShapeDtypeStruct((R, D), F32),
        compiler_params=_cparams("arbitrary"),
        name="moe_combine",
    )(dest, x, meta, vec, ys)


def moe_layer(x, vec, router_w, wg, wu, wd, *, final_norm, tm_route, tm_rows, tm_group):
    R, D = x.shape
    n_exp = router_w.shape[1]
    h, meta, counts = moe_router(x, vec, router_w, tm=tm_route)
    counts = counts[0, :n_exp].astype(jnp.int32)
    padded = ((counts + tm_group - 1) // tm_group) * tm_group
    ends = jnp.cumsum(padded)
    offsets = ends - padded
    eid = meta[:, 0:TOP_K].astype(jnp.int32)
    rank = meta[:, 4:4 + TOP_K].astype(jnp.int32)
    dest = (offsets[eid] + rank).reshape(-1)
    n_tiles = (TOP_K * R) // tm_group + n_exp
    tile_start = jnp.arange(n_tiles, dtype=jnp.int32) * tm_group
    n_valid = ends[-1] // tm_group
    tile_expert = jnp.sum(tile_start[:, None] >= ends[None, :], axis=1).astype(jnp.int32)
    last_expert = jnp.take(tile_expert, jnp.maximum(n_valid - 1, 0))
    tile_expert = jnp.where(jnp.arange(n_tiles) < n_valid, tile_expert, last_expert)
    xs = moe_dispatch(h, dest, ends.astype(jnp.int32), n_tiles * tm_group, tm=tm_rows, tm_group=tm_group)
    ys = moe_grouped_ffn(xs, tile_expert, n_valid.reshape(1).astype(jnp.int32), wg, wu, wd,
                         tm=tm_group)
    return moe_combine(x, meta, vec, ys, dest, tm=tm_rows, final_norm=final_norm)


def _final_norm_kernel(x_ref, g_ref, o_ref):
    x = x_ref[...]
    ms = jnp.mean(x * x, axis=-1, keepdims=True)
    o_ref[...] = x * lax.rsqrt(ms + EPS) * g_ref[0:1]


def final_norm(x, vec, *, tm):
    R, D = x.shape
    return pl.pallas_call(
        _final_norm_kernel,
        grid=(R // tm,),
        in_specs=[pl.BlockSpec((tm, D), lambda i: (i, 0)), pl.BlockSpec((SUBLANES, D), lambda i: (0, 0))],
        out_specs=pl.BlockSpec((tm, D), lambda i: (i, 0)),
        out_shape=jax.ShapeDtypeStruct((R, D), F32),
        compiler_params=_cparams("parallel"),
        name="final_norm",
    )(x, vec)


def _rope_tables(n_tokens):
    rows = n_tokens // GRID_W
    row = jnp.repeat(jnp.arange(rows, dtype=F32), GRID_W)
    col = jnp.tile(jnp.arange(GRID_W, dtype=F32), rows)
    inv = ROPE_THETA ** (-jnp.arange(ROPE_FREQS, dtype=F32) / ROPE_FREQS)
    ar = row[:, None] * inv
    ac = col[:, None] * inv
    ang = jnp.concatenate([ar, ar, ac, ac], axis=-1)
    cos, sin = jnp.cos(ang), jnp.sin(ang)
    lane = jnp.arange(HEAD_DIM) % (2 * ROPE_FREQS)
    lo = lane < ROPE_FREQS
    return cos, jnp.where(lo, -sin, 0.0), jnp.where(lo, 0.0, sin)


def _vec(*rows):
    d = rows[0].shape[-1]
    out = jnp.zeros((SUBLANES, d), F32)
    for i, r in enumerate(rows):
        out = out.at[i].set(r.reshape(d))
    return out


ROWS_RESIDENT_W = 256
ROWS_STREAMED_W = 512
FFN_COLS = 512
ATTN_Q = 512
ATTN_K = 512
ROWS_ROW_DMA = 512
EXPERT_ROWS = 512


def _tile(n, pref):
    t = pref
    while n % t:
        t //= 2
    return t


def kernel(x, c, ctx, c_ctx, ada_w, ada_b, norm_attn_g, norm_ffn_g, w_in, b_gate, q_norm_g, k_norm_g,
           w_attn_proj, w_four_proj, w_out, ffn_w_gate, ffn_w_up, ffn_w_down, router_w, moe_w_gate,
           moe_w_up, moe_w_down, final_norm_g):
    B, S, D = x.shape
    assert B == 1 and c.shape[0] == 1 and ctx.shape[0] == 1
    depth = ada_w.shape[0]
    q_w = w_attn_proj.shape[1]
    four_w = w_four_proj.shape[1]
    kv_w = (w_in.shape[2] - q_w - four_w - 2 * D) // 2
    n_kv = kv_w // HEAD_DIM

    cond = jnp.zeros((SUBLANES, D), F32).at[0].set(c[0]).at[1].set(c_ctx)
    mods = ada_modulation(cond, ada_w, ada_b)
    rope = _rope_tables(S)
    w_in_bf, wa_bf, wf_bf, wo_bf = (w.astype(BF16) for w in (w_in, w_attn_proj, w_four_proj, w_out))
    xl = x[0]
    xc = ctx[0]
    C = xc.shape[0]

    def mod(l, r, k):
        return mods[l, r, k * D:(k + 1) * D]

    def project(l, xs_, r, rope_, tm):
        vec = _vec(norm_attn_g[l], mod(l, r, 0), mod(l, r, 1))
        qkg = _vec(q_norm_g[l], k_norm_g[l])
        return in_projection(xs_, vec, w_in_bf, l, b_gate[l].reshape(1, -1), qkg, rope_,
                             q_w=q_w, kv_w=kv_w, four_w=four_w, tm=tm)

    def finish_mixer(l, xs_, r, proj, extra, tm):
        qt, k, vt, f, gates = proj
        n = xs_.shape[0]
        tq, tk = _tile(n, ATTN_Q), _tile(n, ATTN_K)
        if extra is None:
            attn = attention(qt, k, vt, None, n_kv=n_kv, tq=tq, tk=tk)
        else:
            bound = (SCORE_BOUND_SLACK * HEAD_DIM * ATTN_SCALE * LOG2E
                     * jnp.max(jnp.abs(q_norm_g[l])) * jnp.max(jnp.abs(k_norm_g[l]))).reshape(1)
            attn = lax.cond(
                bound[0] < MAX_SCORE_BOUND,
                lambda: attention_bounded(qt, k, vt, extra, bound, n_kv=n_kv, tq=tq, tk=tk),
                lambda: attention(qt, k, vt, extra, n_kv=n_kv, tq=tq, tk=tk))
        four = fourier_mix(f)
        return merge(xs_, attn, four, gates, _vec(mod(l, r, 2)), wa_bf, wf_bf, wo_bf, l, tm=tm)

    def channel_mixer(l, xs_, r, fin, tm):
        i = l // 2
        vec = _vec(norm_ffn_g[l], mod(l, r, 3), mod(l, r, 4), mod(l, r, 5), final_norm_g)
        if l % 2 == 0:
            y = dense_ffn(xs_, vec, ffn_w_gate[i].astype(BF16), ffn_w_up[i].astype(BF16),
                          ffn_w_down[i].astype(BF16), tm=tm, tf=FFN_COLS)
            if fin:
                y = final_norm(y, _vec(final_norm_g), tm=tm)
            return y
        return moe_layer(xs_, vec, router_w[i], moe_w_gate[i].astype(BF16), moe_w_up[i].astype(BF16),
                         moe_w_down[i].astype(BF16), final_norm=fin,
                         tm_route=_tile(xs_.shape[0], ROWS_STREAMED_W),
                         tm_rows=_tile(xs_.shape[0], ROWS_ROW_DMA), tm_group=EXPERT_ROWS)

    for l in range(depth):
        last = l == depth - 1
        tmc = _tile(C, ROWS_RESIDENT_W)
        proj_c = project(l, xc, 1, None, tmc)
        if not last:
            xc_next = finish_mixer(l, xc, 1, proj_c, None, tmc)
            xc_next = channel_mixer(l, xc_next, 1, False, tmc)
        proj = project(l, xl, 0, rope, _tile(S, ROWS_RESIDENT_W))
        xl = finish_mixer(l, xl, 0, proj, (proj_c[1], proj_c[2]), _tile(S, ROWS_RESIDENT_W))
        xl = channel_mixer(l, xl, 0, last, _tile(S, ROWS_STREAMED_W))
        if not last:
            xc = xc_next
    return xl[None]
```

```python
import functools
import math

import numpy as np
import jax
import jax.numpy as jnp
from jax import lax
from jax.experimental import pallas as pl
from jax.experimental.pallas import tpu as pltpu

F32 = jnp.float32
BF16 = jnp.bfloat16

HEAD_DIM = 128
GRID_W = 64
ROPE_THETA = 10000.0
ROPE_FREQS = HEAD_DIM // 4
ATTN_SCALE = HEAD_DIM ** -0.5
N_FOURIER_GROUPS = 4
TOP_K = 2
EPS = 1e-6

LANES = 128
SUBLANES = 8
VMEM_LIMIT_BYTES = 58 * 1024 * 1024
NEG_INF = -1e30
LOG2E = math.log2(math.e)
BF16_SUBLANES = 16
VT_ROWS = HEAD_DIM + BF16_SUBLANES
SCORE_BOUND_SLACK = 1.02
ROW_DMA_UNROLL = 8
MAX_SCORE_BOUND = 60.0


def _cparams(*sem):
    return pltpu.CompilerParams(dimension_semantics=sem, vmem_limit_bytes=VMEM_LIMIT_BYTES)


def _norm_mod(x, g, shift, scale):
    ms = jnp.mean(x * x, axis=-1, keepdims=True)
    return (x * lax.rsqrt(ms + EPS) * g) * (1.0 + scale) + shift


def _silu(x):
    return x * jax.nn.sigmoid(x)


def _ada_kernel(cond_ref, w_ref, b_ref, o_ref):
    s = _silu(cond_ref[...]).astype(BF16)
    w = w_ref[0].astype(BF16)
    o_ref[0] = jnp.dot(s, w, preferred_element_type=F32) + b_ref[0]


def ada_modulation(cond, ada_w, ada_b, tn=1024):
    L, D, N = ada_w.shape
    return pl.pallas_call(
        _ada_kernel,
        grid=(L, N // tn),
        in_specs=[
            pl.BlockSpec((SUBLANES, D), lambda l, j: (0, 0)),
            pl.BlockSpec((1, D, tn), lambda l, j: (l, 0, j)),
            pl.BlockSpec((1, 1, tn), lambda l, j: (l, 0, j)),
        ],
        out_specs=pl.BlockSpec((1, SUBLANES, tn), lambda l, j: (l, 0, j)),
        out_shape=jax.ShapeDtypeStruct((L, SUBLANES, N), F32),
        compiler_params=_cparams("parallel", "parallel"),
        name="ada_modulation",
    )(cond, ada_w, ada_b.reshape(L, 1, N))


def _inproj_kernel(x_ref, vec_ref, w_ref, bg_ref, qkg_ref, cos_ref, sa_ref, sb_ref,
                   qt_ref, k_ref, vt_ref, f_ref, gates_ref, hs_ref, *, q_w, kv_w, four_w, tn, use_rope):
    hs_ref[...] = _norm_mod(x_ref[...], vec_ref[0:1], vec_ref[1:2], vec_ref[2:3]).astype(BF16)
    n_tiles = w_ref.shape[1] // tn
    gate0 = q_w + 2 * kv_w + four_w

    def matmul(j):
        return jnp.dot(hs_ref[...], w_ref[:, j * tn:(j + 1) * tn], preferred_element_type=F32)

    def head_norm(t, g, scale):
        ms = jnp.mean(t * t, axis=-1, keepdims=True)
        y = t * lax.rsqrt(ms + EPS) * g
        if use_rope:
            y = (y * cos_ref[...] + pltpu.roll(y, HEAD_DIM - ROPE_FREQS, 1) * sa_ref[...]
                 + pltpu.roll(y, ROPE_FREQS, 1) * sb_ref[...])
        return y * scale if scale != 1.0 else y

    def epilogue(j, acc):
        for c in range(0, tn, HEAD_DIM):
            col = j * tn + c
            t = acc[:, c:c + HEAD_DIM]
            if col < q_w:
                qt_ref[col:col + HEAD_DIM, :] = head_norm(t, qkg_ref[0:1], ATTN_SCALE * LOG2E).T.astype(BF16)
            elif col < q_w + kv_w:
                k_ref[:, col - q_w:col - q_w + HEAD_DIM] = head_norm(t, qkg_ref[1:2], 1.0).astype(BF16)
            elif col < q_w + 2 * kv_w:
                r0 = (col - q_w - kv_w) // HEAD_DIM * VT_ROWS
                vt_ref[r0:r0 + HEAD_DIM, :] = t.T.astype(BF16)
                vt_ref[r0 + HEAD_DIM:r0 + VT_ROWS, :] = jnp.ones((VT_ROWS - HEAD_DIM, vt_ref.shape[1]), BF16)
            elif col < gate0:
                f_ref[:, col - gate0 + four_w:col - gate0 + four_w + HEAD_DIM] = t
            else:
                g0 = col - gate0
                gates_ref[:, g0:g0 + HEAD_DIM] = jax.nn.sigmoid(t + bg_ref[:, g0:g0 + HEAD_DIM]).astype(BF16)

    acc = matmul(0)
    for j in range(n_tiles):
        nxt = matmul(j + 1) if j + 1 < n_tiles else None
        epilogue(j, acc)
        acc = nxt


def in_projection(x, vec, w_in, layer, b_gate, qkg, rope, *, q_w, kv_w, four_w, tm, tn=1024):
    R, D = x.shape
    n_cols = w_in.shape[2]
    n_kv = kv_w // HEAD_DIM
    assert n_cols == q_w + 2 * kv_w + four_w + 2 * D and n_cols % tn == 0 and R % tm == 0 and tm % LANES == 0
    use_rope = rope is not None
    if rope is None:
        rope = (jnp.zeros((SUBLANES, HEAD_DIM), F32),) * 3
        rope_spec = pl.BlockSpec((SUBLANES, HEAD_DIM), lambda i: (0, 0))
    else:
        rope_spec = pl.BlockSpec((tm, HEAD_DIM), lambda i: (i, 0))
    kern = functools.partial(_inproj_kernel, q_w=q_w, kv_w=kv_w, four_w=four_w, tn=tn, use_rope=use_rope)
    return pl.pallas_call(
        kern,
        grid=(R // tm,),
        in_specs=[
            pl.BlockSpec((tm, D), lambda i: (i, 0)),
            pl.BlockSpec((SUBLANES, D), lambda i: (0, 0)),
            pl.BlockSpec((None, D, n_cols), lambda i: (layer, 0, 0), pipeline_mode=pl.Buffered(1)),
            pl.BlockSpec((1, 2 * D), lambda i: (0, 0)),
            pl.BlockSpec((SUBLANES, HEAD_DIM), lambda i: (0, 0)),
            rope_spec, rope_spec, rope_spec,
        ],
        out_specs=[
            pl.BlockSpec((q_w, tm), lambda i: (0, i)),
            pl.BlockSpec((tm, kv_w), lambda i: (i, 0)),
            pl.BlockSpec((n_kv * VT_ROWS, tm), lambda i: (0, i)),
            pl.BlockSpec((tm, four_w), lambda i: (i, 0)),
            pl.BlockSpec((tm, 2 * D), lambda i: (i, 0)),
        ],
        out_shape=[
            jax.ShapeDtypeStruct((q_w, R), BF16),
            jax.ShapeDtypeStruct((R, kv_w), BF16),
            jax.ShapeDtypeStruct((n_kv * VT_ROWS, R), BF16),
            jax.ShapeDtypeStruct((R, four_w), F32),
            jax.ShapeDtypeStruct((R, 2 * D), BF16),
        ],
        scratch_shapes=[pltpu.VMEM((tm, D), BF16)],
        compiler_params=_cparams("parallel"),
        name="in_projection",
    )(x, vec, w_in, b_gate, qkg, *rope)


def _attn_kernel(*refs, group, tk, n_chunks, has_extra):
    if has_extra:
        qt_ref, k_ref, vt_ref, kx_ref, vxt_ref, o_ref, m_ref, acc_ref, sa_ref, sb_ref, sx_ref = refs
    else:
        qt_ref, k_ref, vt_ref, o_ref, m_ref, acc_ref, sa_ref, sb_ref = refs
    m_ref[...] = jnp.full(m_ref.shape, NEG_INF, F32)
    acc_ref[...] = jnp.zeros(acc_ref.shape, F32)

    def scores(k, s_ref):
        for g in range(group):
            s_ref[g] = jnp.dot(k, qt_ref[g * HEAD_DIM:(g + 1) * HEAD_DIM, :], preferred_element_type=F32)

    def softmax_pv(vt, s_ref):
        for g in range(group):
            s = s_ref[g]
            m_prev = m_ref[g]
            m_new = jnp.maximum(m_prev, jnp.max(s, axis=0, keepdims=True))
            alpha = jnp.exp2(m_prev - m_new)
            p = jnp.exp2(s - m_new).astype(BF16)
            acc_ref[g] = alpha * acc_ref[g] + jnp.dot(vt, p, preferred_element_type=F32)
            m_ref[g] = m_new

    def k_chunk(c):
        return k_ref[pl.ds(pl.multiple_of(c * tk, tk), tk), :]

    def vt_chunk(c):
        return vt_ref[:, pl.ds(pl.multiple_of(c * tk, tk), tk)]

    n_pairs = (n_chunks - 1) // 2
    scores(k_chunk(0), sa_ref)

    def pair(i, carry):
        c = 2 * i
        scores(k_chunk(c + 1), sb_ref)
        softmax_pv(vt_chunk(c), sa_ref)
        scores(k_chunk(c + 2), sa_ref)
        softmax_pv(vt_chunk(c + 1), sb_ref)
        return carry

    lax.fori_loop(0, n_pairs, pair, 0)
    c = 2 * n_pairs
    if n_chunks - c == 2:
        scores(k_chunk(c + 1), sb_ref)
    softmax_pv(vt_chunk(c), sa_ref)
    if has_extra:
        scores(kx_ref[...], sx_ref)
    if n_chunks - c == 2:
        softmax_pv(vt_chunk(c + 1), sb_ref)
    if has_extra:
        softmax_pv(vxt_ref[...], sx_ref)
    for g in range(group):
        o = acc_ref[g, :HEAD_DIM, :] / acc_ref[g, HEAD_DIM:HEAD_DIM + 1, :]
        o_ref[:, g * HEAD_DIM:(g + 1) * HEAD_DIM] = o.T.astype(BF16)


def attention(qt, k, vt, extra, *, n_kv, tq, tk):
    q_w, R = qt.shape
    group = q_w // HEAD_DIM // n_kv
    gw = group * HEAD_DIM
    assert R % tq == 0 and R % tk == 0
    in_specs = [
        pl.BlockSpec((gw, tq), lambda h, i: (h, i)),
        pl.BlockSpec((R, HEAD_DIM), lambda h, i: (0, h)),
        pl.BlockSpec((VT_ROWS, R), lambda h, i: (h, 0)),
    ]
    args = [qt, k, vt]
    scratch = [
        pltpu.VMEM((group, 1, tq), F32),
        pltpu.VMEM((group, VT_ROWS, tq), F32),
        pltpu.VMEM((group, tk, tq), F32),
        pltpu.VMEM((group, tk, tq), F32),
    ]
    if extra is not None:
        rx = extra[0].shape[0]
        in_specs += [
            pl.BlockSpec((rx, HEAD_DIM), lambda h, i: (0, h)),
            pl.BlockSpec((VT_ROWS, rx), lambda h, i: (h, 0)),
        ]
        args += list(extra)
        scratch.append(pltpu.VMEM((group, rx, tq), F32))
    kern = functools.partial(_attn_kernel, group=group, tk=tk, n_chunks=R // tk, has_extra=extra is not None)
    return pl.pallas_call(
        kern,
        grid=(n_kv, R // tq),
        in_specs=in_specs,
        out_specs=pl.BlockSpec((tq, gw), lambda h, i: (i, h)),
        out_shape=jax.ShapeDtypeStruct((R, q_w), BF16),
        scratch_shapes=scratch,
        compiler_params=_cparams("parallel", "parallel"),
        name="attention",
    )(*args)


def _attn_bounded_kernel(bound_ref, qt_ref, k_ref, vt_ref, kx_ref, vxt_ref, o_ref, acc_ref, sa_ref, sb_ref, sx_ref,
                         *, group, tk, n_chunks):
    acc_ref[...] = jnp.zeros(acc_ref.shape, F32)
    bound = bound_ref[0]

    def scores(k, s_ref, g):
        s_ref[g] = jnp.dot(k, qt_ref[g * HEAD_DIM:(g + 1) * HEAD_DIM, :], preferred_element_type=F32)

    def exp_pv(vt, s_ref, g):
        p = jnp.exp2(s_ref[g] - bound).astype(BF16)
        acc_ref[g] += jnp.dot(vt, p, preferred_element_type=F32)

    def k_chunk(c):
        return k_ref[pl.ds(pl.multiple_of(c * tk, tk), tk), :]

    def vt_chunk(c):
        return vt_ref[:, pl.ds(pl.multiple_of(c * tk, tk), tk)]

    def step(k_next, s_next, vt_cur, s_cur):
        for g in range(group):
            if k_next is not None:
                scores(k_next, s_next, g)
            exp_pv(vt_cur, s_cur, g)

    n_pairs = (n_chunks - 1) // 2
    for g in range(group):
        scores(k_chunk(0), sa_ref, g)

    def pair(i, carry):
        c = 2 * i
        step(k_chunk(c + 1), sb_ref, vt_chunk(c), sa_ref)
        step(k_chunk(c + 2), sa_ref, vt_chunk(c + 1), sb_ref)
        return carry

    lax.fori_loop(0, n_pairs, pair, 0)
    c = 2 * n_pairs
    if n_chunks - c == 2:
        step(k_chunk(c + 1), sb_ref, vt_chunk(c), sa_ref)
        step(kx_ref[...], sx_ref, vt_chunk(c + 1), sb_ref)
    else:
        step(kx_ref[...], sx_ref, vt_chunk(c), sa_ref)
    step(None, None, vxt_ref[...], sx_ref)
    for g in range(group):
        o = acc_ref[g, :HEAD_DIM, :] / acc_ref[g, HEAD_DIM:HEAD_DIM + 1, :]
        o_ref[:, g * HEAD_DIM:(g + 1) * HEAD_DIM] = o.T.astype(BF16)


def attention_bounded(qt, k, vt, extra, bound, *, n_kv, tq, tk):
    q_w, R = qt.shape
    group = q_w // HEAD_DIM // n_kv
    gw = group * HEAD_DIM
    rx = extra[0].shape[0]
    assert R % tq == 0 and R % tk == 0
    kern = functools.partial(_attn_bounded_kernel, group=group, tk=tk, n_chunks=R // tk)
    return pl.pallas_call(
        kern,
        grid=(n_kv, R // tq),
        in_specs=[
            pl.BlockSpec(memory_space=pltpu.SMEM),
            pl.BlockSpec((gw, tq), lambda h, i: (h, i)),
            pl.BlockSpec((R, HEAD_DIM), lambda h, i: (0, h)),
            pl.BlockSpec((VT_ROWS, R), lambda h, i: (h, 0)),
            pl.BlockSpec((rx, HEAD_DIM), lambda h, i: (0, h)),
            pl.BlockSpec((VT_ROWS, rx), lambda h, i: (h, 0)),
        ],
        out_specs=pl.BlockSpec((tq, gw), lambda h, i: (i, h)),
        out_shape=jax.ShapeDtypeStruct((R, q_w), BF16),
        scratch_shapes=[pltpu.VMEM((group, VT_ROWS, tq), F32), pltpu.VMEM((group, tk, tq), F32),
                        pltpu.VMEM((group, tk, tq), F32), pltpu.VMEM((group, rx, tq), F32)],
        compiler_params=_cparams("parallel", "parallel"),
        name="attention_bounded",
    )(bound, qt, k, vt, *extra)


def _dft_mats(n, scale, repeat=1):
    k = np.arange(n)
    ang = 2.0 * np.pi * ((k[:, None] * k[None, :]) % n) / n
    eye = np.eye(repeat)
    return tuple(jnp.asarray(np.kron(m * scale, eye), BF16) for m in (np.cos(ang), np.sin(ang)))


def _four_small_kernel(u_ref, cn_ref, sn_ref, cc_ref, sc_ref, o_ref, *, gc):
    u = u_ref[...].astype(BF16)
    for g in range(u.shape[1] // gc):
        ug = u[:, g * gc:(g + 1) * gc]
        a = jnp.dot(ug, cc_ref[...], preferred_element_type=F32).astype(BF16)
        b = jnp.dot(ug, sc_ref[...], preferred_element_type=F32).astype(BF16)
        o = (jnp.dot(cn_ref[...], a, preferred_element_type=F32)
             - jnp.dot(sn_ref[...], b, preferred_element_type=F32))
        o_ref[:, g * gc:(g + 1) * gc] = o.astype(BF16)


def fourier_mix_small(u):
    n, c = u.shape
    gc = c // N_FOURIER_GROUPS
    cn, sn = _dft_mats(n, n ** -0.5)
    cc, sc = _dft_mats(gc, gc ** -0.5)
    full = lambda a: pl.BlockSpec(a.shape, lambda: (0,) * a.ndim)
    return pl.pallas_call(
        functools.partial(_four_small_kernel, gc=gc),
        in_specs=[full(u), full(cn), full(sn), full(cc), full(sc)],
        out_specs=pl.BlockSpec((n, c), lambda: (0, 0)),
        out_shape=jax.ShapeDtypeStruct((n, c), BF16),
        compiler_params=pltpu.CompilerParams(vmem_limit_bytes=VMEM_LIMIT_BYTES),
        name="fourier_small",
    )(u, cn, sn, cc, sc)


def _four_stage1_kernel(u_ref, ca_ref, sa_ref, cw_ref, sw_ref, yr_ref, yi_ref, *, tb):
    na, _, c = u_ref.shape
    u = u_ref[...].reshape(na * tb, c).astype(BF16)
    yr = jnp.dot(ca_ref[...], u, preferred_element_type=F32)
    yi = -jnp.dot(sa_ref[...], u, preferred_element_type=F32)
    cw = cw_ref[...].reshape(na * tb, LANES)
    sw = sw_ref[...].reshape(na * tb, LANES)
    for l in range(c // LANES):
        sl = slice(l * LANES, (l + 1) * LANES)
        r, i = yr[:, sl], yi[:, sl]
        yr_ref[:, :, sl] = (r * cw + i * sw).reshape(na, tb, LANES)
        yi_ref[:, :, sl] = (i * cw - r * sw).reshape(na, tb, LANES)


def _four_stage2_kernel(yr_ref, yi_ref, cb_ref, sb_ref, cc_ref, sc_ref, o_ref, *, nb, c, gc, tk1):
    for t in range(tk1):
        yr = yr_ref[t * nb:(t + 1) * nb, :].astype(BF16)
        yi = yi_ref[t * nb:(t + 1) * nb, :].astype(BF16)
        pr = (jnp.dot(cb_ref[...], yr, preferred_element_type=F32)
              + jnp.dot(sb_ref[...], yi, preferred_element_type=F32)).astype(BF16)
        pi = (jnp.dot(cb_ref[...], yi, preferred_element_type=F32)
              - jnp.dot(sb_ref[...], yr, preferred_element_type=F32)).astype(BF16)
        for g in range(c // gc):
            sl = slice(g * gc, (g + 1) * gc)
            o = (jnp.dot(pr[:, sl], cc_ref[...], preferred_element_type=F32)
                 + jnp.dot(pi[:, sl], sc_ref[...], preferred_element_type=F32))
            o_ref[:, t, g * gc:(g + 1) * gc] = o


def fourier_mix_large(u, *, na=128, tb=SUBLANES, tk1=SUBLANES):
    n, c = u.shape
    nb = n // na
    assert na * nb == n and nb % tb == 0 and na % tk1 == 0 and nb % SUBLANES == 0
    gc = c // N_FOURIER_GROUPS
    ca_k, sa_k = _dft_mats(na, na ** -0.5, repeat=tb)
    cb, sb = _dft_mats(nb, nb ** -0.5)
    cc, sc = _dft_mats(gc, gc ** -0.5)
    k1 = lax.broadcasted_iota(jnp.int32, (na, nb, LANES), 0)
    b = lax.broadcasted_iota(jnp.int32, (na, nb, LANES), 1)
    ang = ((k1 * b) % n).astype(F32) * (2.0 * math.pi / n)
    cw, sw = jnp.cos(ang), jnp.sin(ang)
    const = lambda a: pl.BlockSpec(a.shape, lambda i: (0,) * a.ndim)
    yr, yi = pl.pallas_call(
        functools.partial(_four_stage1_kernel, tb=tb),
        grid=(nb // tb,),
        in_specs=[
            pl.BlockSpec((na, tb, c), lambda i: (0, i, 0)),
            const(ca_k), const(sa_k),
            pl.BlockSpec((na, tb, LANES), lambda i: (0, i, 0)),
            pl.BlockSpec((na, tb, LANES), lambda i: (0, i, 0)),
        ],
        out_specs=[pl.BlockSpec((na, tb, c), lambda i: (0, i, 0))] * 2,
        out_shape=[jax.ShapeDtypeStruct((na, nb, c), F32)] * 2,
        compiler_params=_cparams("parallel"),
        name="fourier_stage1",
    )(u.reshape(na, nb, c), ca_k, sa_k, cw, sw)
    out = pl.pallas_call(
        functools.partial(_four_stage2_kernel, nb=nb, c=c, gc=gc, tk1=tk1),
        grid=(na // tk1,),
        in_specs=[
            pl.BlockSpec((tk1 * nb, c), lambda i: (i, 0)),
            pl.BlockSpec((tk1 * nb, c), lambda i: (i, 0)),
            const(cb), const(sb), const(cc), const(sc),
        ],
        out_specs=pl.BlockSpec((nb, tk1, c), lambda i: (0, i, 0)),
        out_shape=jax.ShapeDtypeStruct((nb, na, c), F32),
        compiler_params=_cparams("parallel"),
        name="fourier_stage2",
    )(yr.reshape(na * nb, c), yi.reshape(na * nb, c), cb, sb, cc, sc)
    return out.reshape(n, c)


def fourier_mix(u):
    n = u.shape[0]
    if n % (128 * SUBLANES) == 0 and n >= 128 * 16:
        return fourier_mix_large(u)
    return fourier_mix_small(u)


def _merge_kernel(x_ref, attn_ref, four_ref, gates_ref, vec_ref, wa_ref, wf_ref, wo_ref, o_ref):
    d = x_ref.shape[1]
    a = jnp.dot(attn_ref[...], wa_ref[...], preferred_element_type=F32)
    f = jnp.dot(four_ref[...].astype(BF16), wf_ref[...], preferred_element_type=F32)
    y = gates_ref[:, :d].astype(F32) * a + gates_ref[:, d:].astype(F32) * f
    y = jnp.dot(y.astype(BF16), wo_ref[...], preferred_element_type=F32)
    o_ref[...] = x_ref[...] + vec_ref[0:1] * y


def merge(x, attn, four, gates, vec, wa, wf, wo, layer, *, tm):
    R, D = x.shape
    row = lambda w: pl.BlockSpec((tm, w), lambda i: (i, 0))
    const = lambda a: pl.BlockSpec((None,) + a.shape[1:], lambda i: (layer, 0, 0), pipeline_mode=pl.Buffered(1))
    return pl.pallas_call(
        _merge_kernel,
        grid=(R // tm,),
        in_specs=[row(D), row(attn.shape[1]), row(four.shape[1]), row(gates.shape[1]),
                  pl.BlockSpec((SUBLANES, D), lambda i: (0, 0)), const(wa), const(wf), const(wo)],
        out_specs=row(D),
        out_shape=jax.ShapeDtypeStruct((R, D), F32),
        compiler_params=_cparams("parallel"),
        name="merge",
    )(x, attn, four, gates, vec, wa, wf, wo)


def _ffn_kernel(x_ref, vec_ref, wg_ref, wu_ref, wd_ref, o_ref, hs_ref, acc_ref):
    j = pl.program_id(1)

    @pl.when(j == 0)
    def _():
        hs_ref[...] = _norm_mod(x_ref[...], vec_ref[0:1], vec_ref[1:2], vec_ref[2:3]).astype(BF16)
        acc_ref[...] = jnp.zeros(acc_ref.shape, F32)

    h = hs_ref[...]
    g = jnp.dot(h, wg_ref[...], preferred_element_type=F32)
    u = jnp.dot(h, wu_ref[...], preferred_element_type=F32)
    acc_ref[...] += jnp.dot((_silu(g) * u).astype(BF16), wd_ref[...], preferred_element_type=F32)

    @pl.when(j == pl.num_programs(1) - 1)
    def _():
        o_ref[...] = x_ref[...] + vec_ref[3:4] * acc_ref[...]


def dense_ffn(x, vec, wg, wu, wd, *, tm, tf):
    R, D = x.shape
    F = wg.shape[1]
    assert R % tm == 0 and F % tf == 0
    return pl.pallas_call(
        _ffn_kernel,
        grid=(R // tm, F // tf),
        in_specs=[
            pl.BlockSpec((tm, D), lambda i, j: (i, 0)),
            pl.BlockSpec((SUBLANES, D), lambda i, j: (0, 0)),
            pl.BlockSpec((D, tf), lambda i, j: (0, j)),
            pl.BlockSpec((D, tf), lambda i, j: (0, j)),
            pl.BlockSpec((tf, D), lambda i, j: (j, 0)),
        ],
        out_specs=pl.BlockSpec((tm, D), lambda i, j: (i, 0)),
        out_shape=jax.ShapeDtypeStruct((R, D), F32),
        scratch_shapes=[pltpu.VMEM((tm, D), BF16), pltpu.VMEM((tm, D), F32)],
        compiler_params=_cparams("parallel", "arbitrary"),
        name="dense_ffn",
    )(x, vec, wg, wu, wd)


def _router_kernel(x_ref, vec_ref, rhi_ref, rlo_ref, h_ref, meta_ref, cnt_ref, run_ref, *, n_exp):
    i = pl.program_id(0)

    @pl.when(i == 0)
    def _():
        run_ref[...] = jnp.zeros(run_ref.shape, F32)

    h = _norm_mod(x_ref[...], vec_ref[0:1], vec_ref[1:2], vec_ref[2:3])
    h_ref[...] = h
    tm = h.shape[0]
    h_hi = h.astype(BF16)
    h_lo = (h - h_hi.astype(F32)).astype(BF16)
    logits = (jnp.dot(h_hi, rhi_ref[...], preferred_element_type=F32)
              + jnp.dot(h_lo, rhi_ref[...], preferred_element_type=F32)
              + jnp.dot(h_hi, rlo_ref[...], preferred_element_type=F32))
    lane = lax.broadcasted_iota(jnp.int32, (tm, LANES), 1)
    lg = jnp.where(lane < n_exp, logits, NEG_INF)
    m1 = jnp.max(lg, axis=-1, keepdims=True)
    i1 = jnp.min(jnp.where(lg == m1, lane, LANES), axis=-1, keepdims=True)
    lg = jnp.where(lane == i1, NEG_INF, lg)
    m2 = jnp.max(lg, axis=-1, keepdims=True)
    i2 = jnp.min(jnp.where(lg == m2, lane, LANES), axis=-1, keepdims=True)
    e2 = jnp.exp(m2 - m1)
    w1 = 1.0 / (1.0 + e2)
    w2 = e2 / (1.0 + e2)

    hit1 = lane == i1
    hit2 = lane == i2
    onehot = (hit1 | hit2).astype(F32)
    r = lax.broadcasted_iota(jnp.int32, (tm, tm), 0)
    cidx = lax.broadcasted_iota(jnp.int32, (tm, tm), 1)
    ltri = (cidx < r).astype(BF16)
    before = jnp.dot(ltri, onehot.astype(BF16), preferred_element_type=F32) + run_ref[0:1]
    rank1 = jnp.sum(jnp.where(hit1, before, 0.0), axis=-1, keepdims=True)
    rank2 = jnp.sum(jnp.where(hit2, before, 0.0), axis=-1, keepdims=True)
    run_ref[0:1] = run_ref[0:1] + jnp.sum(onehot, axis=0, keepdims=True)

    meta = jnp.where(lane == 0, i1.astype(F32), 0.0)
    meta = jnp.where(lane == 1, i2.astype(F32), meta)
    meta = jnp.where(lane == 2, w1, meta)
    meta = jnp.where(lane == 3, w2, meta)
    meta = jnp.where(lane == 4, rank1, meta)
    meta = jnp.where(lane == 5, rank2, meta)
    meta_ref[...] = meta
    cnt_ref[...] = run_ref[...]


def moe_router(x, vec, router_w, *, tm):
    R, D = x.shape
    n_exp = router_w.shape[1]
    rw = jnp.zeros((D, LANES), F32).at[:, :n_exp].set(router_w)
    rhi = rw.astype(BF16)
    rlo = (rw - rhi.astype(F32)).astype(BF16)
    return pl.pallas_call(
        functools.partial(_router_kernel, n_exp=n_exp),
        grid=(R // tm,),
        in_specs=[
            pl.BlockSpec((tm, D), lambda i: (i, 0)),
            pl.BlockSpec((SUBLANES, D), lambda i: (0, 0)),
            pl.BlockSpec(rhi.shape, lambda i: (0, 0)),
            pl.BlockSpec(rlo.shape, lambda i: (0, 0)),
        ],
        out_specs=[
            pl.BlockSpec((tm, D), lambda i: (i, 0)),
            pl.BlockSpec((tm, LANES), lambda i: (i, 0)),
            pl.BlockSpec((SUBLANES, LANES), lambda i: (0, 0)),
        ],
        out_shape=[
            jax.ShapeDtypeStruct((R, D), F32),
            jax.ShapeDtypeStruct((R, LANES), F32),
            jax.ShapeDtypeStruct((SUBLANES, LANES), F32),
        ],
        scratch_shapes=[pltpu.VMEM((SUBLANES, LANES), F32)],
        compiler_params=_cparams("arbitrary"),
        name="moe_router",
    )(x, vec, rhi, rlo)


def _row_copy(src_ref, s, dst_ref, d, sem):
    return pltpu.make_async_copy(src_ref.at[pl.ds(s, 1)], dst_ref.at[pl.ds(d, 1)], sem)


def _dispatch_kernel(dest_ref, ends_ref, h_ref, xs_ref, zero_ref, sem, zero_sem, *, tm_group):
    tm = h_ref.shape[0]
    base = pl.program_id(0) * tm

    @pl.when(pl.program_id(0) == 0)
    def _():
        zero_ref[...] = jnp.zeros(zero_ref.shape, F32)

        def fill(e):
            prev = ends_ref[jnp.maximum(e - 1, 0)]
            start = pl.multiple_of(ends_ref[e] - tm_group, tm_group)
            nonempty = (ends_ref[e] > prev) | ((e == 0) & (ends_ref[e] > 0))
            return nonempty, pltpu.make_async_copy(zero_ref, xs_ref.at[pl.ds(start, tm_group)], zero_sem)

        for e in range(ends_ref.shape[0]):
            nonempty, copy = fill(e)
            pl.when(nonempty)(copy.start)
        for e in range(ends_ref.shape[0]):
            nonempty, copy = fill(e)
            pl.when(nonempty)(copy.wait)

        def fill_tail(t, c):
            copy = pltpu.make_async_copy(
                zero_ref, xs_ref.at[pl.ds(pl.multiple_of(t * tm_group, tm_group), tm_group)], zero_sem)
            copy.start()
            copy.wait()
            return c

        n_used = ends_ref[ends_ref.shape[0] - 1] // tm_group
        lax.fori_loop(n_used, xs_ref.shape[0] // tm_group, fill_tail, 0)

    def start(t, c):
        for s in range(TOP_K):
            _row_copy(h_ref, t, xs_ref, dest_ref[TOP_K * (base + t) + s], sem).start(priority=s % 2)
        return c

    def wait(t, c):
        for s in range(TOP_K):
            _row_copy(h_ref, t, xs_ref, dest_ref[TOP_K * (base + t) + s], sem).wait()
        return c

    lax.fori_loop(0, tm, start, 0, unroll=ROW_DMA_UNROLL)
    lax.fori_loop(0, tm, wait, 0, unroll=ROW_DMA_UNROLL)


def moe_dispatch(h, dest, ends, n_rows, *, tm, tm_group):
    R, D = h.shape
    return pl.pallas_call(
        functools.partial(_dispatch_kernel, tm_group=tm_group),
        grid_spec=pltpu.PrefetchScalarGridSpec(
            num_scalar_prefetch=2,
            grid=(R // tm,),
            in_specs=[pl.BlockSpec((tm, D), lambda i, dest, ends: (i, 0))],
            out_specs=pl.BlockSpec(memory_space=pl.ANY),
            scratch_shapes=[pltpu.VMEM((tm_group, D), F32), pltpu.SemaphoreType.DMA(()),
                            pltpu.SemaphoreType.DMA(())],
        ),
        out_shape=jax.ShapeDtypeStruct((n_rows, D), F32),
        compiler_params=_cparams("arbitrary"),
        name="moe_dispatch",
    )(dest, ends, h)


def _gffn_up_kernel(te_ref, nv_ref, xs_ref, wg_ref, wu_ref, a_ref, *, tc):
    valid = pl.program_id(0) < nv_ref[0]

    @pl.when(valid)
    def _():
        h = xs_ref[...].astype(BF16)
        n_cols = a_ref.shape[1]
        for c0 in range(0, n_cols, tc):
            c1 = min(c0 + tc, n_cols)
            g = jnp.dot(h, wg_ref[0, :, c0:c1], preferred_element_type=F32)
            u = jnp.dot(h, wu_ref[0, :, c0:c1], preferred_element_type=F32)
            a_ref[:, c0:c1] = (_silu(g) * u).astype(BF16)

    @pl.when(jnp.logical_not(valid))
    def _():
        a_ref[...] = jnp.zeros(a_ref.shape, BF16)


def _gffn_down_kernel(te_ref, nv_ref, a_ref, wd_ref, ys_ref):
    valid = pl.program_id(0) < nv_ref[0]

    @pl.when(valid)
    def _():
        ys_ref[...] = jnp.dot(a_ref[...], wd_ref[0], preferred_element_type=F32)

    @pl.when(jnp.logical_not(valid))
    def _():
        ys_ref[...] = jnp.zeros(ys_ref.shape, F32)


def moe_grouped_ffn(xs, tile_expert, n_valid, wg, wu, wd, *, tm, tc=512):
    n_rows, D = xs.shape
    F = wg.shape[2]
    assert n_rows % tm == 0 and F % LANES == 0
    expert_w = lambda shape: pl.BlockSpec(shape, lambda t, te, nv: (te[t], 0, 0), pipeline_mode=pl.Buffered(1))
    a = pl.pallas_call(
        functools.partial(_gffn_up_kernel, tc=tc),
        grid_spec=pltpu.PrefetchScalarGridSpec(
            num_scalar_prefetch=2,
            grid=(n_rows // tm,),
            in_specs=[
                pl.BlockSpec((tm, D), lambda t, te, nv: (jnp.minimum(t, jnp.maximum(nv[0] - 1, 0)), 0)),
                expert_w((1, D, F)),
                expert_w((1, D, F)),
            ],
            out_specs=pl.BlockSpec((tm, F), lambda t, te, nv: (t, 0)),
        ),
        out_shape=jax.ShapeDtypeStruct((n_rows, F), BF16),
        compiler_params=_cparams("parallel"),
        name="moe_ffn_up",
    )(tile_expert, n_valid, xs, wg, wu)
    return pl.pallas_call(
        _gffn_down_kernel,
        grid_spec=pltpu.PrefetchScalarGridSpec(
            num_scalar_prefetch=2,
            grid=(n_rows // tm,),
            in_specs=[
                pl.BlockSpec((tm, F), lambda t, te, nv: (t, 0)),
                pl.BlockSpec((1, F, D), lambda t, te, nv: (te[t], 0, 0)),
            ],
            out_specs=pl.BlockSpec((tm, D), lambda t, te, nv: (t, 0)),
        ),
        out_shape=jax.ShapeDtypeStruct((n_rows, D), F32),
        compiler_params=_cparams("parallel"),
        name="moe_ffn_down",
    )(tile_expert, n_valid, a, wd)


def _combine_kernel(dest_ref, x_ref, meta_ref, vec_ref, ys_ref, o_ref, b0_ref, b1_ref, sem, *, final_norm):
    tm = x_ref.shape[0]
    base = pl.program_id(0) * tm
    bufs = (b0_ref, b1_ref)

    def start(t, c):
        for s in range(TOP_K):
            _row_copy(ys_ref, dest_ref[TOP_K * (base + t) + s], bufs[s], t, sem).start(priority=s % 2)
        return c

    def wait(t, c):
        for s in range(TOP_K):
            _row_copy(ys_ref, dest_ref[TOP_K * (base + t) + s], bufs[s], t, sem).wait()
        return c

    lax.fori_loop(0, tm, start, 0, unroll=ROW_DMA_UNROLL)
    lax.fori_loop(0, tm, wait, 0, unroll=ROW_DMA_UNROLL)
    y = meta_ref[:, 2:3] * b0_ref[...] + meta_ref[:, 3:4] * b1_ref[...]
    out = x_ref[...] + vec_ref[3:4] * y
    if final_norm:
        ms = jnp.mean(out * out, axis=-1, keepdims=True)
        out = out * lax.rsqrt(ms + EPS) * vec_ref[4:5]
    o_ref[...] = out


def moe_combine(x, meta, vec, ys, dest, *, tm, final_norm):
    R, D = x.shape
    return pl.pallas_call(
        functools.partial(_combine_kernel, final_norm=final_norm),
        grid_spec=pltpu.PrefetchScalarGridSpec(
            num_scalar_prefetch=1,
            grid=(R // tm,),
            in_specs=[
                pl.BlockSpec((tm, D), lambda i, dest: (i, 0)),
                pl.BlockSpec((tm, LANES), lambda i, dest: (i, 0)),
                pl.BlockSpec((SUBLANES, D), lambda i, dest: (0, 0)),
                pl.BlockSpec(memory_space=pl.ANY),
            ],
            out_specs=pl.BlockSpec((tm, D), lambda i, dest: (i, 0)),
            scratch_shapes=[pltpu.VMEM((tm, D), F32), pltpu.VMEM((tm, D), F32),
                            pltpu.SemaphoreType.DMA(())],
        ),
        out_shape=jax.ShapeDtypeStruct((R, D), F32),
        compiler_params=_cparams("arbitrary"),
        name="moe_combine",
    )(dest, x, meta, vec, ys)


def moe_layer(x, vec, router_w, wg, wu, wd, *, final_norm, tm_route, tm_rows, tm_group):
    R, D = x.shape
    n_exp = router_w.shape[1]
    h, meta, counts = moe_router(x, vec, router_w, tm=tm_route)
    counts = counts[0, :n_exp].astype(jnp.int32)
    padded = ((counts + tm_group - 1) // tm_group) * tm_group
    ends = jnp.cumsum(padded)
    offsets = ends - padded
    eid = meta[:, 0:TOP_K].astype(jnp.int32)
    rank = meta[:, 4:4 + TOP_K].astype(jnp.int32)
    dest = (offsets[eid] + rank).reshape(-1)
    n_tiles = (TOP_K * R) // tm_group + n_exp
    tile_start = jnp.arange(n_tiles, dtype=jnp.int32) * tm_group
    n_valid = ends[-1] // tm_group
    tile_expert = jnp.sum(tile_start[:, None] >= ends[None, :], axis=1).astype(jnp.int32)
    last_expert = jnp.take(tile_expert, jnp.maximum(n_valid - 1, 0))
    tile_expert = jnp.where(jnp.arange(n_tiles) < n_valid, tile_expert, last_expert)
    xs = moe_dispatch(h, dest, ends.astype(jnp.int32), n_tiles * tm_group, tm=tm_rows, tm_group=tm_group)
    ys = moe_grouped_ffn(xs, tile_expert, n_valid.reshape(1).astype(jnp.int32), wg, wu, wd,
                         tm=tm_group)
    return moe_combine(x, meta, vec, ys, dest, tm=tm_rows, final_norm=final_norm)


def _final_norm_kernel(x_ref, g_ref, o_ref):
    x = x_ref[...]
    ms = jnp.mean(x * x, axis=-1, keepdims=True)
    o_ref[...] = x * lax.rsqrt(ms + EPS) * g_ref[0:1]


def final_norm(x, vec, *, tm):
    R, D = x.shape
    return pl.pallas_call(
        _final_norm_kernel,
        grid=(R // tm,),
        in_specs=[pl.BlockSpec((tm, D), lambda i: (i, 0)), pl.BlockSpec((SUBLANES, D), lambda i: (0, 0))],
        out_specs=pl.BlockSpec((tm, D), lambda i: (i, 0)),
        out_shape=jax.ShapeDtypeStruct((R, D), F32),
        compiler_params=_cparams("parallel"),
        name="final_norm",
    )(x, vec)


def _rope_tables(n_tokens):
    rows = n_tokens // GRID_W
    row = jnp.repeat(jnp.arange(rows, dtype=F32), GRID_W)
    col = jnp.tile(jnp.arange(GRID_W, dtype=F32), rows)
    inv = ROPE_THETA ** (-jnp.arange(ROPE_FREQS, dtype=F32) / ROPE_FREQS)
    ar = row[:, None] * inv
    ac = col[:, None] * inv
    ang = jnp.concatenate([ar, ar, ac, ac], axis=-1)
    cos, sin = jnp.cos(ang), jnp.sin(ang)
    lane = jnp.arange(HEAD_DIM) % (2 * ROPE_FREQS)
    lo = lane < ROPE_FREQS
    return cos, jnp.where(lo, -sin, 0.0), jnp.where(lo, 0.0, sin)


def _vec(*rows):
    d = rows[0].shape[-1]
    out = jnp.zeros((SUBLANES, d), F32)
    for i, r in enumerate(rows):
        out = out.at[i].set(r.reshape(d))
    return out


ROWS_RESIDENT_W = 256
ROWS_STREAMED_W = 512
FFN_COLS = 512
ATTN_Q = 512
ATTN_K = 512
ROWS_ROW_DMA = 512
EXPERT_ROWS = 512


def _tile(n, pref):
    t = pref
    while n % t:
        t //= 2
    return t


def kernel(x, c, ctx, c_ctx, ada_w, ada_b, norm_attn_g, norm_ffn_g, w_in, b_gate, q_norm_g, k_norm_g,
           w_attn_proj, w_four_proj, w_out, ffn_w_gate, ffn_w_up, ffn_w_down, router_w, moe_w_gate,
           moe_w_up, moe_w_down, final_norm_g):
    B, S, D = x.shape
    assert B == 1 and c.shape[0] == 1 and ctx.shape[0] == 1
    depth = ada_w.shape[0]
    q_w = w_attn_proj.shape[1]
    four_w = w_four_proj.shape[1]
    kv_w = (w_in.shape[2] - q_w - four_w - 2 * D) // 2
    n_kv = kv_w // HEAD_DIM

    cond = jnp.zeros((SUBLANES, D), F32).at[0].set(c[0]).at[1].set(c_ctx)
    mods = ada_modulation(cond, ada_w, ada_b)
    rope = _rope_tables(S)
    w_in_bf, wa_bf, wf_bf, wo_bf = (w.astype(BF16) for w in (w_in, w_attn_proj, w_four_proj, w_out))
    xl = x[0]
    xc = ctx[0]
    C = xc.shape[0]

    def mod(l, r, k):
        return mods[l, r, k * D:(k + 1) * D]

    def project(l, xs_, r, rope_, tm):
        vec = _vec(norm_attn_g[l], mod(l, r, 0), mod(l, r, 1))
        qkg = _vec(q_norm_g[l], k_norm_g[l])
        return in_projection(xs_, vec, w_in_bf, l, b_gate[l].reshape(1, -1), qkg, rope_,
                             q_w=q_w, kv_w=kv_w, four_w=four_w, tm=tm)

    def finish_mixer(l, xs_, r, proj, extra, tm):
        qt, k, vt, f, gates = proj
        n = xs_.shape[0]
        tq, tk = _tile(n, ATTN_Q), _tile(n, ATTN_K)
        if extra is None:
            attn = attention(qt, k, vt, None, n_kv=n_kv, tq=tq, tk=tk)
        else:
            bound = (SCORE_BOUND_SLACK * HEAD_DIM * ATTN_SCALE * LOG2E
                     * jnp.max(jnp.abs(q_norm_g[l])) * jnp.max(jnp.abs(k_norm_g[l]))).reshape(1)
            attn = lax.cond(
                bound[0] < MAX_SCORE_BOUND,
                lambda: attention_bounded(qt, k, vt, extra, bound, n_kv=n_kv, tq=tq, tk=tk),
                lambda: attention(qt, k, vt, extra, n_kv=n_kv, tq=tq, tk=tk))
        four = fourier_mix(f)
        return merge(xs_, attn, four, gates, _vec(mod(l, r, 2)), wa_bf, wf_bf, wo_bf, l, tm=tm)

    def channel_mixer(l, xs_, r, fin, tm):
        i = l // 2
        vec = _vec(norm_ffn_g[l], mod(l, r, 3), mod(l, r, 4), mod(l, r, 5), final_norm_g)
        if l % 2 == 0:
            y = dense_ffn(xs_, vec, ffn_w_gate[i].astype(BF16), ffn_w_up[i].astype(BF16),
                          ffn_w_down[i].astype(BF16), tm=tm, tf=FFN_COLS)
            if fin:
                y = final_norm(y, _vec(final_norm_g), tm=tm)
            return y
        return moe_layer(xs_, vec, router_w[i], moe_w_gate[i].astype(BF16), moe_w_up[i].astype(BF16),
                         moe_w_down[i].astype(BF16), final_norm=fin,
                         tm_route=_tile(xs_.shape[0], ROWS_STREAMED_W),
                         tm_rows=_tile(xs_.shape[0], ROWS_ROW_DMA), tm_group=EXPERT_ROWS)

    for l in range(depth):
        last = l == depth - 1
        tmc = _tile(C, ROWS_RESIDENT_W)
        proj_c = project(l, xc, 1, None, tmc)
        if not last:
            xc_next = finish_mixer(l, xc, 1, proj_c, None, tmc)
            xc_next = channel_mixer(l, xc_next, 1, False, tmc)
        proj = project(l, xl, 0, rope, _tile(S, ROWS_RESIDENT_W))
        xl = finish_mixer(l, xl, 0, proj, (proj_c[1], proj_c[2]), _tile(S, ROWS_RESIDENT_W))
        xl = channel_mixer(l, xl, 0, last, _tile(S, ROWS_STREAMED_W))
        if not last:
            xc = xc_next
    return xl[None]
```
